```python
import math
import jax, jax.numpy as jnp
from jax import lax
import numpy as np

D_MODEL = 1024
BATCH = 8
SEQ = 2048
DEPTH = 4
DEC_BATCH = 128
DEC_SEQ = 4
PAST_LEN = 8192
PAGE_SIZE = 128

N_META = 16
N_ATTN = (DEPTH + 1) // 2
N_SSM = DEPTH // 2
WINDOW = 128
BLOCK = WINDOW
HEAD_DIM = 128
ATTN_WIDTH = 2 * D_MODEL
N_HEADS = ATTN_WIDTH // HEAD_DIM
N_KV_HEADS = 4
GQA_GROUP = N_HEADS // N_KV_HEADS
ATTN_PROJ = ATTN_WIDTH + 2 * N_KV_HEADS * HEAD_DIM + ATTN_WIDTH
D_INNER = 2 * D_MODEL
SSM_HEAD_DIM = 64
SSM_HEADS = D_INNER // SSM_HEAD_DIM
D_STATE = 128
SSM_GROUPS = 4
CONV_W = 4
CONV_DIM = D_INNER + 2 * SSM_GROUPS * D_STATE
SSM_PROJ = D_INNER + CONV_DIM + SSM_HEADS
SSM_CHUNK = 128
EPS = 1e-6

kernel_name = "swa_sink_mamba2_interleaved_step"


def rmsnorm(x, w):
    xf = x.astype(jnp.float32)
    y = xf * lax.rsqrt(jnp.mean(xf * xf, axis=-1, keepdims=True) + EPS)
    return (y * w.astype(jnp.float32)).astype(x.dtype)


def attn_project(u, w_in, q_norm_w, k_norm_w):
    b, L, _ = u.shape
    proj = u @ w_in
    q, k, v, g = jnp.split(proj, [ATTN_WIDTH, ATTN_WIDTH + N_KV_HEADS * HEAD_DIM,
                                  ATTN_WIDTH + 2 * N_KV_HEADS * HEAD_DIM], axis=-1)
    q = rmsnorm(q.reshape(b, L, N_KV_HEADS, GQA_GROUP, HEAD_DIM), q_norm_w)
    k = rmsnorm(k.reshape(b, L, N_KV_HEADS, HEAD_DIM), k_norm_w)
    v = v.reshape(b, L, N_KV_HEADS, HEAD_DIM)
    return q, k, v, g


def attend_with_sink(q, k, v, mask, sinks):
    s = jnp.einsum('...qkgd,...skd->...kgqs', q, k).astype(jnp.float32) * (HEAD_DIM ** -0.5)
    s = jnp.where(mask, s, -jnp.inf)
    sink = sinks.reshape(N_KV_HEADS, GQA_GROUP)[..., None, None].astype(jnp.float32)
    m = jnp.maximum(jnp.max(s, axis=-1, keepdims=True), sink)
    p = jnp.exp(s - m)
    p = p / (jnp.sum(p, axis=-1, keepdims=True) + jnp.exp(sink - m))
    return jnp.einsum('...kgqs,...skd->...qkgd', p.astype(v.dtype), v)


def swa_prompt(u, w_in, q_norm_w, k_norm_w, sinks, w_out):
    b, L, _ = u.shape
    q, k, v, g = attn_project(u, w_in, q_norm_w, k_norm_w)
    pad = (-L) % BLOCK
    Lp = L + pad
    nb = Lp // BLOCK
    qb = jnp.pad(q, ((0, 0), (pad, 0), (0, 0), (0, 0), (0, 0))).reshape(b, nb, BLOCK, N_KV_HEADS, GQA_GROUP, HEAD_DIM)
    kb = jnp.pad(k, ((0, 0), (pad, 0), (0, 0), (0, 0))).reshape(b, nb, BLOCK, N_KV_HEADS, HEAD_DIM)
    vb = jnp.pad(v, ((0, 0), (pad, 0), (0, 0), (0, 0))).reshape(b, nb, BLOCK, N_KV_HEADS, HEAD_DIM)
    shift = ((0, 0), (1, 0), (0, 0), (0, 0), (0, 0))
    keys = jnp.concatenate([jnp.pad(kb[:, :-1], shift), kb], axis=2)
    vals = jnp.concatenate([jnp.pad(vb[:, :-1], shift), vb], axis=2)
    pos = jnp.arange(Lp).reshape(nb, BLOCK) - pad
    kpos = jnp.concatenate([pos - BLOCK, pos], axis=1)
    diff = pos[:, :, None] - kpos[:, None, :]
    mask = (diff >= 0) & (diff < WINDOW) & (kpos[:, None, :] >= 0)
    o = attend_with_sink(qb, keys, vals, mask[None, :, None, None], sinks)
    o = o.reshape(b, Lp, ATTN_WIDTH)[:, pad:]
    out = (o * jax.nn.silu(g)) @ w_out
    return out, k[:, -WINDOW:], v[:, -WINDOW:]


def swa_sample(u, ck, cv, w_in, q_norm_w, k_norm_w, sinks, w_out):
    b, S, _ = u.shape
    q, k, v, g = attn_project(u, w_in, q_norm_w, k_norm_w)
    wbuf = ck.shape[1]
    keys = jnp.concatenate([ck.astype(k.dtype), k], axis=1)
    vals = jnp.concatenate([cv.astype(v.dtype), v], axis=1)
    qpos = jnp.arange(S)
    kpos = jnp.arange(wbuf + S) - wbuf
    diff = qpos[:, None] - kpos[None, :]
    mask = (diff >= 0) & (diff < WINDOW)
    o = attend_with_sink(q, keys, vals, mask[None, None, None], sinks)
    o = o.reshape(b, S, ATTN_WIDTH)
    out = (o * jax.nn.silu(g)) @ w_out
    return out, keys[:, -wbuf:], vals[:, -wbuf:]


def ssd_chunked(x, dt, A, B, C, init_state, chunk):
    f32 = jnp.float32
    b, L, h, p = x.shape
    g, n = B.shape[2], B.shape[3]
    e = h // g
    c = L // chunk
    xg = x.reshape(b, c, chunk, g, e, p).astype(f32)
    dtg = dt.reshape(b, c, chunk, g, e).astype(f32)
    Bg = B.reshape(b, c, chunk, g, n).astype(f32)
    Cg = C.reshape(b, c, chunk, g, n).astype(f32)
    a_cum = jnp.cumsum(dtg * A.reshape(g, e).astype(f32), axis=2)
    xdt = xg * dtg[..., None]
    seg = a_cum[:, :, :, None] - a_cum[:, :, None, :]
    causal = jnp.tril(jnp.ones((chunk, chunk), dtype=bool))[:, :, None, None]
    decay = jnp.exp(jnp.where(causal, seg, -jnp.inf))
    cb = jnp.einsum('bcign,bcjgn->bcijg', Cg, Bg)
    y_diag = jnp.einsum('bcijg,bcijge,bcjgep->bcigep', cb, decay, xdt)
    decay_to_end = jnp.exp(a_cum[:, :, -1:] - a_cum)
    chunk_states = jnp.einsum('bcjgn,bcjge,bcjgep->bcgepn', Bg, decay_to_end, xdt)
    chunk_decay = jnp.exp(a_cum[:, :, -1])

    def step(s, inp):
        st, dec = inp
        return s * dec[..., None, None] + st, s

    init = init_state.reshape(b, g, e, p, n).astype(f32)
    final, starts = lax.scan(step, init, (jnp.moveaxis(chunk_states, 1, 0), jnp.moveaxis(chunk_decay, 1, 0)))
    starts = jnp.moveaxis(starts, 0, 1)
    y_off = jnp.einsum('bcign,bcgepn,bcige->bcigep', Cg, starts, jnp.exp(a_cum))
    y = (y_diag + y_off).reshape(b, L, h, p)
    return y, final.reshape(b, h, p, n)


def mamba2_mixer(u, conv_state, ssm_state, w_in, conv_w, conv_b, dt_bias, a_log, d_skip, norm_w, w_out):
    b, L, _ = u.shape
    proj = u @ w_in
    z, xbc, dt = jnp.split(proj, [D_INNER, D_INNER + CONV_DIM], axis=-1)
    xbc_full = jnp.concatenate([conv_state.astype(xbc.dtype), xbc], axis=1)
    conv = lax.conv_general_dilated(xbc_full, conv_w[:, None, :].astype(xbc.dtype), (1,), 'VALID',
                                    dimension_numbers=('NWC', 'WIO', 'NWC'),
                                    feature_group_count=CONV_DIM)
    xbc_act = jax.nn.silu(conv + conv_b)
    new_conv = xbc_full[:, -(CONV_W - 1):]
    xs, Bm, Cm = jnp.split(xbc_act, [D_INNER, D_INNER + SSM_GROUPS * D_STATE], axis=-1)
    xh = xs.reshape(b, L, SSM_HEADS, SSM_HEAD_DIM)
    Bm = Bm.reshape(b, L, SSM_GROUPS, D_STATE)
    Cm = Cm.reshape(b, L, SSM_GROUPS, D_STATE)
    dtf = jax.nn.softplus(dt.astype(jnp.float32) + dt_bias.astype(jnp.float32))
    A = -jnp.exp(a_log.astype(jnp.float32))
    chunk = min(SSM_CHUNK, L)
    pad = (-L) % chunk
    pad4 = ((0, 0), (pad, 0), (0, 0), (0, 0))
    y, final = ssd_chunked(jnp.pad(xh, pad4), jnp.pad(dtf, ((0, 0), (pad, 0), (0, 0))), A,
                           jnp.pad(Bm, pad4), jnp.pad(Cm, pad4), ssm_state, chunk)
    y = y[:, pad:] + d_skip.astype(jnp.float32)[:, None] * xh.astype(jnp.float32)
    y = y.reshape(b, L, D_INNER) * jax.nn.silu(z.astype(jnp.float32))
    yg = y.reshape(b, L, SSM_GROUPS, D_INNER // SSM_GROUPS)
    yg = yg * lax.rsqrt(jnp.mean(yg * yg, axis=-1, keepdims=True) + EPS)
    y = (yg.reshape(b, L, D_INNER) * norm_w.astype(jnp.float32)).astype(u.dtype)
    return y @ w_out, new_conv, final.astype(u.dtype)


def setup_inputs(seed: int = 0) -> dict:
    key = jax.random.key(seed)
    ks = jax.random.split(key, 24)
    f32 = jnp.float32
    w_buf = min(WINDOW, PAST_LEN)
    nrm = lambda k, shape, s=1.0: jax.random.normal(k, shape, f32) * s
    dt0 = jnp.exp(jax.random.uniform(ks[15], (N_SSM, SSM_HEADS), f32) * (math.log(0.1) - math.log(0.001)) + math.log(0.001))
    return {
        "x_prompt": nrm(ks[0], (BATCH, SEQ, D_MODEL)),
        "x_sample": nrm(ks[1], (DEC_BATCH, DEC_SEQ, D_MODEL)),
        "cache_k": nrm(ks[2], (N_ATTN, DEC_BATCH, w_buf, N_KV_HEADS, HEAD_DIM)),
        "cache_v": nrm(ks[3], (N_ATTN, DEC_BATCH, w_buf, N_KV_HEADS, HEAD_DIM)),
        "state_conv": nrm(ks[4], (N_SSM, DEC_BATCH, CONV_W - 1, CONV_DIM)),
        "state_ssm": nrm(ks[5], (N_SSM, DEC_BATCH, SSM_HEADS, SSM_HEAD_DIM, D_STATE), 0.1),
        "meta_tokens": nrm(ks[6], (N_META, D_MODEL)),
        "norm_w": 1.0 + nrm(ks[7], (DEPTH, D_MODEL), 0.02),
        "w_attn_in": nrm(ks[8], (N_ATTN, D_MODEL, ATTN_PROJ), D_MODEL ** -0.5),
        "q_norm_w": 1.0 + nrm(ks[9], (N_ATTN, HEAD_DIM), 0.02),
        "k_norm_w": 1.0 + nrm(ks[10], (N_ATTN, HEAD_DIM), 0.02),
        "attn_sinks": nrm(ks[11], (N_ATTN, N_HEADS), 0.5),
        "w_attn_out": nrm(ks[12], (N_ATTN, ATTN_WIDTH, D_MODEL), ATTN_WIDTH ** -0.5),
        "w_ssm_in": nrm(ks[13], (N_SSM, D_MODEL, SSM_PROJ), D_MODEL ** -0.5),
        "conv_w": nrm(ks[14], (N_SSM, CONV_W, CONV_DIM), CONV_W ** -0.5),
        "conv_b": nrm(ks[16], (N_SSM, CONV_DIM), 0.01),
        "dt_bias": dt0 + jnp.log(-jnp.expm1(-dt0)),
        "a_log": jnp.log(jax.random.uniform(ks[17], (N_SSM, SSM_HEADS), f32, 1.0, 16.0)),
        "d_skip": 1.0 + nrm(ks[18], (N_SSM, SSM_HEADS), 0.02),
        "ssm_norm_w": 1.0 + nrm(ks[19], (N_SSM, D_INNER), 0.02),
        "w_ssm_out": nrm(ks[20], (N_SSM, D_INNER, D_MODEL), D_INNER ** -0.5),
    }


def reference(x_prompt, x_sample, cache_k, cache_v, state_conv, state_ssm, meta_tokens, norm_w,
              w_attn_in, q_norm_w, k_norm_w, attn_sinks, w_attn_out, w_ssm_in, conv_w, conv_b,
              dt_bias, a_log, d_skip, ssm_norm_w, w_ssm_out):
    bp = x_prompt.shape[0]
    meta = jnp.broadcast_to(meta_tokens[None].astype(x_prompt.dtype), (bp, N_META, D_MODEL))
    xp = jnp.concatenate([meta, x_prompt], axis=1)
    xs = x_sample
    kp_l, vp_l, ks_l, vs_l = [], [], [], []
    cp_l, sp_l, cs_l, ss_l = [], [], [], []
    for i in range(DEPTH):
        j = i // 2
        hp = rmsnorm(xp, norm_w[i])
        hs = rmsnorm(xs, norm_w[i])
        if i % 2 == 0:
            op, kp, vp = swa_prompt(hp, w_attn_in[j], q_norm_w[j], k_norm_w[j], attn_sinks[j], w_attn_out[j])
            os_, ks_, vs_ = swa_sample(hs, cache_k[j], cache_v[j], w_attn_in[j], q_norm_w[j], k_norm_w[j],
                                       attn_sinks[j], w_attn_out[j])
            kp_l.append(kp); vp_l.append(vp); ks_l.append(ks_); vs_l.append(vs_)
        else:
            zc = jnp.zeros((bp, CONV_W - 1, CONV_DIM), xp.dtype)
            zs = jnp.zeros((bp, SSM_HEADS, SSM_HEAD_DIM, D_STATE), jnp.float32)
            op, cp, sp = mamba2_mixer(hp, zc, zs, w_ssm_in[j], conv_w[j], conv_b[j], dt_bias[j], a_log[j],
                                      d_skip[j], ssm_norm_w[j], w_ssm_out[j])
            os_, cs, ss = mamba2_mixer(hs, state_conv[j], state_ssm[j], w_ssm_in[j], conv_w[j], conv_b[j],
                                       dt_bias[j], a_log[j], d_skip[j], ssm_norm_w[j], w_ssm_out[j])
            cp_l.append(cp); sp_l.append(sp); cs_l.append(cs); ss_l.append(ss)
        xp = xp + op
        xs = xs + os_
    y_prompt = xp[:, N_META:]
    y_sample = xs
    return (y_prompt, y_sample,
            jnp.stack(kp_l), jnp.stack(vp_l), jnp.stack(cp_l), jnp.stack(sp_l),
            jnp.stack(ks_l), jnp.stack(vs_l), jnp.stack(cs_l), jnp.stack(ss_l))
```

```python
import functools
import math

import jax
import jax.numpy as jnp
from jax import lax
from jax.experimental import pallas as pl
from jax.experimental.pallas import tpu as pltpu

F32 = jnp.float32
BF16 = jnp.bfloat16

D_MODEL = 1024
DEPTH = 4
N_META = 16
WINDOW = 128
BLOCK = 128
HEAD_DIM = 128
ATTN_WIDTH = 2 * D_MODEL
N_HEADS = ATTN_WIDTH // HEAD_DIM
N_KV_HEADS = 4
GQA_GROUP = N_HEADS // N_KV_HEADS
KV_WIDTH = N_KV_HEADS * HEAD_DIM
D_INNER = 2 * D_MODEL
SSM_HEAD_DIM = 64
SSM_HEADS = D_INNER // SSM_HEAD_DIM
D_STATE = 128
SSM_GROUPS = 4
GROUP_WIDTH = D_INNER // SSM_GROUPS
CONV_W = 4
CONV_DIM = D_INNER + 2 * SSM_GROUPS * D_STATE
EPS = 1e-6
NEG = -1e30

LANES = 128
ROW_TILE = 512
SAMPLE_SEQS = 8
V7X_VMEM_BYTES = 64 * 1024 * 1024
VMEM_LIMIT = V7X_VMEM_BYTES * 7 // 8


def _dot(a, b):
    return jnp.dot(a, b, preferred_element_type=F32)


def _dot_nt(a, b):
    return lax.dot_general(a, b, (((1,), (1,)), ((), ())), preferred_element_type=F32)


def _rmsnorm(x, w):
    return x * lax.rsqrt(jnp.mean(x * x, axis=-1, keepdims=True) + EPS) * w


def _silu(x):
    return x * jax.nn.sigmoid(x)


def _softplus(x):
    return jnp.maximum(x, 0.0) + jnp.log1p(jnp.exp(-jnp.abs(x)))


def _split3(x):
    hi = x.astype(BF16)
    r = x - hi.astype(F32)
    mid = r.astype(BF16)
    lo = (r - mid.astype(F32)).astype(BF16)
    return hi, mid, lo


def _exact_left(p, x):
    hi, mid, lo = _split3(x)
    return _dot(p, hi) + _dot(p, mid) + _dot(p, lo)


def _exact_right(x, e):
    hi, mid, lo = _split3(x)
    return _dot(hi, e) + _dot(mid, e) + _dot(lo, e)


def _params(semantics):
    return pltpu.CompilerParams(dimension_semantics=semantics, vmem_limit_bytes=VMEM_LIMIT)


def _const_spec(shape):
    nd = len(shape)
    return pl.BlockSpec(shape, lambda *_: (0,) * nd)


def _row_spec(cols, tm=ROW_TILE):
    return pl.BlockSpec((tm, cols), lambda i: (i, 0))


def _attn_in_body(x_ref, nw_ref, w_ref, qn_ref, kn_ref, q_ref, k_ref, v_ref, g_ref):
    h = _rmsnorm(x_ref[...], nw_ref[...]).astype(BF16)
    q = _dot(h, w_ref[:, :ATTN_WIDTH])
    for hd in range(N_HEADS):
        sl = slice(hd * HEAD_DIM, (hd + 1) * HEAD_DIM)
        q_ref[:, sl] = _rmsnorm(q[:, sl], qn_ref[...]).astype(q_ref.dtype)
    k = _dot(h, w_ref[:, ATTN_WIDTH:ATTN_WIDTH + KV_WIDTH])
    for hd in range(N_KV_HEADS):
        sl = slice(hd * HEAD_DIM, (hd + 1) * HEAD_DIM)
        k_ref[:, sl] = _rmsnorm(k[:, sl], kn_ref[...])
    v_ref[...] = _dot(h, w_ref[:, ATTN_WIDTH + KV_WIDTH:ATTN_WIDTH + 2 * KV_WIDTH])
    g_ref[...] = _dot(h, w_ref[:, ATTN_WIDTH + 2 * KV_WIDTH:]).astype(g_ref.dtype)


def _attn_in(x, nw, w, qn, kn, act_dtype):
    n = x.shape[0]
    return pl.pallas_call(
        _attn_in_body,
        grid=(n // ROW_TILE,),
        in_specs=[_row_spec(D_MODEL), _const_spec(nw.shape), _const_spec(w.shape),
                  _const_spec(qn.shape), _const_spec(kn.shape)],
        out_specs=[_row_spec(ATTN_WIDTH), _row_spec(KV_WIDTH), _row_spec(KV_WIDTH), _row_spec(ATTN_WIDTH)],
        out_shape=[jax.ShapeDtypeStruct((n, ATTN_WIDTH), act_dtype),
                   jax.ShapeDtypeStruct((n, KV_WIDTH), F32),
                   jax.ShapeDtypeStruct((n, KV_WIDTH), F32),
                   jax.ShapeDtypeStruct((n, ATTN_WIDTH), act_dtype)],
        compiler_params=_params(("parallel",)),
        name="attn_in",
    )(x, nw, w, qn, kn)


def _attn_prompt_body(sink_ref, q_ref, kc_ref, kp_ref, vc_ref, vp_ref, o_ref):
    j = pl.program_id(1)
    rows = GQA_GROUP * BLOCK
    row = lax.broadcasted_iota(jnp.int32, (rows, 2 * BLOCK), 0) & (BLOCK - 1)
    col = lax.broadcasted_iota(jnp.int32, (rows, 2 * BLOCK), 1)
    diff = BLOCK + row - col
    kpos = (j - 1) * BLOCK + col - (BLOCK - N_META)
    mask = (diff >= 0) & (diff < WINDOW) & (kpos >= 0)
    rowg = lax.broadcasted_iota(jnp.int32, (rows, 1), 0) // BLOCK
    scale = HEAD_DIM ** -0.5
    for kv in range(N_KV_HEADS):
        sl = slice(kv * HEAD_DIM, (kv + 1) * HEAD_DIM)
        keys = jnp.concatenate([kp_ref[:, sl], kc_ref[:, sl]], axis=0).astype(BF16)
        vals = jnp.concatenate([vp_ref[:, sl], vc_ref[:, sl]], axis=0).astype(BF16)
        base = kv * GQA_GROUP * HEAD_DIM
        q4 = jnp.concatenate(
            [q_ref[:, base + g * HEAD_DIM:base + (g + 1) * HEAD_DIM] for g in range(GQA_GROUP)], axis=0)
        s = _dot_nt(q4, keys) * scale
        s = jnp.where(mask, s, NEG)
        sink = jnp.zeros((rows, 1), F32)
        for g in range(GQA_GROUP):
            sink = jnp.where(rowg == g, sink_ref[kv * GQA_GROUP + g], sink)
        m = jnp.maximum(jnp.max(s, axis=-1, keepdims=True), sink)
        p = jnp.exp(s - m)
        denom = jnp.sum(p, axis=-1, keepdims=True) + jnp.exp(sink - m)
        o = _dot(p.astype(BF16), vals) / denom
        for g in range(GQA_GROUP):
            o_ref[:, base + g * HEAD_DIM:base + (g + 1) * HEAD_DIM] = (
                o[g * BLOCK:(g + 1) * BLOCK].astype(o_ref.dtype))


def _attn_prompt(q, k, v, sinks, n_batch, n_blocks):
    cur = lambda b, j: (b * n_blocks + j, 0)
    prev = lambda b, j: (b * n_blocks + jnp.maximum(j - 1, 0), 0)
    return pl.pallas_call(
        _attn_prompt_body,
        grid=(n_batch, n_blocks),
        in_specs=[pl.BlockSpec(memory_space=pltpu.SMEM),
                  pl.BlockSpec((BLOCK, ATTN_WIDTH), cur),
                  pl.BlockSpec((BLOCK, KV_WIDTH), cur), pl.BlockSpec((BLOCK, KV_WIDTH), prev),
                  pl.BlockSpec((BLOCK, KV_WIDTH), cur), pl.BlockSpec((BLOCK, KV_WIDTH), prev)],
        out_specs=pl.BlockSpec((BLOCK, ATTN_WIDTH), cur),
        out_shape=jax.ShapeDtypeStruct(q.shape, BF16),
        compiler_params=_params(("parallel", "parallel")),
        name="attn_prompt",
    )(sinks, q, k, k, v, v)


def _attn_sample_body(sink_ref, q_ref, k_ref, v_ref, ck_ref, cv_ref, o_ref, nk_ref, nv_ref):
    n_seq = q_ref.shape[0]
    n_new = k_ref.shape[1]
    wbuf = ck_ref.shape[1]
    rows = n_new * GQA_GROUP
    rowt = lax.broadcasted_iota(jnp.int32, (rows, wbuf), 0) // GQA_GROUP
    col = lax.broadcasted_iota(jnp.int32, (rows, wbuf), 1)
    diff = rowt - col + wbuf
    cache_mask = (diff >= 0) & (diff < WINDOW)
    rowt1 = lax.broadcasted_iota(jnp.int32, (rows, 1), 0) // GQA_GROUP
    rowg1 = lax.broadcasted_iota(jnp.int32, (rows, 1), 0) % GQA_GROUP
    scale = HEAD_DIM ** -0.5
    for s in range(n_seq):
        nk_ref[s, 0:wbuf - n_new, :] = ck_ref[s, n_new:wbuf, :]
        nk_ref[s, wbuf - n_new:wbuf, :] = k_ref[s]
        nv_ref[s, 0:wbuf - n_new, :] = cv_ref[s, n_new:wbuf, :]
        nv_ref[s, wbuf - n_new:wbuf, :] = v_ref[s]
        for kv in range(N_KV_HEADS):
            sl = slice(kv * HEAD_DIM, (kv + 1) * HEAD_DIM)
            q16 = q_ref[s, kv]
            knew = k_ref[s, :, sl]
            vnew = v_ref[s, :, sl]
            sc = _dot_nt(q16.astype(BF16), ck_ref[s, :, sl].astype(BF16)) * scale
            sc = jnp.where(cache_mask, sc, NEG)
            sink = jnp.zeros((rows, 1), F32)
            for g in range(GQA_GROUP):
                sink = jnp.where(rowg1 == g, sink_ref[kv * GQA_GROUP + g], sink)
            sn = []
            for t in range(n_new):
                st = jnp.sum(q16 * knew[t:t + 1, :], axis=-1, keepdims=True) * scale
                sn.append(jnp.where(rowt1 >= t, st, NEG))
            m = jnp.maximum(jnp.max(sc, axis=-1, keepdims=True), sink)
            for t in range(n_new):
                m = jnp.maximum(m, sn[t])
            pc = jnp.exp(sc - m)
            denom = jnp.sum(pc, axis=-1, keepdims=True) + jnp.exp(sink - m)
            o = _dot(pc.astype(BF16), cv_ref[s, :, sl].astype(BF16))
            for t in range(n_new):
                pt = jnp.exp(sn[t] - m)
                denom = denom + pt
                o = o + pt * vnew[t:t + 1, :]
            o_ref[s, kv] = o / denom


def _attn_sample(q, k, v, ck, cv, sinks):
    n_seq, n_new = k.shape[0], k.shape[1]
    wbuf = ck.shape[1]
    sb = SAMPLE_SEQS
    blk3 = lambda a: pl.BlockSpec((sb,) + a.shape[1:], lambda i: (i,) + (0,) * (a.ndim - 1))
    return pl.pallas_call(
        _attn_sample_body,
        grid=(n_seq // sb,),
        in_specs=[pl.BlockSpec(memory_space=pltpu.SMEM), blk3(q), blk3(k), blk3(v), blk3(ck), blk3(cv)],
        out_specs=[blk3(q), blk3(ck), blk3(cv)],
        out_shape=[jax.ShapeDtypeStruct(q.shape, F32),
                   jax.ShapeDtypeStruct(ck.shape, F32),
                   jax.ShapeDtypeStruct(cv.shape, F32)],
        compiler_params=_params(("parallel",)),
        name="attn_sample",
    )(sinks, q, k, v, ck, cv)


def _attn_out_body(x_ref, o_ref, g_ref, w_ref, y_ref):
    a = o_ref[...].astype(F32) * _silu(g_ref[...].astype(F32))
    y_ref[...] = x_ref[...] + _dot(a.astype(BF16), w_ref[...])


def _attn_out(x, o, g, w):
    n = x.shape[0]
    return pl.pallas_call(
        _attn_out_body,
        grid=(n // ROW_TILE,),
        in_specs=[_row_spec(D_MODEL), _row_spec(ATTN_WIDTH), _row_spec(ATTN_WIDTH), _const_spec(w.shape)],
        out_specs=_row_spec(D_MODEL),
        out_shape=jax.ShapeDtypeStruct(x.shape, F32),
        compiler_params=_params(("parallel",)),
        name="attn_out",
    )(x, o, g, w)


def _ssm_in_body(x_ref, nw_ref, w_ref, wdt_ref, z_ref, xbc_ref, dt_ref):
    h = _rmsnorm(x_ref[...], nw_ref[...]).astype(BF16)
    z_ref[...] = _dot(h, w_ref[:, :D_INNER]).astype(z_ref.dtype)
    xbc_ref[...] = _dot(h, w_ref[:, D_INNER:])
    dt_ref[...] = _dot(h, wdt_ref[...])


def _ssm_in(x, nw, w, wdt, act_dtype):
    n = x.shape[0]
    return pl.pallas_call(
        _ssm_in_body,
        grid=(n // ROW_TILE,),
        in_specs=[_row_spec(D_MODEL), _const_spec(nw.shape), _const_spec(w.shape), _const_spec(wdt.shape)],
        out_specs=[_row_spec(D_INNER), _row_spec(CONV_DIM), _row_spec(LANES)],
        out_shape=[jax.ShapeDtypeStruct((n, D_INNER), act_dtype),
                   jax.ShapeDtypeStruct((n, CONV_DIM), F32),
                   jax.ShapeDtypeStruct((n, LANES), F32)],
        compiler_params=_params(("parallel",)),
        name="ssm_in",
    )(x, nw, w, wdt)


def _lane_bcast(x, h):
    return jnp.broadcast_to(x[:, h:h + 1], (x.shape[0], LANES))


def _row_bcast(x, h):
    return jnp.broadcast_to(x[h:h + 1, :], (LANES, x.shape[1]))


def _pair_blockdiag(xp):
    lane = lax.broadcasted_iota(jnp.int32, xp.shape, 1)
    xb = xp.astype(BF16)
    zero = jnp.zeros_like(xb)
    return jnp.concatenate([jnp.where(lane < SSM_HEAD_DIM, xb, zero),
                            jnp.where(lane >= SSM_HEAD_DIM, xb, zero)], axis=0)


def _pair_select(a0, a1):
    lane = lax.broadcasted_iota(jnp.int32, a0.shape, 1)
    return jnp.where(lane < SSM_HEAD_DIM, a0, a1)


def _intra_pair(cb, cmask, a_cum, a_cum_t, dt_t, pair):
    ms = []
    for hh in range(2):
        h = 2 * pair + hh
        seg = _lane_bcast(a_cum, h) - _row_bcast(a_cum_t, h)
        dec = jnp.exp(jnp.where(cmask, seg, NEG))
        ms.append((cb * dec * _row_bcast(dt_t, h)).astype(BF16))
    return jnp.concatenate(ms, axis=1)


def _ssd_prompt_body(xbc_ref, dt_ref, cw_ref, cbias_ref, dtb_ref, alog_ref, dsk_ref, e_ref,
                     y_ref, st_out_ref, xfull_ref, st_ref):
    j = pl.program_id(1)
    n_chunks = pl.num_programs(1)
    q = BLOCK
    halo = 8

    @pl.when(j == 0)
    def _():
        xfull_ref[0:halo, :] = jnp.zeros((halo, CONV_DIM), F32)
        st_ref[...] = jnp.zeros(st_ref.shape, F32)

    xfull_ref[halo:halo + q, :] = xbc_ref[...]
    conv = cbias_ref[...] + cw_ref[CONV_W - 1:CONV_W, :] * xfull_ref[halo:halo + q, :]
    for k in range(1, CONV_W):
        conv = conv + cw_ref[CONV_W - 1 - k:CONV_W - k, :] * xfull_ref[halo - k:halo - k + q, :]
    xfull_ref[0:halo, :] = xbc_ref[q - halo:q, :]
    act = _silu(conv)

    rowi = lax.broadcasted_iota(jnp.int32, (q, 1), 0)
    valid = (j > 0) | (rowi >= q - N_META)
    xh = jnp.where(valid, act[:, :D_INNER], 0.0)
    bm = act[:, D_INNER:D_INNER + SSM_GROUPS * D_STATE]
    cm = act[:, D_INNER + SSM_GROUPS * D_STATE:]
    ri = lax.broadcasted_iota(jnp.int32, (q, q), 0)
    ci = lax.broadcasted_iota(jnp.int32, (q, q), 1)
    dt = jnp.where(valid & (ci < SSM_HEADS), _softplus(dt_ref[...] + dtb_ref[...]), 0.0)
    a = dt * (-jnp.exp(alog_ref[...]))
    cmask = ri >= ci
    tri = cmask.astype(BF16)
    a_cum = _exact_left(tri, a)
    a_cum_t = a_cum.T
    dt_t = dt.T
    a_last_t = jnp.broadcast_to(a_cum_t[:, q - 1:q], (q, q))
    w_t = dt_t * jnp.exp(a_last_t - a_cum_t)
    dec_rows = _exact_right(jnp.exp(jnp.broadcast_to(a_cum[q - 1:q, :], (8, LANES))), e_ref[...])[0:1, :]

    for g in range(SSM_GROUPS):
        gs = slice(g * D_STATE, (g + 1) * D_STATE)
        bg = bm[:, gs]
        cg = cm[:, gs].astype(BF16)
        cb = _dot_nt(cg, bg.astype(BF16))
        bg_t = bg.T
        for pr in range(SSM_HEADS // SSM_GROUPS // 2):
            pair = g * (SSM_HEADS // SSM_GROUPS // 2) + pr
            ps = slice(pair * LANES, (pair + 1) * LANES)
            xp = xh[:, ps]
            xbd = _pair_blockdiag(xp)
            y = _dot(_intra_pair(cb, cmask, a_cum, a_cum_t, dt_t, pair), xbd)
            st = st_ref[:, ps]
            esc = _pair_select(jnp.exp(_lane_bcast(a_cum, 2 * pair)), jnp.exp(_lane_bcast(a_cum, 2 * pair + 1)))
            y = y + _dot(cg, st.astype(BF16)) * esc
            wn = jnp.concatenate([(bg_t * _row_bcast(w_t, 2 * pair)).astype(BF16),
                                  (bg_t * _row_bcast(w_t, 2 * pair + 1)).astype(BF16)], axis=1)
            st_ref[:, ps] = st * dec_rows[:, ps] + _dot(wn, xbd)
            y_ref[:, ps] = (y + dsk_ref[:, ps] * xp).astype(y_ref.dtype)

    @pl.when(j == n_chunks - 1)
    def _():
        for pair in range(SSM_HEADS // 2):
            ps = slice(pair * LANES, (pair + 1) * LANES)
            st_out_ref[0, ps, :] = st_ref[:, ps].T


def _ssd_prompt(xbc, dt, cw, cbias, dtb, alog, dsk, emat, n_batch, n_chunks):
    cur = lambda b, j: (b * n_chunks + j, 0)
    return pl.pallas_call(
        _ssd_prompt_body,
        grid=(n_batch, n_chunks),
        in_specs=[pl.BlockSpec((BLOCK, CONV_DIM), cur), pl.BlockSpec((BLOCK, LANES), cur),
                  _const_spec(cw.shape), _const_spec(cbias.shape), _const_spec(dtb.shape),
                  _const_spec(alog.shape), _const_spec(dsk.shape), _const_spec(emat.shape)],
        out_specs=[pl.BlockSpec((BLOCK, D_INNER), cur),
                   pl.BlockSpec((1, D_INNER, D_STATE), lambda b, j: (b, 0, 0))],
        out_shape=[jax.ShapeDtypeStruct((n_batch * n_chunks * BLOCK, D_INNER), BF16),
                   jax.ShapeDtypeStruct((n_batch, D_INNER, D_STATE), F32)],
        scratch_shapes=[pltpu.VMEM((BLOCK + 8, CONV_DIM), F32), pltpu.VMEM((D_STATE, D_INNER), F32)],
        compiler_params=_params(("parallel", "arbitrary")),
        name="ssd_prompt",
    )(xbc, dt, cw, cbias, dtb, alog, dsk, emat)


def _ssd_sample_body(xbc_ref, dt_ref, cs_ref, st_ref, cw_ref, cbias_ref, dtb_ref, alog_ref, dsk_ref, e_ref,
                     y_ref, nst_ref, yt_ref):
    n_seq = st_ref.shape[0]
    n_new = xbc_ref.shape[0] // n_seq
    n_cs = CONV_W - 1
    qr = n_seq * n_new
    q = LANES
    kpad = 64
    assert qr + n_seq * n_cs <= kpad and n_new >= n_cs

    x_new = xbc_ref[...]
    xc = jnp.concatenate([x_new, cs_ref[...], jnp.zeros((kpad - qr - n_seq * n_cs, CONV_DIM), F32)], axis=0)
    r = lax.broadcasted_iota(jnp.int32, (qr, kpad), 0)
    c = lax.broadcasted_iota(jnp.int32, (qr, kpad), 1)
    s_of_r = r // n_new
    t_of_r = r % n_new
    conv = cbias_ref[...] + cw_ref[CONV_W - 1:CONV_W, :] * x_new
    for k in range(1, CONV_W):
        target = jnp.where(t_of_r >= k, r - k, qr + n_cs * s_of_r + t_of_r + n_cs - k)
        conv = conv + cw_ref[CONV_W - 1 - k:CONV_W - k, :] * _exact_left((c == target).astype(BF16), xc)
    act = jnp.concatenate([_silu(conv), jnp.zeros((q - qr, CONV_DIM), F32)], axis=0)
    xh = act[:, :D_INNER]
    bm = act[:, D_INNER:D_INNER + SSM_GROUPS * D_STATE]
    cm = act[:, D_INNER + SSM_GROUPS * D_STATE:]
    ri = lax.broadcasted_iota(jnp.int32, (q, q), 0)
    ci = lax.broadcasted_iota(jnp.int32, (q, q), 1)
    dt = jnp.concatenate([_softplus(dt_ref[...] + dtb_ref[...]), jnp.zeros((q - qr, LANES), F32)], axis=0)
    dt = jnp.where(ci < SSM_HEADS, dt, 0.0)
    a = dt * (-jnp.exp(alog_ref[...]))
    cmask = (ri >= ci) & (ri // n_new == ci // n_new) & (ri < qr)
    a_cum = _exact_left(cmask.astype(BF16), a)
    a_cum_t = a_cum.T
    dt_t = dt.T
    last = ((ci == (ri // n_new) * n_new + n_new - 1) & (ri < qr)).astype(BF16)
    a_last = _exact_left(last, a_cum)
    w_exp = _exact_right(dt * jnp.exp(a_last - a_cum), e_ref[...])
    dec_exp = _exact_right(jnp.exp(a_last), e_ref[...])
    xw = xh * w_exp

    yt_ref[...] = jnp.zeros(yt_ref.shape, F32)
    colseq = ci // n_new
    rowseq = ri // n_new
    for g in range(SSM_GROUPS):
        gs = slice(g * D_STATE, (g + 1) * D_STATE)
        hs = slice(g * GROUP_WIDTH, (g + 1) * GROUP_WIDTH)
        cg_t = cm[:, gs].T.astype(BF16)
        bg = bm[:, gs].astype(BF16)
        xw_t = jnp.concatenate(
            [xw[:, g * GROUP_WIDTH + i * LANES:g * GROUP_WIDTH + (i + 1) * LANES].T
             for i in range(GROUP_WIDTH // LANES)], axis=0).astype(BF16)
        dec_t = jnp.concatenate(
            [dec_exp[:, g * GROUP_WIDTH + i * LANES:g * GROUP_WIDTH + (i + 1) * LANES].T
             for i in range(GROUP_WIDTH // LANES)], axis=0)
        zero = jnp.zeros((q, q), BF16)
        for s in range(n_seq):
            st0 = st_ref[s, hs, :]
            yt_ref[hs, :] += _dot(st0.astype(BF16), jnp.where(colseq == s, cg_t, zero))
            inc = _dot(xw_t, jnp.where(rowseq == s, bg, zero))
            dcol = jnp.broadcast_to(dec_t[:, s * n_new:s * n_new + 1], (GROUP_WIDTH, q))
            nst_ref[s, hs, :] = st0 * dcol + inc

    for g in range(SSM_GROUPS):
        gs = slice(g * D_STATE, (g + 1) * D_STATE)
        cb = _dot_nt(cm[:, gs].astype(BF16), bm[:, gs].astype(BF16))
        for pr in range(SSM_HEADS // SSM_GROUPS // 2):
            pair = g * (SSM_HEADS // SSM_GROUPS // 2) + pr
            ps = slice(pair * LANES, (pair + 1) * LANES)
            xp = xh[:, ps]
            y = _dot(_intra_pair(cb, cmask, a_cum, a_cum_t, dt_t, pair), _pair_blockdiag(xp))
            esc = _pair_select(jnp.exp(_lane_bcast(a_cum, 2 * pair)), jnp.exp(_lane_bcast(a_cum, 2 * pair + 1)))
            y = y + yt_ref[ps, :].T * esc + dsk_ref[:, ps] * xp
            y_ref[:, ps] = y[0:qr, :]


def _ssd_sample(xbc, dt, cs, st, cw, cbias, dtb, alog, dsk, emat):
    n_seq = st.shape[0]
    n_new = xbc.shape[0] // n_seq
    sb = SAMPLE_SEQS
    rows = lambda cols, per: pl.BlockSpec((sb * per, cols), lambda i: (i, 0))
    st_spec = pl.BlockSpec((sb, D_INNER, D_STATE), lambda i: (i, 0, 0))
    return pl.pallas_call(
        _ssd_sample_body,
        grid=(n_seq // sb,),
        in_specs=[rows(CONV_DIM, n_new), rows(LANES, n_new), rows(CONV_DIM, CONV_W - 1), st_spec,
                  _const_spec(cw.shape), _const_spec(cbias.shape), _const_spec(dtb.shape),
                  _const_spec(alog.shape), _const_spec(dsk.shape), _const_spec(emat.shape)],
        out_specs=[rows(D_INNER, n_new), st_spec],
        out_shape=[jax.ShapeDtypeStruct((n_seq * n_new, D_INNER), F32),
                   jax.ShapeDtypeStruct(st.shape, F32)],
        scratch_shapes=[pltpu.VMEM((D_INNER, LANES), F32)],
        compiler_params=_params(("parallel",)),
        name="ssd_sample",
    )(xbc, dt, cs, st, cw, cbias, dtb, alog, dsk, emat)


def _ssm_out_body(x_ref, y_ref, z_ref, nw_ref, w_ref, o_ref):
    y = y_ref[...].astype(F32) * _silu(z_ref[...].astype(F32))
    parts = []
    for g in range(SSM_GROUPS):
        yg = y[:, g * GROUP_WIDTH:(g + 1) * GROUP_WIDTH]
        parts.append(yg * lax.rsqrt(jnp.mean(yg * yg, axis=-1, keepdims=True) + EPS))
    yn = (jnp.concatenate(parts, axis=1) * nw_ref[...]).astype(BF16)
    o_ref[...] = x_ref[...] + _dot(yn, w_ref[...])


def _ssm_out(x, y, z, nw, w):
    n = x.shape[0]
    return pl.pallas_call(
        _ssm_out_body,
        grid=(n // ROW_TILE,),
        in_specs=[_row_spec(D_MODEL), _row_spec(D_INNER), _row_spec(D_INNER),
                  _const_spec(nw.shape), _const_spec(w.shape)],
        out_specs=_row_spec(D_MODEL),
        out_shape=jax.ShapeDtypeStruct(x.shape, F32),
        compiler_params=_params(("parallel",)),
        name="ssm_out",
    )(x, y, z, nw, w)


def _pad_lanes(v):
    return jnp.pad(v.astype(F32), (0, LANES - v.shape[0]))[None, :]


def kernel(x_prompt, x_sample, cache_k, cache_v, state_conv, state_ssm, meta_tokens, norm_w,
           w_attn_in, q_norm_w, k_norm_w, attn_sinks, w_attn_out, w_ssm_in, conv_w, conv_b,
           dt_bias, a_log, d_skip, ssm_norm_w, w_ssm_out):
    n_batch, seq, _ = x_prompt.shape
    n_seq, n_new, _ = x_sample.shape
    wbuf = cache_k.shape[2]
    lead = (-(N_META + seq)) % BLOCK
    n_blocks = (lead + N_META + seq) // BLOCK
    assert lead + N_META == BLOCK and wbuf == WINDOW

    meta = jnp.broadcast_to(meta_tokens[None].astype(F32), (n_batch, N_META, D_MODEL))
    xp = jnp.concatenate([jnp.zeros((n_batch, lead, D_MODEL), F32), meta, x_prompt], axis=1)
    xp = xp.reshape(n_batch * n_blocks * BLOCK, D_MODEL)
    xs = x_sample.reshape(n_seq * n_new, D_MODEL)

    emat = jnp.pad(jnp.repeat(jnp.eye(SSM_HEADS, dtype=BF16), SSM_HEAD_DIM, axis=1),
                   ((0, LANES - SSM_HEADS), (0, 0)))

    kp_l, vp_l, ks_l, vs_l, cp_l, sp_l, cs_l, ss_l = [], [], [], [], [], [], [], []
    for i in range(DEPTH):
        l = i // 2
        nw = norm_w[i][None, :]
        if i % 2 == 0:
            w_in = w_attn_in[l].astype(BF16)
            w_out = w_attn_out[l].astype(BF16)
            qn, kn = q_norm_w[l][None, :], k_norm_w[l][None, :]
            sinks = attn_sinks[l].astype(F32)

            q, k, v, g = _attn_in(xp, nw, w_in, qn, kn, BF16)
            o = _attn_prompt(q, k, v, sinks, n_batch, n_blocks)
            xp = _attn_out(xp, o, g, w_out)
            kp_l.append(k.reshape(n_batch, n_blocks * BLOCK, N_KV_HEADS, HEAD_DIM)[:, -WINDOW:])
            vp_l.append(v.reshape(n_batch, n_blocks * BLOCK, N_KV_HEADS, HEAD_DIM)[:, -WINDOW:])

            q, k, v, g = _attn_in(xs, nw, w_in, qn, kn, F32)
            q = q.reshape(n_seq, n_new, N_KV_HEADS, GQA_GROUP * HEAD_DIM).transpose(0, 2, 1, 3)
            q = q.reshape(n_seq, N_KV_HEADS, n_new * GQA_GROUP, HEAD_DIM)
            o, nk, nv = _attn_sample(q, k.reshape(n_seq, n_new, KV_WIDTH), v.reshape(n_seq, n_new, KV_WIDTH),
                                     cache_k[l].reshape(n_seq, wbuf, KV_WIDTH),
                                     cache_v[l].reshape(n_seq, wbuf, KV_WIDTH), sinks)
            o = o.reshape(n_seq, N_KV_HEADS, n_new, GQA_GROUP * HEAD_DIM).transpose(0, 2, 1, 3)
            xs = _attn_out(xs, o.reshape(n_seq * n_new, ATTN_WIDTH), g, w_out)
            ks_l.append(nk.reshape(n_seq, wbuf, N_KV_HEADS, HEAD_DIM))
            vs_l.append(nv.reshape(n_seq, wbuf, N_KV_HEADS, HEAD_DIM))
        else:
            w_in = w_ssm_in[l][:, :D_INNER + CONV_DIM].astype(BF16)
            w_dt = jnp.pad(w_ssm_in[l][:, D_INNER + CONV_DIM:], ((0, 0), (0, LANES - SSM_HEADS))).astype(BF16)
            w_out = w_ssm_out[l].astype(BF16)
            cw, cbias = conv_w[l].astype(F32), conv_b[l][None, :].astype(F32)
            dtb, alog = _pad_lanes(dt_bias[l]), _pad_lanes(a_log[l])
            dsk = jnp.repeat(d_skip[l].astype(F32), SSM_HEAD_DIM)[None, :]
            snw = ssm_norm_w[l][None, :].astype(F32)

            z, xbc, dt = _ssm_in(xp, nw, w_in, w_dt, BF16)
            y, st = _ssd_prompt(xbc, dt, cw, cbias, dtb, alog, dsk, emat, n_batch, n_blocks)
            xp = _ssm_out(xp, y, z, snw, w_out)
            cp_l.append(xbc.reshape(n_batch, n_blocks * BLOCK, CONV_DIM)[:, -(CONV_W - 1):])
            sp_l.append(st.reshape(n_batch, SSM_HEADS, SSM_HEAD_DIM, D_STATE))

            z, xbc, dt = _ssm_in(xs, nw, w_in, w_dt, F32)
            y, st = _ssd_sample(xbc, dt, state_conv[l].reshape(n_seq * (CONV_W - 1), CONV_DIM),
                                state_ssm[l].reshape(n_seq, D_INNER, D_STATE),
                                cw, cbias, dtb, alog, dsk, emat)
            xs = _ssm_out(xs, y, z, snw, w_out)
            cs_l.append(xbc.reshape(n_seq, n_new, CONV_DIM)[:, -(CONV_W - 1):])
            ss_l.append(st.reshape(n_seq, SSM_HEADS, SSM_HEAD_DIM, D_STATE))

    y_prompt = xp.reshape(n_batch, n_blocks * BLOCK, D_MODEL)[:, lead + N_META:]
    y_sample = xs.reshape(n_seq, n_new, D_MODEL)
    return (y_prompt, y_sample,
            jnp.stack(kp_l), jnp.stack(vp_l), jnp.stack(cp_l), jnp.stack(sp_l),
            jnp.stack(ks_l), jnp.stack(vs_l), jnp.stack(cs_l), jnp.stack(ss_l))
```

```python
import functools
import math

import jax
import jax.numpy as jnp
from jax import lax
from jax.experimental import pallas as pl
from jax.experimental.pallas import tpu as pltpu

F32 = jnp.float32
BF16 = jnp.bfloat16

D_MODEL = 1024
DEPTH = 4
N_META = 16
WINDOW = 128
BLOCK = 128
HEAD_DIM = 128
ATTN_WIDTH = 2 * D_MODEL
N_HEADS = ATTN_WIDTH // HEAD_DIM
N_KV_HEADS = 4
GQA_GROUP = N_HEADS // N_KV_HEADS
KV_WIDTH = N_KV_HEADS * HEAD_DIM
D_INNER = 2 * D_MODEL
SSM_HEAD_DIM = 64
SSM_HEADS = D_INNER // SSM_HEAD_DIM
D_STATE = 128
SSM_GROUPS = 4
GROUP_WIDTH = D_INNER // SSM_GROUPS
CONV_W = 4
CONV_DIM = D_INNER + 2 * SSM_GROUPS * D_STATE
EPS = 1e-6
NEG = -1e30

LANES = 128
ROW_TILE = 512
SAMPLE_SEQS = 8
V7X_VMEM_BYTES = 64 * 1024 * 1024
VMEM_LIMIT = V7X_VMEM_BYTES * 7 // 8


def _dot(a, b):
    return jnp.dot(a, b, preferred_element_type=F32)


def _dot_nt(a, b):
    return lax.dot_general(a, b, (((1,), (1,)), ((), ())), preferred_element_type=F32)


def _rmsnorm(x, w):
    return x * lax.rsqrt(jnp.mean(x * x, axis=-1, keepdims=True) + EPS) * w


def _silu(x):
    return x * jax.nn.sigmoid(x)


def _softplus(x):
    return jnp.maximum(x, 0.0) + jnp.log1p(jnp.exp(-jnp.abs(x)))


def _split3(x):
    hi = x.astype(BF16)
    r = x - hi.astype(F32)
    mid = r.astype(BF16)
    lo = (r - mid.astype(F32)).astype(BF16)
    return hi, mid, lo


def _exact_left(p, x):
    hi, mid, lo = _split3(x)
    return _dot(p, hi) + _dot(p, mid) + _dot(p, lo)


def _exact_right(x, e):
    hi, mid, lo = _split3(x)
    return _dot(hi, e) + _dot(mid, e) + _dot(lo, e)


def _params(semantics):
    return pltpu.CompilerParams(dimension_semantics=semantics, vmem_limit_bytes=VMEM_LIMIT)


def _const_spec(shape):
    nd = len(shape)
    return pl.BlockSpec(shape, lambda *_: (0,) * nd)


def _row_spec(cols, tm=ROW_TILE):
    return pl.BlockSpec((tm, cols), lambda i: (i, 0))


def _attn_in_body(x_ref, nw_ref, w_ref, qn_ref, kn_ref, q_ref, k_ref, v_ref, g_ref):
    h = _rmsnorm(x_ref[...], nw_ref[...]).astype(BF16)
    q = _dot(h, w_ref[:, :ATTN_WIDTH])
    for hd in range(N_HEADS):
        sl = slice(hd * HEAD_DIM, (hd + 1) * HEAD_DIM)
        q_ref[:, sl] = _rmsnorm(q[:, sl], qn_ref[...]).astype(q_ref.dtype)
    k = _dot(h, w_ref[:, ATTN_WIDTH:ATTN_WIDTH + KV_WIDTH])
    for hd in range(N_KV_HEADS):
        sl = slice(hd * HEAD_DIM, (hd + 1) * HEAD_DIM)
        k_ref[:, sl] = _rmsnorm(k[:, sl], kn_ref[...])
    v_ref[...] = _dot(h, w_ref[:, ATTN_WIDTH + KV_WIDTH:ATTN_WIDTH + 2 * KV_WIDTH])
    g_ref[...] = _dot(h, w_ref[:, ATTN_WIDTH + 2 * KV_WIDTH:]).astype(g_ref.dtype)


def _attn_in(x, nw, w, qn, kn, act_dtype):
    n = x.shape[0]
    return pl.pallas_call(
        _attn_in_body,
        grid=(n // ROW_TILE,),
        in_specs=[_row_spec(D_MODEL), _const_spec(nw.shape), _const_spec(w.shape),
                  _const_spec(qn.shape), _const_spec(kn.shape)],
        out_specs=[_row_spec(ATTN_WIDTH), _row_spec(KV_WIDTH), _row_spec(KV_WIDTH), _row_spec(ATTN_WIDTH)],
        out_shape=[jax.ShapeDtypeStruct((n, ATTN_WIDTH), act_dtype),
                   jax.ShapeDtypeStruct((n, KV_WIDTH), F32),
                   jax.ShapeDtypeStruct((n, KV_WIDTH), F32),
                   jax.ShapeDtypeStruct((n, ATTN_WIDTH), act_dtype)],
        compiler_params=_params(("parallel",)),
        name="attn_in",
    )(x, nw, w, qn, kn)


def _attn_prompt_body(sink_ref, q_ref, kc_ref, kp_ref, vc_ref, vp_ref, o_ref):
    j = pl.program_id(1)
    rows = GQA_GROUP * BLOCK
    row = lax.broadcasted_iota(jnp.int32, (rows, 2 * BLOCK), 0) & (BLOCK - 1)
    col = lax.broadcasted_iota(jnp.int32, (rows, 2 * BLOCK), 1)
    diff = BLOCK + row - col
    kpos = (j - 1) * BLOCK + col - (BLOCK - N_META)
    mask = (diff >= 0) & (diff < WINDOW) & (kpos >= 0)
    rowg = lax.broadcasted_iota(jnp.int32, (rows, 1), 0) // BLOCK
    scale = HEAD_DIM ** -0.5
    for kv in range(N_KV_HEADS):
        sl = slice(kv * HEAD_DIM, (kv + 1) * HEAD_DIM)
        keys = jnp.concatenate([kp_ref[:, sl], kc_ref[:, sl]], axis=0).astype(BF16)
        vals = jnp.concatenate([vp_ref[:, sl], vc_ref[:, sl]], axis=0).astype(BF16)
        base = kv * GQA_GROUP * HEAD_DIM
        q4 = jnp.concatenate(
            [q_ref[:, base + g * HEAD_DIM:base + (g + 1) * HEAD_DIM] for g in range(GQA_GROUP)], axis=0)
        s = _dot_nt(q4, keys) * scale
        s = jnp.where(mask, s, NEG)
        sink = jnp.zeros((rows, 1), F32)
        for g in range(GQA_GROUP):
            sink = jnp.where(rowg == g, sink_ref[kv * GQA_GROUP + g], sink)
        m = jnp.maximum(jnp.max(s, axis=-1, keepdims=True), sink)
        p = jnp.exp(s - m)
        denom = jnp.sum(p, axis=-1, keepdims=True) + jnp.exp(sink - m)
        o = _dot(p.astype(BF16), vals) / denom
        for g in range(GQA_GROUP):
            o_ref[:, base + g * HEAD_DIM:base + (g + 1) * HEAD_DIM] = (
                o[g * BLOCK:(g + 1) * BLOCK].astype(o_ref.dtype))


def _attn_prompt(q, k, v, sinks, n_batch, n_blocks):
    cur = lambda b, j: (b * n_blocks + j, 0)
    prev = lambda b, j: (b * n_blocks + jnp.maximum(j - 1, 0), 0)
    return pl.pallas_call(
        _attn_prompt_body,
        grid=(n_batch, n_blocks),
        in_specs=[pl.BlockSpec(memory_space=pltpu.SMEM),
                  pl.BlockSpec((BLOCK, ATTN_WIDTH), cur),
                  pl.BlockSpec((BLOCK, KV_WIDTH), cur), pl.BlockSpec((BLOCK, KV_WIDTH), prev),
                  pl.BlockSpec((BLOCK, KV_WIDTH), cur), pl.BlockSpec((BLOCK, KV_WIDTH), prev)],
        out_specs=pl.BlockSpec((BLOCK, ATTN_WIDTH), cur),
        out_shape=jax.ShapeDtypeStruct(q.shape, BF16),
        compiler_params=_params(("parallel", "parallel")),
        name="attn_prompt",
    )(sinks, q, k, k, v, v)


def _attn_sample_body(sink_ref, q_ref, k_ref, v_ref, ck_ref, cv_ref, *rest):
    o_ref, nk_ref, nv_ref = rest[-3:]
    n_seq = q_ref.shape[0]
    n_new = k_ref.shape[1] // N_KV_HEADS
    wbuf = ck_ref.shape[1] // N_KV_HEADS
    rows = n_new * GQA_GROUP
    rowt = lax.broadcasted_iota(jnp.int32, (rows, wbuf), 0) // GQA_GROUP
    col = lax.broadcasted_iota(jnp.int32, (rows, wbuf), 1)
    diff = rowt - col + wbuf
    cache_mask = (diff >= 0) & (diff < WINDOW)
    rowt1 = lax.broadcasted_iota(jnp.int32, (rows, 1), 0) // GQA_GROUP
    rowg1 = lax.broadcasted_iota(jnp.int32, (rows, 1), 0) % GQA_GROUP
    scale = HEAD_DIM ** -0.5
    keep = (wbuf - n_new) * N_KV_HEADS
    for s in range(n_seq):
        nk_ref[s, 0:keep, :] = ck_ref[s, n_new * N_KV_HEADS:, :]
        nk_ref[s, keep:, :] = k_ref[s]
        nv_ref[s, 0:keep, :] = cv_ref[s, n_new * N_KV_HEADS:, :]
        nv_ref[s, keep:, :] = v_ref[s]
        for kv in range(N_KV_HEADS):
            q16 = q_ref[s, kv]
            knew = k_ref[s, pl.ds(kv, n_new, stride=N_KV_HEADS), :]
            vnew = v_ref[s, pl.ds(kv, n_new, stride=N_KV_HEADS), :]
            kc = ck_ref[s, pl.ds(kv, wbuf, stride=N_KV_HEADS), :]
            vc = cv_ref[s, pl.ds(kv, wbuf, stride=N_KV_HEADS), :]
            sc = _dot_nt(q16.astype(BF16), kc.astype(BF16)) * scale
            sc = jnp.where(cache_mask, sc, NEG)
            sink = jnp.zeros((rows, 1), F32)
            for g in range(GQA_GROUP):
                sink = jnp.where(rowg1 == g, sink_ref[kv * GQA_GROUP + g], sink)
            sn = []
            for t in range(n_new):
                st = jnp.sum(q16 * knew[t:t + 1, :], axis=-1, keepdims=True) * scale
                sn.append(jnp.where(rowt1 >= t, st, NEG))
            m = jnp.maximum(jnp.max(sc, axis=-1, keepdims=True), sink)
            for t in range(n_new):
                m = jnp.maximum(m, sn[t])
            pc = jnp.exp(sc - m)
            denom = jnp.sum(pc, axis=-1, keepdims=True) + jnp.exp(sink - m)
            o = _dot(pc.astype(BF16), vc.astype(BF16))
            for t in range(n_new):
                pt = jnp.exp(sn[t] - m)
                denom = denom + pt
                o = o + pt * vnew[t:t + 1, :]
            o_ref[s, kv] = o / denom


def _attn_sample(q, k, v, ck, cv, sinks, layer, nk_prev=None, nv_prev=None):
    sb = SAMPLE_SEQS
    n_seq = q.shape[0]
    blk = lambda a: pl.BlockSpec((sb,) + a.shape[1:], lambda i: (i,) + (0,) * (a.ndim - 1))
    cache_spec = pl.BlockSpec((None, sb) + ck.shape[2:], lambda i: (layer, i, 0, 0))
    in_specs = [pl.BlockSpec(memory_space=pltpu.SMEM), blk(q), blk(k), blk(v), cache_spec, cache_spec]
    args = [sinks, q, k, v, ck, cv]
    aliases = {}
    if nk_prev is not None:
        in_specs += [pl.BlockSpec(memory_space=pl.ANY)] * 2
        aliases = {len(args): 1, len(args) + 1: 2}
        args += [nk_prev, nv_prev]
    return pl.pallas_call(
        _attn_sample_body,
        grid=(n_seq // sb,),
        in_specs=in_specs,
        out_specs=[blk(q), cache_spec, cache_spec],
        out_shape=[jax.ShapeDtypeStruct(q.shape, F32),
                   jax.ShapeDtypeStruct(ck.shape, F32),
                   jax.ShapeDtypeStruct(cv.shape, F32)],
        input_output_aliases=aliases,
        compiler_params=_params(("parallel",)),
        name="attn_sample",
    )(*args)


def _attn_out_body(x_ref, o_ref, g_ref, w_ref, y_ref):
    a = o_ref[...].astype(F32) * _silu(g_ref[...].astype(F32))
    y_ref[...] = x_ref[...] + _dot(a.astype(BF16), w_ref[...])


def _attn_out(x, o, g, w):
    n = x.shape[0]
    return pl.pallas_call(
        _attn_out_body,
        grid=(n // ROW_TILE,),
        in_specs=[_row_spec(D_MODEL), _row_spec(ATTN_WIDTH), _row_spec(ATTN_WIDTH), _const_spec(w.shape)],
        out_specs=_row_spec(D_MODEL),
        out_shape=jax.ShapeDtypeStruct(x.shape, F32),
        compiler_params=_params(("parallel",)),
        name="attn_out",
    )(x, o, g, w)


def _ssm_in_body(x_ref, nw_ref, w_ref, wdt_ref, z_ref, xbc_ref, dt_ref):
    h = _rmsnorm(x_ref[...], nw_ref[...]).astype(BF16)
    z_ref[...] = _dot(h, w_ref[:, :D_INNER]).astype(z_ref.dtype)
    xbc_ref[...] = _dot(h, w_ref[:, D_INNER:])
    dt_ref[...] = _dot(h, wdt_ref[...])


def _ssm_in(x, nw, w, wdt, act_dtype):
    n = x.shape[0]
    return pl.pallas_call(
        _ssm_in_body,
        grid=(n // ROW_TILE,),
        in_specs=[_row_spec(D_MODEL), _const_spec(nw.shape), _const_spec(w.shape), _const_spec(wdt.shape)],
        out_specs=[_row_spec(D_INNER), _row_spec(CONV_DIM), _row_spec(LANES)],
        out_shape=[jax.ShapeDtypeStruct((n, D_INNER), act_dtype),
                   jax.ShapeDtypeStruct((n, CONV_DIM), F32),
                   jax.ShapeDtypeStruct((n, LANES), F32)],
        compiler_params=_params(("parallel",)),
        name="ssm_in",
    )(x, nw, w, wdt)


def _lane_bcast(x, h):
    return jnp.broadcast_to(x[:, h:h + 1], (x.shape[0], LANES))


def _row_bcast(x, h):
    return jnp.broadcast_to(x[h:h + 1, :], (LANES, x.shape[1]))


def _pair_blockdiag(xp):
    lane = lax.broadcasted_iota(jnp.int32, xp.shape, 1)
    xb = xp.astype(BF16)
    zero = jnp.zeros_like(xb)
    return jnp.concatenate([jnp.where(lane < SSM_HEAD_DIM, xb, zero),
                            jnp.where(lane >= SSM_HEAD_DIM, xb, zero)], axis=0)


def _pair_select(a0, a1):
    lane = lax.broadcasted_iota(jnp.int32, a0.shape, 1)
    return jnp.where(lane < SSM_HEAD_DIM, a0, a1)


def _intra_pair(cb, cmask, a_cum, a_cum_t, dt_t, pair):
    ms = []
    for hh in range(2):
        h = 2 * pair + hh
        seg = _lane_bcast(a_cum, h) - _row_bcast(a_cum_t, h)
        dec = jnp.exp(jnp.where(cmask, seg, NEG))
        ms.append((cb * dec * _row_bcast(dt_t, h)).astype(BF16))
    return jnp.concatenate(ms, axis=1)


def _ssd_prompt_body(xbc_ref, dt_ref, cw_ref, cbias_ref, dtb_ref, alog_ref, dsk_ref, e_ref,
                     y_ref, st_out_ref, xfull_ref, st_ref):
    j = pl.program_id(1)
    n_chunks = pl.num_programs(1)
    q = BLOCK
    halo = 8

    @pl.when(j == 0)
    def _():
        xfull_ref[0:halo, :] = jnp.zeros((halo, CONV_DIM), F32)
        st_ref[...] = jnp.zeros(st_ref.shape, F32)

    xfull_ref[halo:halo + q, :] = xbc_ref[...]
    conv = cbias_ref[...] + cw_ref[CONV_W - 1:CONV_W, :] * xfull_ref[halo:halo + q, :]
    for k in range(1, CONV_W):
        conv = conv + cw_ref[CONV_W - 1 - k:CONV_W - k, :] * xfull_ref[halo - k:halo - k + q, :]
    xfull_ref[0:halo, :] = xbc_ref[q - halo:q, :]
    act = _silu(conv)

    rowi = lax.broadcasted_iota(jnp.int32, (q, 1), 0)
    valid = (j > 0) | (rowi >= q - N_META)
    xh = jnp.where(valid, act[:, :D_INNER], 0.0)
    bm = act[:, D_INNER:D_INNER + SSM_GROUPS * D_STATE]
    cm = act[:, D_INNER + SSM_GROUPS * D_STATE:]
    ri = lax.broadcasted_iota(jnp.int32, (q, q), 0)
    ci = lax.broadcasted_iota(jnp.int32, (q, q), 1)
    dt = jnp.where(valid & (ci < SSM_HEADS), _softplus(dt_ref[...] + dtb_ref[...]), 0.0)
    a = dt * (-jnp.exp(alog_ref[...]))
    cmask = ri >= ci
    tri = cmask.astype(BF16)
    a_cum = _exact_left(tri, a)
    a_cum_t = a_cum.T
    dt_t = dt.T
    a_last_t = jnp.broadcast_to(a_cum_t[:, q - 1:q], (q, q))
    w_t = dt_t * jnp.exp(a_last_t - a_cum_t)
    dec_rows = _exact_right(jnp.exp(jnp.broadcast_to(a_cum[q - 1:q, :], (8, LANES))), e_ref[...])[0:1, :]

    for g in range(SSM_GROUPS):
        gs = slice(g * D_STATE, (g + 1) * D_STATE)
        bg = bm[:, gs]
        cg = cm[:, gs].astype(BF16)
        cb = _dot_nt(cg, bg.astype(BF16))
        bg_t = bg.T
        for pr in range(SSM_HEADS // SSM_GROUPS // 2):
            pair = g * (SSM_HEADS // SSM_GROUPS // 2) + pr
            ps = slice(pair * LANES, (pair + 1) * LANES)
            xp = xh[:, ps]
            xbd = _pair_blockdiag(xp)
            y = _dot(_intra_pair(cb, cmask, a_cum, a_cum_t, dt_t, pair), xbd)
            st = st_ref[:, ps]
            esc = _pair_select(jnp.exp(_lane_bcast(a_cum, 2 * pair)), jnp.exp(_lane_bcast(a_cum, 2 * pair + 1)))
            y = y + _dot(cg, st.astype(BF16)) * esc
            wn = jnp.concatenate([(bg_t * _row_bcast(w_t, 2 * pair)).astype(BF16),
                                  (bg_t * _row_bcast(w_t, 2 * pair + 1)).astype(BF16)], axis=1)
            st_ref[:, ps] = st * dec_rows[:, ps] + _dot(wn, xbd)
            y_ref[:, ps] = (y + dsk_ref[:, ps] * xp).astype(y_ref.dtype)

    @pl.when(j == n_chunks - 1)
    def _():
        for pair in range(SSM_HEADS // 2):
            ps = slice(pair * LANES, (pair + 1) * LANES)
            st_out_ref[0, ps, :] = st_ref[:, ps].T


def _ssd_prompt(xbc, dt, cw, cbias, dtb, alog, dsk, emat, n_batch, n_chunks):
    cur = lambda b, j: (b * n_chunks + j, 0)
    return pl.pallas_call(
        _ssd_prompt_body,
        grid=(n_batch, n_chunks),
        in_specs=[pl.BlockSpec((BLOCK, CONV_DIM), cur), pl.BlockSpec((BLOCK, LANES), cur),
                  _const_spec(cw.shape), _const_spec(cbias.shape), _const_spec(dtb.shape),
                  _const_spec(alog.shape), _const_spec(dsk.shape), _const_spec(emat.shape)],
        out_specs=[pl.BlockSpec((BLOCK, D_INNER), cur),
                   pl.BlockSpec((1, D_INNER, D_STATE), lambda b, j: (b, 0, 0))],
        out_shape=[jax.ShapeDtypeStruct((n_batch * n_chunks * BLOCK, D_INNER), BF16),
                   jax.ShapeDtypeStruct((n_batch, D_INNER, D_STATE), F32)],
        scratch_shapes=[pltpu.VMEM((BLOCK + 8, CONV_DIM), F32), pltpu.VMEM((D_STATE, D_INNER), F32)],
        compiler_params=_params(("parallel", "arbitrary")),
        name="ssd_prompt",
    )(xbc, dt, cw, cbias, dtb, alog, dsk, emat)


def _ssd_sample_body(xbc_ref, dt_ref, cs_ref, st_ref, cw_ref, cbias_ref, dtb_ref, alog_ref, dsk_ref, e_ref,
                     *rest):
    y_ref, nst_ref, yt_ref = rest[-3:]
    n_seq = st_ref.shape[0]
    n_new = xbc_ref.shape[0] // n_seq
    n_cs = CONV_W - 1
    qr = n_seq * n_new
    q = LANES
    kpad = 64
    assert qr + n_seq * n_cs <= kpad and n_new >= n_cs

    x_new = xbc_ref[...]
    xc = jnp.concatenate([x_new, cs_ref[...], jnp.zeros((kpad - qr - n_seq * n_cs, CONV_DIM), F32)], axis=0)
    r = lax.broadcasted_iota(jnp.int32, (qr, kpad), 0)
    c = lax.broadcasted_iota(jnp.int32, (qr, kpad), 1)
    s_of_r = r // n_new
    t_of_r = r % n_new
    conv = cbias_ref[...] + cw_ref[CONV_W - 1:CONV_W, :] * x_new
    for k in range(1, CONV_W):
        target = jnp.where(t_of_r >= k, r - k, qr + n_cs * s_of_r + t_of_r + n_cs - k)
        conv = conv + cw_ref[CONV_W - 1 - k:CONV_W - k, :] * _exact_left((c == target).astype(BF16), xc)
    act = jnp.concatenate([_silu(conv), jnp.zeros((q - qr, CONV_DIM), F32)], axis=0)
    xh = act[:, :D_INNER]
    bm = act[:, D_INNER:D_INNER + SSM_GROUPS * D_STATE]
    cm = act[:, D_INNER + SSM_GROUPS * D_STATE:]
    ri = lax.broadcasted_iota(jnp.int32, (q, q), 0)
    ci = lax.broadcasted_iota(jnp.int32, (q, q), 1)
    dt = jnp.concatenate([_softplus(dt_ref[...] + dtb_ref[...]), jnp.zeros((q - qr, LANES), F32)], axis=0)
    dt = jnp.where(ci < SSM_HEADS, dt, 0.0)
    a = dt * (-jnp.exp(alog_ref[...]))
    cmask = (ri >= ci) & (ri // n_new == ci // n_new) & (ri < qr)
    a_cum = _exact_left(cmask.astype(BF16), a)
    a_cum_t = a_cum.T
    dt_t = dt.T
    last = ((ci == (ri // n_new) * n_new + n_new - 1) & (ri < qr)).astype(BF16)
    a_last = _exact_left(last, a_cum)
    w_exp = _exact_right(dt * jnp.exp(a_last - a_cum), e_ref[...])
    dec_exp = _exact_right(jnp.exp(a_last), e_ref[...])
    xw = xh * w_exp

    yt_ref[...] = jnp.zeros(yt_ref.shape, F32)
    colseq = ci // n_new
    rowseq = ri // n_new
    for g in range(SSM_GROUPS):
        gs = slice(g * D_STATE, (g + 1) * D_STATE)
        hs = slice(g * GROUP_WIDTH, (g + 1) * GROUP_WIDTH)
        cg_t = cm[:, gs].T.astype(BF16)
        bg = bm[:, gs].astype(BF16)
        xw_t = jnp.concatenate(
            [xw[:, g * GROUP_WIDTH + i * LANES:g * GROUP_WIDTH + (i + 1) * LANES].T
             for i in range(GROUP_WIDTH // LANES)], axis=0).astype(BF16)
        dec_t = jnp.concatenate(
            [dec_exp[:, g * GROUP_WIDTH + i * LANES:g * GROUP_WIDTH + (i + 1) * LANES].T
             for i in range(GROUP_WIDTH // LANES)], axis=0)
        zero = jnp.zeros((q, q), BF16)
        for s in range(n_seq):
            st0 = st_ref[s, hs, :]
            yt_ref[hs, :] += _dot(st0.astype(BF16), jnp.where(colseq == s, cg_t, zero))
            inc = _dot(xw_t, jnp.where(rowseq == s, bg, zero))
            dcol = jnp.broadcast_to(dec_t[:, s * n_new:s * n_new + 1], (GROUP_WIDTH, q))
            nst_ref[s, hs, :] = st0 * dcol + inc

    for g in range(SSM_GROUPS):
        gs = slice(g * D_STATE, (g + 1) * D_STATE)
        cb = _dot_nt(cm[:, gs].astype(BF16), bm[:, gs].astype(BF16))
        for pr in range(SSM_HEADS // SSM_GROUPS // 2):
            pair = g * (SSM_HEADS // SSM_GROUPS // 2) + pr
            ps = slice(pair * LANES, (pair + 1) * LANES)
            xp = xh[:, ps]
            y = _dot(_intra_pair(cb, cmask, a_cum, a_cum_t, dt_t, pair), _pair_blockdiag(xp))
            esc = _pair_select(jnp.exp(_lane_bcast(a_cum, 2 * pair)), jnp.exp(_lane_bcast(a_cum, 2 * pair + 1)))
            y = y + yt_ref[ps, :].T * esc + dsk_ref[:, ps] * xp
            y_ref[:, ps] = y[0:qr, :]


def _ssd_sample(xbc, dt, cs, st, cw, cbias, dtb, alog, dsk, emat, layer, nst_prev=None):
    n_seq = st.shape[1]
    n_new = xbc.shape[0] // n_seq
    sb = SAMPLE_SEQS
    rows = lambda cols, per: pl.BlockSpec((sb * per, cols), lambda i: (i, 0))
    st_spec = pl.BlockSpec((None, sb, D_INNER, D_STATE), lambda i: (layer, i, 0, 0))
    in_specs = [rows(CONV_DIM, n_new), rows(LANES, n_new), rows(CONV_DIM, CONV_W - 1), st_spec,
                _const_spec(cw.shape), _const_spec(cbias.shape), _const_spec(dtb.shape),
                _const_spec(alog.shape), _const_spec(dsk.shape), _const_spec(emat.shape)]
    args = [xbc, dt, cs, st, cw, cbias, dtb, alog, dsk, emat]
    aliases = {}
    if nst_prev is not None:
        in_specs.append(pl.BlockSpec(memory_space=pl.ANY))
        aliases = {len(args): 1}
        args.append(nst_prev)
    return pl.pallas_call(
        _ssd_sample_body,
        grid=(n_seq // sb,),
        in_specs=in_specs,
        out_specs=[rows(D_INNER, n_new), st_spec],
        out_shape=[jax.ShapeDtypeStruct((n_seq * n_new, D_INNER), F32),
                   jax.ShapeDtypeStruct(st.shape, F32)],
        input_output_aliases=aliases,
        scratch_shapes=[pltpu.VMEM((D_INNER, LANES), F32)],
        compiler_params=_params(("parallel",)),
        name="ssd_sample",
    )(*args)


def _ssm_out_body(x_ref, y_ref, z_ref, nw_ref, w_ref, o_ref):
    y = y_ref[...].astype(F32) * _silu(z_ref[...].astype(F32))
    parts = []
    for g in range(SSM_GROUPS):
        yg = y[:, g * GROUP_WIDTH:(g + 1) * GROUP_WIDTH]
        parts.append(yg * lax.rsqrt(jnp.mean(yg * yg, axis=-1, keepdims=True) + EPS))
    yn = (jnp.concatenate(parts, axis=1) * nw_ref[...]).astype(BF16)
    o_ref[...] = x_ref[...] + _dot(yn, w_ref[...])


def _ssm_out(x, y, z, nw, w):
    n = x.shape[0]
    return pl.pallas_call(
        _ssm_out_body,
        grid=(n // ROW_TILE,),
        in_specs=[_row_spec(D_MODEL), _row_spec(D_INNER), _row_spec(D_INNER),
                  _const_spec(nw.shape), _const_spec(w.shape)],
        out_specs=_row_spec(D_MODEL),
        out_shape=jax.ShapeDtypeStruct(x.shape, F32),
        compiler_params=_params(("parallel",)),
        name="ssm_out",
    )(x, y, z, nw, w)


def _pad_lanes(v):
    return jnp.pad(v.astype(F32), (0, LANES - v.shape[0]))[None, :]


def kernel(x_prompt, x_sample, cache_k, cache_v, state_conv, state_ssm, meta_tokens, norm_w,
           w_attn_in, q_norm_w, k_norm_w, attn_sinks, w_attn_out, w_ssm_in, conv_w, conv_b,
           dt_bias, a_log, d_skip, ssm_norm_w, w_ssm_out):
    n_batch, seq, _ = x_prompt.shape
    n_seq, n_new, _ = x_sample.shape
    wbuf = cache_k.shape[2]
    lead = (-(N_META + seq)) % BLOCK
    n_blocks = (lead + N_META + seq) // BLOCK
    assert lead + N_META == BLOCK and wbuf == WINDOW

    meta = jnp.broadcast_to(meta_tokens[None].astype(F32), (n_batch, N_META, D_MODEL))
    xp = jnp.concatenate([jnp.zeros((n_batch, lead, D_MODEL), F32), meta, x_prompt], axis=1)
    xp = xp.reshape(n_batch * n_blocks * BLOCK, D_MODEL)
    xs = x_sample.reshape(n_seq * n_new, D_MODEL)

    emat = jnp.pad(jnp.repeat(jnp.eye(SSM_HEADS, dtype=BF16), SSM_HEAD_DIM, axis=1),
                   ((0, LANES - SSM_HEADS), (0, 0)))

    n_attn, n_ssm = cache_k.shape[0], state_ssm.shape[0]
    ck_all = cache_k.reshape(n_attn, n_seq, wbuf * N_KV_HEADS, HEAD_DIM)
    cv_all = cache_v.reshape(n_attn, n_seq, wbuf * N_KV_HEADS, HEAD_DIM)
    st_all = state_ssm.reshape(n_ssm, n_seq, D_INNER, D_STATE)
    new_ck = new_cv = new_st = None

    kp_l, vp_l, cp_l, sp_l, cs_l = [], [], [], [], []
    for i in range(DEPTH):
        l = i // 2
        nw = norm_w[i][None, :]
        if i % 2 == 0:
            w_in = w_attn_in[l].astype(BF16)
            w_out = w_attn_out[l].astype(BF16)
            qn, kn = q_norm_w[l][None, :], k_norm_w[l][None, :]
            sinks = attn_sinks[l].astype(F32)

            q, k, v, g = _attn_in(xp, nw, w_in, qn, kn, BF16)
            o = _attn_prompt(q, k, v, sinks, n_batch, n_blocks)
            xp = _attn_out(xp, o, g, w_out)
            kp_l.append(k.reshape(n_batch, n_blocks * BLOCK, N_KV_HEADS, HEAD_DIM)[:, -WINDOW:])
            vp_l.append(v.reshape(n_batch, n_blocks * BLOCK, N_KV_HEADS, HEAD_DIM)[:, -WINDOW:])

            q, k, v, g = _attn_in(xs, nw, w_in, qn, kn, F32)
            q = q.reshape(n_seq, n_new, N_KV_HEADS, GQA_GROUP * HEAD_DIM).transpose(0, 2, 1, 3)
            q = q.reshape(n_seq, N_KV_HEADS, n_new * GQA_GROUP, HEAD_DIM)
            o, new_ck, new_cv = _attn_sample(q, k.reshape(n_seq, n_new * N_KV_HEADS, HEAD_DIM),
                                             v.reshape(n_seq, n_new * N_KV_HEADS, HEAD_DIM),
                                             ck_all, cv_all, sinks, l, new_ck, new_cv)
            o = o.reshape(n_seq, N_KV_HEADS, n_new, GQA_GROUP * HEAD_DIM).transpose(0, 2, 1, 3)
            xs = _attn_out(xs, o.reshape(n_seq * n_new, ATTN_WIDTH), g, w_out)
        else:
            w_in = w_ssm_in[l][:, :D_INNER + CONV_DIM].astype(BF16)
            w_dt = jnp.pad(w_ssm_in[l][:, D_INNER + CONV_DIM:], ((0, 0), (0, LANES - SSM_HEADS))).astype(BF16)
            w_out = w_ssm_out[l].astype(BF16)
            cw, cbias = conv_w[l].astype(F32), conv_b[l][None, :].astype(F32)
            dtb, alog = _pad_lanes(dt_bias[l]), _pad_lanes(a_log[l])
            dsk = jnp.repeat(d_skip[l].astype(F32), SSM_HEAD_DIM)[None, :]
            snw = ssm_norm_w[l][None, :].astype(F32)

            z, xbc, dt = _ssm_in(xp, nw, w_in, w_dt, BF16)
            y, st = _ssd_prompt(xbc, dt, cw, cbias, dtb, alog, dsk, emat, n_batch, n_blocks)
            xp = _ssm_out(xp, y, z, snw, w_out)
            cp_l.append(xbc.reshape(n_batch, n_blocks * BLOCK, CONV_DIM)[:, -(CONV_W - 1):])
            sp_l.append(st.reshape(n_batch, SSM_HEADS, SSM_HEAD_DIM, D_STATE))

            z, xbc, dt = _ssm_in(xs, nw, w_in, w_dt, F32)
            y, new_st = _ssd_sample(xbc, dt, state_conv[l].reshape(n_seq * (CONV_W - 1), CONV_DIM), st_all,
                                    cw, cbias, dtb, alog, dsk, emat, l, new_st)
            xs = _ssm_out(xs, y, z, snw, w_out)
            cs_l.append(xbc.reshape(n_seq, n_new, CONV_DIM)[:, -(CONV_W - 1):])

    y_prompt = xp.reshape(n_batch, n_blocks * BLOCK, D_MODEL)[:, lead + N_META:]
    y_sample = xs.reshape(n_seq, n_new, D_MODEL)
    return (y_prompt, y_sample,
            jnp.stack(kp_l), jnp.stack(vp_l), jnp.stack(cp_l), jnp.stack(sp_l),
            new_ck.reshape(cache_k.shape), new_cv.reshape(cache_v.shape), jnp.stack(cs_l),
            new_st.reshape(state_ssm.shape))
```

```python
import functools
import math

import jax
import jax.numpy as jnp
from jax import lax
from jax.experimental import pallas as pl
from jax.experimental.pallas import tpu as pltpu

F32 = jnp.float32
BF16 = jnp.bfloat16

D_MODEL = 1024
DEPTH = 4
N_META = 16
WINDOW = 128
BLOCK = 128
HEAD_DIM = 128
ATTN_WIDTH = 2 * D_MODEL
N_HEADS = ATTN_WIDTH // HEAD_DIM
N_KV_HEADS = 4
GQA_GROUP = N_HEADS // N_KV_HEADS
KV_WIDTH = N_KV_HEADS * HEAD_DIM
D_INNER = 2 * D_MODEL
SSM_HEAD_DIM = 64
SSM_HEADS = D_INNER // SSM_HEAD_DIM
D_STATE = 128
SSM_GROUPS = 4
GROUP_WIDTH = D_INNER // SSM_GROUPS
PAIRS_PER_GROUP = SSM_HEADS // SSM_GROUPS // 2
CONV_W = 4
CONV_DIM = D_INNER + 2 * SSM_GROUPS * D_STATE
EPS = 1e-6
NEG = -1e30
LOG2E = math.log2(math.e)

LANES = 128
ROW_TILE = 512
SAMPLE_SEQS = 8
CONV_HALO = 16
V7X_VMEM_BYTES = 64 * 1024 * 1024
VMEM_LIMIT = V7X_VMEM_BYTES * 7 // 8


def _dot(a, b):
    return jnp.dot(a, b, preferred_element_type=F32)


def _dot_nt(a, b):
    return lax.dot_general(a, b, (((1,), (1,)), ((), ())), preferred_element_type=F32)


def _rmsnorm(x, w):
    return x * lax.rsqrt(jnp.mean(x * x, axis=-1, keepdims=True) + EPS) * w


def _silu(x):
    return x * jax.nn.sigmoid(x)


def _softplus(x):
    return jnp.maximum(x, 0.0) + jnp.log1p(jnp.exp(-jnp.abs(x)))


def _split3(x):
    hi = x.astype(BF16)
    r = x - hi.astype(F32)
    mid = r.astype(BF16)
    lo = (r - mid.astype(F32)).astype(BF16)
    return hi, mid, lo


def _exact_left(p, x):
    hi, mid, lo = _split3(x)
    return _dot(p, hi) + _dot(p, mid) + _dot(p, lo)


def _exact_right(x, e):
    hi, mid, lo = _split3(x)
    return _dot(hi, e) + _dot(mid, e) + _dot(lo, e)


def _params(semantics):
    return pltpu.CompilerParams(dimension_semantics=semantics, vmem_limit_bytes=VMEM_LIMIT)


def _const_spec(shape):
    nd = len(shape)
    return pl.BlockSpec(shape, lambda *_: (0,) * nd)


def _row_spec(cols, tm):
    return pl.BlockSpec((tm, cols), lambda i: (i, 0))


def _attn_in_body(x_ref, nw_ref, w_ref, qn_ref, kn_ref, q_ref, k_ref, v_ref, g_ref):
    h = _rmsnorm(x_ref[...], nw_ref[...]).astype(BF16)
    q = _dot(h, w_ref[:, :ATTN_WIDTH])
    for hd in range(N_HEADS):
        sl = slice(hd * HEAD_DIM, (hd + 1) * HEAD_DIM)
        q_ref[:, sl] = _rmsnorm(q[:, sl], qn_ref[...]).astype(q_ref.dtype)
    k = _dot(h, w_ref[:, ATTN_WIDTH:ATTN_WIDTH + KV_WIDTH])
    for hd in range(N_KV_HEADS):
        sl = slice(hd * HEAD_DIM, (hd + 1) * HEAD_DIM)
        k_ref[:, sl] = _rmsnorm(k[:, sl], kn_ref[...])
    v_ref[...] = _dot(h, w_ref[:, ATTN_WIDTH + KV_WIDTH:ATTN_WIDTH + 2 * KV_WIDTH])
    g_ref[...] = _dot(h, w_ref[:, ATTN_WIDTH + 2 * KV_WIDTH:]).astype(g_ref.dtype)


def _attn_in(x, nw, w, qn, kn, act_dtype, tm):
    n = x.shape[0]
    return pl.pallas_call(
        _attn_in_body,
        grid=(n // tm,),
        in_specs=[_row_spec(D_MODEL, tm), _const_spec(nw.shape), _const_spec(w.shape),
                  _const_spec(qn.shape), _const_spec(kn.shape)],
        out_specs=[_row_spec(ATTN_WIDTH, tm), _row_spec(KV_WIDTH, tm), _row_spec(KV_WIDTH, tm),
                   _row_spec(ATTN_WIDTH, tm)],
        out_shape=[jax.ShapeDtypeStruct((n, ATTN_WIDTH), act_dtype),
                   jax.ShapeDtypeStruct((n, KV_WIDTH), F32),
                   jax.ShapeDtypeStruct((n, KV_WIDTH), F32),
                   jax.ShapeDtypeStruct((n, ATTN_WIDTH), act_dtype)],
        compiler_params=_params(("parallel",)),
        name="attn_in",
    )(x, nw, w, qn, kn)


def _attn_block_body(is_meta, sink_ref, q_ref, kc_ref, kp_ref, vc_ref, vp_ref, km_ref, vm_ref, o_ref):
    j = pl.program_id(1)
    rows = GQA_GROUP * BLOCK
    row = lax.broadcasted_iota(jnp.int32, (rows, 2 * BLOCK), 0) & (BLOCK - 1)
    col = lax.broadcasted_iota(jnp.int32, (rows, 2 * BLOCK), 1)
    cur_ok = (col >= BLOCK) & (col - BLOCK <= row)
    if is_meta:
        mask = cur_ok
    else:
        n_prev = jnp.where(j == 0, N_META, BLOCK)
        mask = cur_ok | ((col < n_prev) & (col > row - (BLOCK - n_prev)))
    rowg = lax.broadcasted_iota(jnp.int32, (rows, 1), 0) // BLOCK
    scale = HEAD_DIM ** -0.5
    first = j == 0
    for kv in range(N_KV_HEADS):
        sl = slice(kv * HEAD_DIM, (kv + 1) * HEAD_DIM)
        kprev = jnp.where(first, km_ref[:, sl], kp_ref[:, sl])
        vprev = jnp.where(first, vm_ref[:, sl], vp_ref[:, sl])
        keys = jnp.concatenate([kprev, kc_ref[:, sl]], axis=0).astype(BF16)
        vals = jnp.concatenate([vprev, vc_ref[:, sl]], axis=0).astype(BF16)
        base = kv * GQA_GROUP * HEAD_DIM
        q4 = jnp.concatenate(
            [q_ref[:, base + g * HEAD_DIM:base + (g + 1) * HEAD_DIM] for g in range(GQA_GROUP)],
            axis=0).astype(BF16)
        s = _dot_nt(q4, keys) * scale
        s = jnp.where(mask, s, NEG)
        sink = jnp.zeros((rows, 1), F32)
        for g in range(GQA_GROUP):
            sink = jnp.where(rowg == g, sink_ref[kv * GQA_GROUP + g], sink)
        m = jnp.maximum(jnp.max(s, axis=-1, keepdims=True), sink)
        p = jnp.exp(s - m)
        denom = jnp.sum(p, axis=-1, keepdims=True) + jnp.exp(sink - m)
        o = _dot(p.astype(BF16), vals) / denom
        for g in range(GQA_GROUP):
            o_ref[:, base + g * HEAD_DIM:base + (g + 1) * HEAD_DIM] = (
                o[g * BLOCK:(g + 1) * BLOCK].astype(o_ref.dtype))


def _attn_blocks(q, k, v, k_small, v_small, sinks, n_batch, n_blocks, first_block, meta_block, is_meta, out_dtype):
    cur = lambda b, j: (first_block + b * n_blocks + j, 0)
    prev = lambda b, j: (first_block + b * n_blocks + jnp.maximum(j - 1, 0), 0)
    meta = lambda b, j: (meta_block, 0)
    out = lambda b, j: (b * n_blocks + j, 0)
    return pl.pallas_call(
        functools.partial(_attn_block_body, is_meta),
        grid=(n_batch, n_blocks),
        in_specs=[pl.BlockSpec(memory_space=pltpu.SMEM),
                  pl.BlockSpec((BLOCK, ATTN_WIDTH), cur),
                  pl.BlockSpec((BLOCK, KV_WIDTH), cur), pl.BlockSpec((BLOCK, KV_WIDTH), prev),
                  pl.BlockSpec((BLOCK, KV_WIDTH), cur), pl.BlockSpec((BLOCK, KV_WIDTH), prev),
                  pl.BlockSpec((BLOCK, KV_WIDTH), meta), pl.BlockSpec((BLOCK, KV_WIDTH), meta)],
        out_specs=pl.BlockSpec((BLOCK, ATTN_WIDTH), out),
        out_shape=jax.ShapeDtypeStruct((n_batch * n_blocks * BLOCK, ATTN_WIDTH), out_dtype),
        compiler_params=_params(("parallel", "parallel")),
        name="attn_meta" if is_meta else "attn_prompt",
    )(sinks, q, k, k, v, v, k_small, v_small)


def _attn_sample_body(sink_ref, q_ref, k_ref, v_ref, ck_ref, cv_ref, *rest):
    o_ref, nk_ref, nv_ref = rest[-3:]
    n_seq, rows, _ = q_ref.shape
    n_cache = ck_ref.shape[1]
    n_fresh = k_ref.shape[1]
    wbuf = n_cache // N_KV_HEADS
    per_kv = rows // N_KV_HEADS
    row = lax.broadcasted_iota(jnp.int32, (rows, n_cache + n_fresh), 0)
    col = lax.broadcasted_iota(jnp.int32, (rows, n_cache + n_fresh), 1)
    qpos = (row % per_kv) // GQA_GROUP
    kpos = jnp.where(col < n_cache, col // N_KV_HEADS - wbuf, (col - n_cache) // N_KV_HEADS)
    diff = qpos - kpos
    mask = (col % N_KV_HEADS == row // per_kv) & (diff >= 0) & (diff < WINDOW)
    row1 = lax.broadcasted_iota(jnp.int32, (rows, 1), 0)
    sink = jnp.zeros((rows, 1), F32)
    for hd in range(N_HEADS):
        sink = jnp.where((row1 // per_kv == hd // GQA_GROUP) & (row1 % GQA_GROUP == hd % GQA_GROUP),
                         sink_ref[hd], sink)
    scale = HEAD_DIM ** -0.5
    for s in range(n_seq):
        keys = jnp.concatenate([ck_ref[s], k_ref[s]], axis=0).astype(BF16)
        vals = jnp.concatenate([cv_ref[s], v_ref[s]], axis=0).astype(BF16)
        sc = _dot_nt(q_ref[s].astype(BF16), keys) * scale
        sc = jnp.where(mask, sc, NEG)
        m = jnp.maximum(jnp.max(sc, axis=-1, keepdims=True), sink)
        p = jnp.exp(sc - m)
        denom = jnp.sum(p, axis=-1, keepdims=True) + jnp.exp(sink - m)
        o_ref[s] = _dot(p.astype(BF16), vals) / denom
        nk_ref[s, 0:n_cache - n_fresh, :] = ck_ref[s, n_fresh:, :]
        nk_ref[s, n_cache - n_fresh:, :] = k_ref[s]
        nv_ref[s, 0:n_cache - n_fresh, :] = cv_ref[s, n_fresh:, :]
        nv_ref[s, n_cache - n_fresh:, :] = v_ref[s]


def _attn_sample(q, k, v, ck, cv, sinks, layer, nk_prev=None, nv_prev=None):
    sb = SAMPLE_SEQS
    n_seq = q.shape[0]
    blk = lambda a: pl.BlockSpec((sb,) + a.shape[1:], lambda i: (i,) + (0,) * (a.ndim - 1))
    cache_spec = pl.BlockSpec((None, sb) + ck.shape[2:], lambda i: (layer, i, 0, 0))
    in_specs = [pl.BlockSpec(memory_space=pltpu.SMEM), blk(q), blk(k), blk(v), cache_spec, cache_spec]
    args = [sinks, q, k, v, ck, cv]
    aliases = {}
    if nk_prev is not None:
        in_specs += [pl.BlockSpec(memory_space=pl.ANY)] * 2
        aliases = {len(args): 1, len(args) + 1: 2}
        args += [nk_prev, nv_prev]
    return pl.pallas_call(
        _attn_sample_body,
        grid=(n_seq // sb,),
        in_specs=in_specs,
        out_specs=[blk(q), cache_spec, cache_spec],
        out_shape=[jax.ShapeDtypeStruct(q.shape, F32),
                   jax.ShapeDtypeStruct(ck.shape, F32),
                   jax.ShapeDtypeStruct(cv.shape, F32)],
        input_output_aliases=aliases,
        compiler_params=_params(("parallel",)),
        name="attn_sample",
    )(*args)


def _attn_out_body(x_ref, o_ref, g_ref, w_ref, y_ref):
    a = o_ref[...].astype(F32) * _silu(g_ref[...].astype(F32))
    y_ref[...] = x_ref[...] + _dot(a.astype(BF16), w_ref[...])


def _attn_out(x, o, g, w, tm):
    n = x.shape[0]
    return pl.pallas_call(
        _attn_out_body,
        grid=(n // tm,),
        in_specs=[_row_spec(D_MODEL, tm), _row_spec(ATTN_WIDTH, tm), _row_spec(ATTN_WIDTH, tm),
                  _const_spec(w.shape)],
        out_specs=_row_spec(D_MODEL, tm),
        out_shape=jax.ShapeDtypeStruct(x.shape, F32),
        compiler_params=_params(("parallel",)),
        name="attn_out",
    )(x, o, g, w)


def _ssm_in_body(x_ref, nw_ref, w_ref, wdt_ref, z_ref, xbc_ref, dt_ref):
    h = _rmsnorm(x_ref[...], nw_ref[...]).astype(BF16)
    z_ref[...] = _dot(h, w_ref[:, :D_INNER]).astype(z_ref.dtype)
    xbc_ref[...] = _dot(h, w_ref[:, D_INNER:])
    dt_ref[...] = _dot(h, wdt_ref[...])


def _ssm_in(x, nw, w, wdt, act_dtype, tm):
    n = x.shape[0]
    return pl.pallas_call(
        _ssm_in_body,
        grid=(n // tm,),
        in_specs=[_row_spec(D_MODEL, tm), _const_spec(nw.shape), _const_spec(w.shape), _const_spec(wdt.shape)],
        out_specs=[_row_spec(D_INNER, tm), _row_spec(CONV_DIM, tm), _row_spec(LANES, tm)],
        out_shape=[jax.ShapeDtypeStruct((n, D_INNER), act_dtype),
                   jax.ShapeDtypeStruct((n, CONV_DIM), F32),
                   jax.ShapeDtypeStruct((n, LANES), F32)],
        compiler_params=_params(("parallel",)),
        name="ssm_in",
    )(x, nw, w, wdt)


def _lane_bcast(x, h):
    return jnp.broadcast_to(x[:, h:h + 1], (x.shape[0], LANES))


def _row_bcast(x, h):
    return jnp.broadcast_to(x[h:h + 1, :], (LANES, x.shape[1]))


def _pair_blockdiag(xp):
    lane = lax.broadcasted_iota(jnp.int32, xp.shape, 1)
    xb = xp.astype(BF16)
    zero = jnp.zeros_like(xb)
    return jnp.concatenate([jnp.where(lane < SSM_HEAD_DIM, xb, zero),
                            jnp.where(lane >= SSM_HEAD_DIM, xb, zero)], axis=0)


def _pair_select(a0, a1):
    lane = lax.broadcasted_iota(jnp.int32, a0.shape, 1)
    return jnp.where(lane < SSM_HEAD_DIM, a0, a1)


def _intra_pair(cb, cmask, a2, a2_t, dt_t, pair):
    ms = []
    for hh in range(2):
        h = 2 * pair + hh
        seg = _lane_bcast(a2, h) - _row_bcast(a2_t, h)
        dec = jnp.exp2(jnp.where(cmask, seg, NEG))
        ms.append((cb * dec * _row_bcast(dt_t, h)).astype(BF16))
    return jnp.concatenate(ms, axis=1)


def _pair_out_scale(e_all, pair):
    return _pair_select(_lane_bcast(e_all, 2 * pair), _lane_bcast(e_all, 2 * pair + 1))


def _ssd_chunk_body(is_meta, xbc_ref, dt_ref, mtile_ref, init_ref, cw_ref, cbias_ref, dtb_ref, alog_ref,
                    dsk_ref, e_ref, y_ref, st_out_ref, halo_ref, act_ref, st_ref):
    j = pl.program_id(1)
    n_chunks = pl.num_programs(1)
    q = BLOCK
    halo = CONV_HALO

    @pl.when(j == 0)
    def _():
        if is_meta:
            halo_ref[...] = jnp.zeros((halo, CONV_DIM), F32)
            st_ref[...] = jnp.zeros(st_ref.shape, F32)
        else:
            halo_ref[...] = mtile_ref[0:halo, :]
            for pair in range(SSM_HEADS // 2):
                ps = slice(pair * LANES, (pair + 1) * LANES)
                st_ref[:, ps] = init_ref[0, ps, :].T

    n_sh = CONV_W - 1
    r = lax.broadcasted_iota(jnp.int32, (n_sh * q, 2 * q), 0)
    c = lax.broadcasted_iota(jnp.int32, (n_sh * q, 2 * q), 1)
    sel = (c % q == r % q - (r // q + 1)).astype(BF16)
    top = 8
    rh = lax.broadcasted_iota(jnp.int32, (n_sh * top, 2 * halo), 0)
    ch = lax.broadcasted_iota(jnp.int32, (n_sh * top, 2 * halo), 1)
    sel_halo = (ch % halo == halo + rh % top - (rh // top + 1)).astype(BF16)
    width = 512
    for cc in range(CONV_DIM // width):
        cs = slice(cc * width, (cc + 1) * width)
        xc = xbc_ref[:, cs]
        hi = xc.astype(BF16)
        lo = (xc - hi.astype(F32)).astype(BF16)
        sh = _dot(sel, jnp.concatenate([hi, lo], axis=0))
        conv = cbias_ref[:, cs] + cw_ref[CONV_W - 1:CONV_W, cs] * xc
        for k in range(1, CONV_W):
            conv = conv + cw_ref[CONV_W - 1 - k:CONV_W - k, cs] * sh[(k - 1) * q:k * q, :]
        act_ref[:, cs] = _silu(conv)
        xh_ = halo_ref[:, cs]
        hhi = xh_.astype(BF16)
        hlo = (xh_ - hhi.astype(F32)).astype(BF16)
        shh = _dot(sel_halo, jnp.concatenate([hhi, hlo], axis=0))
        conv_top = conv[0:top, :]
        for k in range(1, CONV_W):
            conv_top = conv_top + cw_ref[CONV_W - 1 - k:CONV_W - k, cs] * shh[(k - 1) * top:k * top, :]
        act_ref[0:top, cs] = _silu(conv_top)
    halo_ref[...] = xbc_ref[q - halo:, :]

    ri = lax.broadcasted_iota(jnp.int32, (q, q), 0)
    ci = lax.broadcasted_iota(jnp.int32, (q, q), 1)
    dt_ok = ci < SSM_HEADS
    if is_meta:
        dt_ok = dt_ok & (ri < N_META)
    dt = jnp.where(dt_ok, _softplus(dt_ref[...] + dtb_ref[...]), 0.0)
    a = dt * (-jnp.exp(alog_ref[...]))
    cmask = ri >= ci
    a2 = _exact_left(cmask.astype(BF16), a) * LOG2E
    a2_t = a2.T
    dt_t = dt.T
    e_all = jnp.exp2(a2)
    w_t = dt_t * jnp.exp2(jnp.broadcast_to(a2_t[:, q - 1:q], (q, q)) - a2_t)
    dec_rows = _exact_right(jnp.broadcast_to(e_all[q - 1:q, :], (8, LANES)), e_ref[...])[0:1, :]
    rowi = lax.broadcasted_iota(jnp.int32, (q, LANES), 0)

    for g in range(SSM_GROUPS):
        gs = slice(D_INNER + g * D_STATE, D_INNER + (g + 1) * D_STATE)
        bg = act_ref[:, gs]
        cg = act_ref[:, SSM_GROUPS * D_STATE + gs.start:SSM_GROUPS * D_STATE + gs.stop].astype(BF16)
        cb = _dot_nt(cg, bg.astype(BF16))
        bg_t = bg.T
        y_off = _dot(cg, st_ref[:, g * GROUP_WIDTH:(g + 1) * GROUP_WIDTH].astype(BF16))
        for pr in range(PAIRS_PER_GROUP):
            pair = g * PAIRS_PER_GROUP + pr
            ps = slice(pair * LANES, (pair + 1) * LANES)
            xp = act_ref[:, ps]
            if is_meta:
                xp = jnp.where(rowi < N_META, xp, 0.0)
            xbd = _pair_blockdiag(xp)
            y = _dot(_intra_pair(cb, cmask, a2, a2_t, dt_t, pair), xbd)
            st = st_ref[:, ps]
            y = y + y_off[:, pr * LANES:(pr + 1) * LANES] * _pair_out_scale(e_all, pair)
            wn = jnp.concatenate([(bg_t * _row_bcast(w_t, 2 * pair)).astype(BF16),
                                  (bg_t * _row_bcast(w_t, 2 * pair + 1)).astype(BF16)], axis=1)
            st_ref[:, ps] = st * dec_rows[:, ps] + _dot(wn, xbd)
            y_ref[:, ps] = (y + dsk_ref[:, ps] * xp).astype(y_ref.dtype)

    @pl.when(j == n_chunks - 1)
    def _():
        for pair in range(SSM_HEADS // 2):
            ps = slice(pair * LANES, (pair + 1) * LANES)
            st_out_ref[0, ps, :] = st_ref[:, ps].T


def _ssd_chunks(xbc, dt, xbc_small, init, cw, cbias, dtb, alog, dsk, emat, n_batch, n_chunks, first_block,
                meta_block, is_meta, out_dtype):
    cur = lambda b, j: (first_block + b * n_chunks + j, 0)
    out = lambda b, j: (b * n_chunks + j, 0)
    return pl.pallas_call(
        functools.partial(_ssd_chunk_body, is_meta),
        grid=(n_batch, n_chunks),
        in_specs=[pl.BlockSpec((BLOCK, CONV_DIM), cur), pl.BlockSpec((BLOCK, LANES), cur),
                  pl.BlockSpec((BLOCK, CONV_DIM), lambda b, j: (meta_block, 0)), _const_spec(init.shape),
                  _const_spec(cw.shape), _const_spec(cbias.shape), _const_spec(dtb.shape),
                  _const_spec(alog.shape), _const_spec(dsk.shape), _const_spec(emat.shape)],
        out_specs=[pl.BlockSpec((BLOCK, D_INNER), out),
                   pl.BlockSpec((1, D_INNER, D_STATE), lambda b, j: (b, 0, 0))],
        out_shape=[jax.ShapeDtypeStruct((n_batch * n_chunks * BLOCK, D_INNER), out_dtype),
                   jax.ShapeDtypeStruct((n_batch, D_INNER, D_STATE), F32)],
        scratch_shapes=[pltpu.VMEM((CONV_HALO, CONV_DIM), F32), pltpu.VMEM((BLOCK, CONV_DIM), F32),
                        pltpu.VMEM((D_STATE, D_INNER), F32)],
        compiler_params=_params(("parallel", "arbitrary")),
        name="ssd_meta" if is_meta else "ssd_prompt",
    )(xbc, dt, xbc_small, init, cw, cbias, dtb, alog, dsk, emat)


def _ssd_sample_body(xbc_ref, dt_ref, cs_ref, st_ref, cw_ref, cbias_ref, dtb_ref, alog_ref, dsk_ref, e_ref,
                     *rest):
    y_ref, nst_ref, yt_ref = rest[-3:]
    n_seq = st_ref.shape[0]
    n_new = xbc_ref.shape[0] // n_seq
    n_cs = CONV_W - 1
    qr = n_seq * n_new
    q = LANES
    kpad = 64
    assert qr + n_seq * n_cs <= kpad and n_new >= n_cs

    x_new = xbc_ref[...]
    xc = jnp.concatenate([x_new, cs_ref[...], jnp.zeros((kpad - qr - n_seq * n_cs, CONV_DIM), F32)], axis=0)
    r = lax.broadcasted_iota(jnp.int32, (qr, kpad), 0)
    c = lax.broadcasted_iota(jnp.int32, (qr, kpad), 1)
    s_of_r = r // n_new
    t_of_r = r % n_new
    conv = cbias_ref[...] + cw_ref[CONV_W - 1:CONV_W, :] * x_new
    for k in range(1, CONV_W):
        target = jnp.where(t_of_r >= k, r - k, qr + n_cs * s_of_r + t_of_r + n_cs - k)
        conv = conv + cw_ref[CONV_W - 1 - k:CONV_W - k, :] * _exact_left((c == target).astype(BF16), xc)
    act = jnp.concatenate([_silu(conv), jnp.zeros((q - qr, CONV_DIM), F32)], axis=0)
    xh = act[:, :D_INNER]
    bm = act[:, D_INNER:D_INNER + SSM_GROUPS * D_STATE]
    cm = act[:, D_INNER + SSM_GROUPS * D_STATE:]
    ri = lax.broadcasted_iota(jnp.int32, (q, q), 0)
    ci = lax.broadcasted_iota(jnp.int32, (q, q), 1)
    dt = jnp.concatenate([_softplus(dt_ref[...] + dtb_ref[...]), jnp.zeros((q - qr, LANES), F32)], axis=0)
    dt = jnp.where(ci < SSM_HEADS, dt, 0.0)
    a = dt * (-jnp.exp(alog_ref[...]))
    cmask = (ri >= ci) & (ri // n_new == ci // n_new) & (ri < qr)
    a_cum = _exact_left(cmask.astype(BF16), a)
    a2 = a_cum * LOG2E
    a2_t = a2.T
    dt_t = dt.T
    e_all = jnp.exp2(a2)
    last = ((ci == (ri // n_new) * n_new + n_new - 1) & (ri < qr)).astype(BF16)
    a2_last = _exact_left(last, a_cum) * LOG2E
    w_exp = _exact_right(dt * jnp.exp2(a2_last - a2), e_ref[...])
    dec_exp = _exact_right(jnp.exp2(a2_last), e_ref[...])
    xw = xh * w_exp

    yt_ref[...] = jnp.zeros(yt_ref.shape, F32)
    colseq = ci // n_new
    rowseq = ri // n_new
    for g in range(SSM_GROUPS):
        gs = slice(g * D_STATE, (g + 1) * D_STATE)
        hs = slice(g * GROUP_WIDTH, (g + 1) * GROUP_WIDTH)
        cg_t = cm[:, gs].T.astype(BF16)
        bg = bm[:, gs].astype(BF16)
        xw_t = jnp.concatenate(
            [xw[:, g * GROUP_WIDTH + i * LANES:g * GROUP_WIDTH + (i + 1) * LANES].T
             for i in range(GROUP_WIDTH // LANES)], axis=0).astype(BF16)
        dec_t = jnp.concatenate(
            [dec_exp[:, g * GROUP_WIDTH + i * LANES:g * GROUP_WIDTH + (i + 1) * LANES].T
             for i in range(GROUP_WIDTH // LANES)], axis=0)
        zero = jnp.zeros((q, q), BF16)
        for s in range(n_seq):
            st0 = st_ref[s, hs, :]
            yt_ref[hs, :] += _dot(st0.astype(BF16), jnp.where(colseq == s, cg_t, zero))
            inc = _dot(xw_t, jnp.where(rowseq == s, bg, zero))
            dcol = jnp.broadcast_to(dec_t[:, s * n_new:s * n_new + 1], (GROUP_WIDTH, q))
            nst_ref[s, hs, :] = st0 * dcol + inc

    for g in range(SSM_GROUPS):
        gs = slice(g * D_STATE, (g + 1) * D_STATE)
        cb = _dot_nt(cm[:, gs].astype(BF16), bm[:, gs].astype(BF16))
        for pr in range(PAIRS_PER_GROUP):
            pair = g * PAIRS_PER_GROUP + pr
            ps = slice(pair * LANES, (pair + 1) * LANES)
            xp = xh[:, ps]
            y = _dot(_intra_pair(cb, cmask, a2, a2_t, dt_t, pair), _pair_blockdiag(xp))
            y = y + yt_ref[ps, :].T * _pair_out_scale(e_all, pair) + dsk_ref[:, ps] * xp
            y_ref[:, ps] = y[0:qr, :]


def _ssd_sample(xbc, dt, cs, st, cw, cbias, dtb, alog, dsk, emat, n_new, layer, nst_prev=None):
    n_seq = st.shape[1]
    sb = SAMPLE_SEQS
    rows = lambda cols, per: pl.BlockSpec((sb * per, cols), lambda i: (i, 0))
    st_spec = pl.BlockSpec((None, sb, D_INNER, D_STATE), lambda i: (layer, i, 0, 0))
    in_specs = [rows(CONV_DIM, n_new), rows(LANES, n_new), rows(CONV_DIM, CONV_W - 1), st_spec,
                _const_spec(cw.shape), _const_spec(cbias.shape), _const_spec(dtb.shape),
                _const_spec(alog.shape), _const_spec(dsk.shape), _const_spec(emat.shape)]
    args = [xbc, dt, cs, st, cw, cbias, dtb, alog, dsk, emat]
    aliases = {}
    if nst_prev is not None:
        in_specs.append(pl.BlockSpec(memory_space=pl.ANY))
        aliases = {len(args): 1}
        args.append(nst_prev)
    return pl.pallas_call(
        _ssd_sample_body,
        grid=(n_seq // sb,),
        in_specs=in_specs,
        out_specs=[rows(D_INNER, n_new), st_spec],
        out_shape=[jax.ShapeDtypeStruct((n_seq * n_new, D_INNER), F32),
                   jax.ShapeDtypeStruct(st.shape, F32)],
        input_output_aliases=aliases,
        scratch_shapes=[pltpu.VMEM((D_INNER, LANES), F32)],
        compiler_params=_params(("parallel",)),
        name="ssd_sample",
    )(*args)


def _ssm_out_body(x_ref, y_ref, z_ref, nw_ref, w_ref, o_ref):
    y = y_ref[...].astype(F32) * _silu(z_ref[...].astype(F32))
    parts = []
    for g in range(SSM_GROUPS):
        yg = y[:, g * GROUP_WIDTH:(g + 1) * GROUP_WIDTH]
        parts.append(yg * lax.rsqrt(jnp.mean(yg * yg, axis=-1, keepdims=True) + EPS))
    yn = (jnp.concatenate(parts, axis=1) * nw_ref[...]).astype(BF16)
    o_ref[...] = x_ref[...] + _dot(yn, w_ref[...])


def _ssm_out(x, y, z, nw, w, tm):
    n = x.shape[0]
    return pl.pallas_call(
        _ssm_out_body,
        grid=(n // tm,),
        in_specs=[_row_spec(D_MODEL, tm), _row_spec(D_INNER, tm), _row_spec(D_INNER, tm),
                  _const_spec(nw.shape), _const_spec(w.shape)],
        out_specs=_row_spec(D_MODEL, tm),
        out_shape=jax.ShapeDtypeStruct(x.shape, F32),
        compiler_params=_params(("parallel",)),
        name="ssm_out",
    )(x, y, z, nw, w)


def _pad_lanes(v):
    return jnp.pad(v.astype(F32), (0, LANES - v.shape[0]))[None, :]


def kernel(x_prompt, x_sample, cache_k, cache_v, state_conv, state_ssm, meta_tokens, norm_w,
           w_attn_in, q_norm_w, k_norm_w, attn_sinks, w_attn_out, w_ssm_in, conv_w, conv_b,
           dt_bias, a_log, d_skip, ssm_norm_w, w_ssm_out):
    n_batch, seq, _ = x_prompt.shape
    n_seq, n_new, _ = x_sample.shape
    wbuf = cache_k.shape[2]
    n_blocks = seq // BLOCK
    n_rows = n_seq * n_new
    assert seq % BLOCK == 0 and wbuf == WINDOW and n_rows % BLOCK == 0 and n_new == 4
    meta_block = n_rows // BLOCK
    small_rows = n_rows + BLOCK

    xp = x_prompt.reshape(n_batch * seq, D_MODEL)
    xs = jnp.concatenate([x_sample.reshape(n_rows, D_MODEL), meta_tokens.astype(F32),
                          jnp.zeros((BLOCK - N_META, D_MODEL), F32)], axis=0)

    emat = jnp.pad(jnp.repeat(jnp.eye(SSM_HEADS, dtype=BF16), SSM_HEAD_DIM, axis=1),
                   ((0, LANES - SSM_HEADS), (0, 0)))
    zero_state = jnp.zeros((1, D_INNER, D_STATE), F32)

    n_attn, n_ssm = cache_k.shape[0], state_ssm.shape[0]
    ck_all = cache_k.reshape(n_attn, n_seq, wbuf * N_KV_HEADS, HEAD_DIM)
    cv_all = cache_v.reshape(n_attn, n_seq, wbuf * N_KV_HEADS, HEAD_DIM)
    st_all = state_ssm.reshape(n_ssm, n_seq, D_INNER, D_STATE)
    new_ck = new_cv = new_st = None

    kp_l, vp_l, cp_l, sp_l, cs_l = [], [], [], [], []
    for i in range(DEPTH):
        l = i // 2
        nw = norm_w[i][None, :]
        if i % 2 == 0:
            w_in = w_attn_in[l].astype(BF16)
            w_out = w_attn_out[l].astype(BF16)
            qn, kn = q_norm_w[l][None, :], k_norm_w[l][None, :]
            sinks = attn_sinks[l].astype(F32)

            qs, ks, vs, gs = _attn_in(xs, nw, w_in, qn, kn, F32, small_rows)
            o_meta = _attn_blocks(qs, ks, vs, ks, vs, sinks, 1, 1, meta_block, meta_block, True, F32)
            qq = qs[:n_rows].reshape(n_seq, n_new, N_KV_HEADS, GQA_GROUP * HEAD_DIM).transpose(0, 2, 1, 3)
            o, new_ck, new_cv = _attn_sample(qq.reshape(n_seq, N_HEADS * n_new, HEAD_DIM),
                                             ks[:n_rows].reshape(n_seq, n_new * N_KV_HEADS, HEAD_DIM),
                                             vs[:n_rows].reshape(n_seq, n_new * N_KV_HEADS, HEAD_DIM),
                                             ck_all, cv_all, sinks, l, new_ck, new_cv)
            o = o.reshape(n_seq, N_KV_HEADS, n_new, GQA_GROUP * HEAD_DIM).transpose(0, 2, 1, 3)
            o = jnp.concatenate([o.reshape(n_rows, ATTN_WIDTH), o_meta], axis=0)

            q, k, v, g = _attn_in(xp, nw, w_in, qn, kn, BF16, ROW_TILE)
            op = _attn_blocks(q, k, v, ks, vs, sinks, n_batch, n_blocks, 0, meta_block, False, BF16)
            xp = _attn_out(xp, op, g, w_out, ROW_TILE)
            xs = _attn_out(xs, o, gs, w_out, small_rows)
            kp_l.append(k.reshape(n_batch, seq, KV_WIDTH)[:, -WINDOW:].reshape(n_batch, WINDOW, N_KV_HEADS, HEAD_DIM))
            vp_l.append(v.reshape(n_batch, seq, KV_WIDTH)[:, -WINDOW:].reshape(n_batch, WINDOW, N_KV_HEADS, HEAD_DIM))
        else:
            w_in = w_ssm_in[l][:, :D_INNER + CONV_DIM].astype(BF16)
            w_dt = jnp.pad(w_ssm_in[l][:, D_INNER + CONV_DIM:], ((0, 0), (0, LANES - SSM_HEADS))).astype(BF16)
            w_out = w_ssm_out[l].astype(BF16)
            cw, cbias = conv_w[l].astype(F32), conv_b[l][None, :].astype(F32)
            dtb, alog = _pad_lanes(dt_bias[l]), _pad_lanes(a_log[l])
            dsk = jnp.repeat(d_skip[l].astype(F32), SSM_HEAD_DIM)[None, :]
            snw = ssm_norm_w[l][None, :].astype(F32)
            consts = (cw, cbias, dtb, alog, dsk, emat)

            zs, xbcs, dts = _ssm_in(xs, nw, w_in, w_dt, F32, small_rows)
            y_meta, st_meta = _ssd_chunks(xbcs, dts, xbcs, zero_state, *consts, 1, 1, meta_block, meta_block,
                                          True, F32)
            y, new_st = _ssd_sample(xbcs, dts, state_conv[l].reshape(n_seq * (CONV_W - 1), CONV_DIM), st_all,
                                    *consts, n_new, l, new_st)
            y = jnp.concatenate([y, y_meta], axis=0)

            z, xbc, dt = _ssm_in(xp, nw, w_in, w_dt, BF16, ROW_TILE)
            yp, st = _ssd_chunks(xbc, dt, xbcs, st_meta, *consts, n_batch, n_blocks, 0, meta_block, False, BF16)
            xp = _ssm_out(xp, yp, z, snw, w_out, ROW_TILE)
            xs = _ssm_out(xs, y, zs, snw, w_out, small_rows)
            cp_l.append(xbc.reshape(n_batch, seq, CONV_DIM)[:, -(CONV_W - 1):])
            sp_l.append(st.reshape(n_batch, SSM_HEADS, SSM_HEAD_DIM, D_STATE))
            cs_l.append(xbcs[:n_rows].reshape(n_seq, n_new, CONV_DIM)[:, -(CONV_W - 1):])

    y_prompt = xp.reshape(n_batch, seq, D_MODEL)
    y_sample = xs[:n_rows].reshape(n_seq, n_new, D_MODEL)
    return (y_prompt, y_sample,
            jnp.stack(kp_l), jnp.stack(vp_l), jnp.stack(cp_l), jnp.stack(sp_l),
            new_ck.reshape(cache_k.shape), new_cv.reshape(cache_v.shape), jnp.stack(cs_l),
            new_st.reshape(state_ssm.shape))
```

```python
import functools
import math

import jax
import jax.numpy as jnp
from jax import lax
from jax.experimental import pallas as pl
from jax.experimental.pallas import tpu as pltpu

F32 = jnp.float32
BF16 = jnp.bfloat16

D_MODEL = 1024
DEPTH = 4
N_META = 16
WINDOW = 128
BLOCK = 128
HEAD_DIM = 128
ATTN_WIDTH = 2 * D_MODEL
N_HEADS = ATTN_WIDTH // HEAD_DIM
N_KV_HEADS = 4
GQA_GROUP = N_HEADS // N_KV_HEADS
KV_WIDTH = N_KV_HEADS * HEAD_DIM
D_INNER = 2 * D_MODEL
SSM_HEAD_DIM = 64
SSM_HEADS = D_INNER // SSM_HEAD_DIM
D_STATE = 128
SSM_GROUPS = 4
GROUP_WIDTH = D_INNER // SSM_GROUPS
PAIRS_PER_GROUP = SSM_HEADS // SSM_GROUPS // 2
CONV_W = 4
CONV_DIM = D_INNER + 2 * SSM_GROUPS * D_STATE
EPS = 1e-6
NEG = -1e30
LOG2E = math.log2(math.e)
QK_SCALE_LOG2 = HEAD_DIM ** -0.5 * LOG2E

LANES = 128
ROW_TILE = 512
SAMPLE_SEQS = 8
CONV_HALO = 16
V7X_VMEM_BYTES = 64 * 1024 * 1024
VMEM_LIMIT = V7X_VMEM_BYTES * 7 // 8


def _dot(a, b):
    return jnp.dot(a, b, preferred_element_type=F32)


def _dot_nt(a, b):
    return lax.dot_general(a, b, (((1,), (1,)), ((), ())), preferred_element_type=F32)


def _rmsnorm(x, w):
    return x * lax.rsqrt(jnp.mean(x * x, axis=-1, keepdims=True) + EPS) * w


def _silu(x):
    return x * jax.nn.sigmoid(x)


def _softplus(x):
    return jnp.maximum(x, 0.0) + jnp.log1p(jnp.exp(-jnp.abs(x)))


def _split3(x):
    hi = x.astype(BF16)
    r = x - hi.astype(F32)
    mid = r.astype(BF16)
    lo = (r - mid.astype(F32)).astype(BF16)
    return hi, mid, lo


def _exact_left(p, x):
    hi, mid, lo = _split3(x)
    return _dot(p, hi) + _dot(p, mid) + _dot(p, lo)


def _exact_right(x, e):
    hi, mid, lo = _split3(x)
    return _dot(hi, e) + _dot(mid, e) + _dot(lo, e)


def _params(semantics):
    return pltpu.CompilerParams(dimension_semantics=semantics, vmem_limit_bytes=VMEM_LIMIT)


def _const_spec(shape):
    nd = len(shape)
    return pl.BlockSpec(shape, lambda *_: (0,) * nd)


def _row_spec(cols, tm):
    return pl.BlockSpec((tm, cols), lambda i: (i, 0))


def _attn_in_body(x_ref, nw_ref, w_ref, qn_ref, kn_ref, q_ref, k_ref, v_ref, g_ref):
    h = _rmsnorm(x_ref[...], nw_ref[...]).astype(BF16)
    q = _dot(h, w_ref[:, :ATTN_WIDTH])
    for hd in range(N_HEADS):
        sl = slice(hd * HEAD_DIM, (hd + 1) * HEAD_DIM)
        q_ref[:, sl] = (_rmsnorm(q[:, sl], qn_ref[...]) * QK_SCALE_LOG2).astype(q_ref.dtype)
    k = _dot(h, w_ref[:, ATTN_WIDTH:ATTN_WIDTH + KV_WIDTH])
    for hd in range(N_KV_HEADS):
        sl = slice(hd * HEAD_DIM, (hd + 1) * HEAD_DIM)
        k_ref[:, sl] = _rmsnorm(k[:, sl], kn_ref[...])
    v_ref[...] = _dot(h, w_ref[:, ATTN_WIDTH + KV_WIDTH:ATTN_WIDTH + 2 * KV_WIDTH])
    g_ref[...] = _dot(h, w_ref[:, ATTN_WIDTH + 2 * KV_WIDTH:]).astype(g_ref.dtype)


def _attn_in(x, nw, w, qn, kn, act_dtype, tm):
    n = x.shape[0]
    return pl.pallas_call(
        _attn_in_body,
        grid=(n // tm,),
        in_specs=[_row_spec(D_MODEL, tm), _const_spec(nw.shape), _const_spec(w.shape),
                  _const_spec(qn.shape), _const_spec(kn.shape)],
        out_specs=[_row_spec(ATTN_WIDTH, tm), _row_spec(KV_WIDTH, tm), _row_spec(KV_WIDTH, tm),
                   _row_spec(ATTN_WIDTH, tm)],
        out_shape=[jax.ShapeDtypeStruct((n, ATTN_WIDTH), act_dtype),
                   jax.ShapeDtypeStruct((n, KV_WIDTH), F32),
                   jax.ShapeDtypeStruct((n, KV_WIDTH), F32),
                   jax.ShapeDtypeStruct((n, ATTN_WIDTH), act_dtype)],
        compiler_params=_params(("parallel",)),
        name="attn_in",
    )(x, nw, w, qn, kn)


def _attn_block_body(is_meta, sink_ref, q_ref, kc_ref, kp_ref, vc_ref, vp_ref, km_ref, vm_ref, o_ref):
    j = pl.program_id(1)
    rows = GQA_GROUP * BLOCK
    rowg = lax.broadcasted_iota(jnp.int32, (rows, 1), 0) // BLOCK

    def heads(kv, kprev_ref, vprev_ref):
        sl = slice(kv * HEAD_DIM, (kv + 1) * HEAD_DIM)
        keys = jnp.concatenate([kprev_ref[:, sl], kc_ref[:, sl]], axis=0).astype(BF16)
        vals = jnp.concatenate([vprev_ref[:, sl], vc_ref[:, sl]], axis=0).astype(BF16)
        base = kv * GQA_GROUP * HEAD_DIM
        q4 = jnp.concatenate(
            [q_ref[:, base + g * HEAD_DIM:base + (g + 1) * HEAD_DIM] for g in range(GQA_GROUP)],
            axis=0).astype(BF16)
        sink = jnp.zeros((rows, 1), F32)
        for g in range(GQA_GROUP):
            sink = jnp.where(rowg == g, sink_ref[kv * GQA_GROUP + g] * LOG2E, sink)
        return _dot_nt(q4, keys), vals, sink, base

    def store(o, base):
        for g in range(GQA_GROUP):
            o_ref[:, base + g * HEAD_DIM:base + (g + 1) * HEAD_DIM] = (
                o[g * BLOCK:(g + 1) * BLOCK].astype(o_ref.dtype))

    def masked_path(kprev_ref, vprev_ref, n_prev):
        row = lax.broadcasted_iota(jnp.int32, (rows, 2 * BLOCK), 0) & (BLOCK - 1)
        col = lax.broadcasted_iota(jnp.int32, (rows, 2 * BLOCK), 1)
        mask = (col >= BLOCK) & (col - BLOCK <= row)
        if n_prev:
            mask = mask | ((col < n_prev) & (col > row - (BLOCK - n_prev)))
        for kv in range(N_KV_HEADS):
            s, vals, sink, base = heads(kv, kprev_ref, vprev_ref)
            s = jnp.where(mask, s, NEG)
            m = jnp.maximum(jnp.max(s, axis=-1, keepdims=True), sink)
            p = jnp.exp2(s - m)
            denom = jnp.sum(p, axis=-1, keepdims=True) + jnp.exp2(sink - m)
            store(_dot(p.astype(BF16), vals) / denom, base)

    if is_meta:
        masked_path(km_ref, vm_ref, 0)
    else:
        @pl.when(j == 0)
        def _():
            masked_path(km_ref, vm_ref, N_META)

        @pl.when(j > 0)
        def _():
            masked_path(kp_ref, vp_ref, BLOCK)


def _attn_blocks(q, k, v, k_small, v_small, sinks, n_batch, n_blocks, first_block, meta_block, is_meta, out_dtype):
    cur = lambda b, j: (first_block + b * n_blocks + j, 0)
    prev = lambda b, j: (first_block + b * n_blocks + jnp.maximum(j - 1, 0), 0)
    meta = lambda b, j: (meta_block, 0)
    out = lambda b, j: (b * n_blocks + j, 0)
    return pl.pallas_call(
        functools.partial(_attn_block_body, is_meta),
        grid=(n_batch, n_blocks),
        in_specs=[pl.BlockSpec(memory_space=pltpu.SMEM),
                  pl.BlockSpec((BLOCK, ATTN_WIDTH), cur),
                  pl.BlockSpec((BLOCK, KV_WIDTH), cur), pl.BlockSpec((BLOCK, KV_WIDTH), prev),
                  pl.BlockSpec((BLOCK, KV_WIDTH), cur), pl.BlockSpec((BLOCK, KV_WIDTH), prev),
                  pl.BlockSpec((BLOCK, KV_WIDTH), meta), pl.BlockSpec((BLOCK, KV_WIDTH), meta)],
        out_specs=pl.BlockSpec((BLOCK, ATTN_WIDTH), out),
        out_shape=jax.ShapeDtypeStruct((n_batch * n_blocks * BLOCK, ATTN_WIDTH), out_dtype),
        compiler_params=_params(("parallel", "parallel")),
        name="attn_meta" if is_meta else "attn_prompt",
    )(sinks, q, k, k, v, v, k_small, v_small)


def _attn_sample_body(fill_layers, sink_ref, q_ref, k_ref, v_ref, ck_ref, cv_ref, *rest):
    o_ref, nk_ref, nv_ref = rest[-3:]
    if fill_layers:
        layer, n_layers = fill_layers
        for other in range(n_layers):
            if other != layer:
                nk_ref[other] = jnp.zeros(nk_ref.shape[1:], F32)
                nv_ref[other] = jnp.zeros(nv_ref.shape[1:], F32)
        nk_ref, nv_ref = nk_ref.at[layer], nv_ref.at[layer]
    n_seq, rows, _ = q_ref.shape
    n_cache = ck_ref.shape[1]
    n_fresh = k_ref.shape[1]
    wbuf = n_cache // N_KV_HEADS
    per_kv = rows // N_KV_HEADS
    row = lax.broadcasted_iota(jnp.int32, (rows, n_cache + n_fresh), 0)
    col = lax.broadcasted_iota(jnp.int32, (rows, n_cache + n_fresh), 1)
    qpos = (row % per_kv) // GQA_GROUP
    kpos = jnp.where(col < n_cache, col // N_KV_HEADS - wbuf, (col - n_cache) // N_KV_HEADS)
    diff = qpos - kpos
    mask = (col % N_KV_HEADS == row // per_kv) & (diff >= 0) & (diff < WINDOW)
    row1 = lax.broadcasted_iota(jnp.int32, (rows, 1), 0)
    sink = jnp.zeros((rows, 1), F32)
    for hd in range(N_HEADS):
        sink = jnp.where((row1 // per_kv == hd // GQA_GROUP) & (row1 % GQA_GROUP == hd % GQA_GROUP),
                         sink_ref[hd] * LOG2E, sink)
    for s in range(n_seq):
        keys = jnp.concatenate([ck_ref[s], k_ref[s]], axis=0).astype(BF16)
        vals = jnp.concatenate([cv_ref[s], v_ref[s]], axis=0).astype(BF16)
        sc = _dot_nt(q_ref[s].astype(BF16), keys)
        sc = jnp.where(mask, sc, NEG)
        m = jnp.maximum(jnp.max(sc, axis=-1, keepdims=True), sink)
        p = jnp.exp2(sc - m)
        denom = jnp.sum(p, axis=-1, keepdims=True) + jnp.exp2(sink - m)
        o_ref[s] = _dot(p.astype(BF16), vals) / denom
        nk_ref[s, 0:n_cache - n_fresh, :] = ck_ref[s, n_fresh:, :]
        nk_ref[s, n_cache - n_fresh:, :] = k_ref[s]
        nv_ref[s, 0:n_cache - n_fresh, :] = cv_ref[s, n_fresh:, :]
        nv_ref[s, n_cache - n_fresh:, :] = v_ref[s]


def _attn_sample(q, k, v, ck, cv, sinks, layer, nk_prev=None, nv_prev=None):
    sb = SAMPLE_SEQS
    n_seq = q.shape[0]
    n_layers = ck.shape[0]
    blk = lambda a: pl.BlockSpec((sb,) + a.shape[1:], lambda i: (i,) + (0,) * (a.ndim - 1))
    cache_spec = pl.BlockSpec((None, sb) + ck.shape[2:], lambda i: (layer, i, 0, 0))
    in_specs = [pl.BlockSpec(memory_space=pltpu.SMEM), blk(q), blk(k), blk(v), cache_spec, cache_spec]
    args = [sinks, q, k, v, ck, cv]
    if nk_prev is None:
        aliases, fill_layers = {}, (layer, n_layers)
        new_spec = pl.BlockSpec((n_layers, sb) + ck.shape[2:], lambda i: (0, i, 0, 0))
    else:
        in_specs += [pl.BlockSpec(memory_space=pl.ANY)] * 2
        aliases, fill_layers = {len(args): 1, len(args) + 1: 2}, None
        new_spec = cache_spec
        args += [nk_prev, nv_prev]
    return pl.pallas_call(
        functools.partial(_attn_sample_body, fill_layers),
        grid=(n_seq // sb,),
        in_specs=in_specs,
        out_specs=[blk(q), new_spec, new_spec],
        out_shape=[jax.ShapeDtypeStruct(q.shape, F32),
                   jax.ShapeDtypeStruct(ck.shape, F32),
                   jax.ShapeDtypeStruct(cv.shape, F32)],
        input_output_aliases=aliases,
        compiler_params=_params(("parallel",)),
        name="attn_sample",
    )(*args)


def _attn_out_body(x_ref, o_ref, g_ref, w_ref, y_ref):
    a = o_ref[...].astype(F32) * _silu(g_ref[...].astype(F32))
    y_ref[...] = x_ref[...] + _dot(a.astype(BF16), w_ref[...])


def _attn_out(x, o, g, w, tm):
    n = x.shape[0]
    return pl.pallas_call(
        _attn_out_body,
        grid=(n // tm,),
        in_specs=[_row_spec(D_MODEL, tm), _row_spec(ATTN_WIDTH, tm), _row_spec(ATTN_WIDTH, tm),
                  _const_spec(w.shape)],
        out_specs=_row_spec(D_MODEL, tm),
        out_shape=jax.ShapeDtypeStruct(x.shape, F32),
        compiler_params=_params(("parallel",)),
        name="attn_out",
    )(x, o, g, w)


def _ssm_in_body(x_ref, nw_ref, w_ref, wdt_ref, z_ref, xbc_ref, dt_ref):
    h = _rmsnorm(x_ref[...], nw_ref[...]).astype(BF16)
    z_ref[...] = _dot(h, w_ref[:, :D_INNER]).astype(z_ref.dtype)
    xbc_ref[...] = _dot(h, w_ref[:, D_INNER:])
    dt_ref[...] = _dot(h, wdt_ref[...])


def _ssm_in(x, nw, w, wdt, act_dtype, tm):
    n = x.shape[0]
    return pl.pallas_call(
        _ssm_in_body,
        grid=(n // tm,),
        in_specs=[_row_spec(D_MODEL, tm), _const_spec(nw.shape), _const_spec(w.shape), _const_spec(wdt.shape)],
        out_specs=[_row_spec(D_INNER, tm), _row_spec(CONV_DIM, tm), _row_spec(LANES, tm)],
        out_shape=[jax.ShapeDtypeStruct((n, D_INNER), act_dtype),
                   jax.ShapeDtypeStruct((n, CONV_DIM), F32),
                   jax.ShapeDtypeStruct((n, LANES), F32)],
        compiler_params=_params(("parallel",)),
        name="ssm_in",
    )(x, nw, w, wdt)


def _lane_bcast(x, h):
    return jnp.broadcast_to(x[:, h:h + 1], (x.shape[0], LANES))


def _row_bcast(x, h):
    return jnp.broadcast_to(x[h:h + 1, :], (LANES, x.shape[1]))


def _pair_blockdiag(xp):
    lane = lax.broadcasted_iota(jnp.int32, xp.shape, 1)
    xb = xp.astype(BF16)
    zero = jnp.zeros_like(xb)
    return jnp.concatenate([jnp.where(lane < SSM_HEAD_DIM, xb, zero),
                            jnp.where(lane >= SSM_HEAD_DIM, xb, zero)], axis=0)


def _pair_select(a0, a1):
    lane = lax.broadcasted_iota(jnp.int32, a0.shape, 1)
    return jnp.where(lane < SSM_HEAD_DIM, a0, a1)


def _intra_pair(cb, cmask, a2, a2_t, dt_t, pair):
    ms = []
    for hh in range(2):
        h = 2 * pair + hh
        seg = _lane_bcast(a2, h) - _row_bcast(a2_t, h)
        dec = jnp.exp2(jnp.where(cmask, seg, NEG))
        ms.append((cb * dec * _row_bcast(dt_t, h)).astype(BF16))
    return jnp.concatenate(ms, axis=1)


def _pair_out_scale(e_all, pair):
    return _pair_select(_lane_bcast(e_all, 2 * pair), _lane_bcast(e_all, 2 * pair + 1))


def _ssd_chunk_body(is_meta, xbc_ref, dt_ref, mtile_ref, init_ref, cw_ref, cbias_ref, dtb_ref, alog_ref,
                    dsk_ref, e_ref, y_ref, st_out_ref, halo_ref, act_ref, st_ref):
    j = pl.program_id(1)
    n_chunks = pl.num_programs(1)
    q = BLOCK
    halo = CONV_HALO

    @pl.when(j == 0)
    def _():
        if is_meta:
            halo_ref[...] = jnp.zeros((halo, CONV_DIM), F32)
            st_ref[...] = jnp.zeros(st_ref.shape, F32)
        else:
            halo_ref[...] = mtile_ref[0:halo, :]
            for pair in range(SSM_HEADS // 2):
                ps = slice(pair * LANES, (pair + 1) * LANES)
                st_ref[:, ps] = init_ref[0, ps, :].T

    n_sh = CONV_W - 1
    r = lax.broadcasted_iota(jnp.int32, (q, n_sh * q), 0)
    c = lax.broadcasted_iota(jnp.int32, (q, n_sh * q), 1)
    sel = (c % q == r - (c // q + 1)).astype(BF16)
    top = 8
    rh = lax.broadcasted_iota(jnp.int32, (top, n_sh * halo), 0)
    ch = lax.broadcasted_iota(jnp.int32, (top, n_sh * halo), 1)
    sel_halo = (ch % halo == halo + rh - (ch // halo + 1)).astype(BF16)

    def taps(x, cs):
        return jnp.concatenate([(cw_ref[CONV_W - 1 - k:CONV_W - k, cs] * x).astype(BF16)
                                for k in range(1, CONV_W)], axis=0)

    width = 512
    for cc in range(CONV_DIM // width):
        cs = slice(cc * width, (cc + 1) * width)
        xc = xbc_ref[:, cs]
        conv = cbias_ref[:, cs] + cw_ref[CONV_W - 1:CONV_W, cs] * xc + _dot(sel, taps(xc, cs))
        act_ref[:, cs] = _silu(conv)
        conv_top = conv[0:top, :] + _dot(sel_halo, taps(halo_ref[:, cs], cs))
        act_ref[0:top, cs] = _silu(conv_top)
    halo_ref[...] = xbc_ref[q - halo:, :]

    ri = lax.broadcasted_iota(jnp.int32, (q, q), 0)
    ci = lax.broadcasted_iota(jnp.int32, (q, q), 1)
    dt_ok = ci < SSM_HEADS
    if is_meta:
        dt_ok = dt_ok & (ri < N_META)
    dt = jnp.where(dt_ok, _softplus(dt_ref[...] + dtb_ref[...]), 0.0)
    a = dt * (-jnp.exp(alog_ref[...]))
    cmask = ri >= ci
    a2 = _exact_left(cmask.astype(BF16), a) * LOG2E
    a2_t = a2.T
    dt_t = dt.T
    e_all = jnp.exp2(a2)
    w_t = dt_t * jnp.exp2(jnp.broadcast_to(a2_t[:, q - 1:q], (q, q)) - a2_t)
    dec_rows = _exact_right(jnp.broadcast_to(e_all[q - 1:q, :], (8, LANES)), e_ref[...])[0:1, :]
    rowi = lax.broadcasted_iota(jnp.int32, (q, LANES), 0)

    for g in range(SSM_GROUPS):
        gs = slice(D_INNER + g * D_STATE, D_INNER + (g + 1) * D_STATE)
        bg = act_ref[:, gs]
        cg = act_ref[:, SSM_GROUPS * D_STATE + gs.start:SSM_GROUPS * D_STATE + gs.stop].astype(BF16)
        cb = _dot_nt(cg, bg.astype(BF16))
        bg_t = bg.T
        y_off = _dot(cg, st_ref[:, g * GROUP_WIDTH:(g + 1) * GROUP_WIDTH].astype(BF16))
        for pr in range(PAIRS_PER_GROUP):
            pair = g * PAIRS_PER_GROUP + pr
            ps = slice(pair * LANES, (pair + 1) * LANES)
            xp = act_ref[:, ps]
            if is_meta:
                xp = jnp.where(rowi < N_META, xp, 0.0)
            xbd = _pair_blockdiag(xp)
            y = _dot(_intra_pair(cb, cmask, a2, a2_t, dt_t, pair), xbd)
            st = st_ref[:, ps]
            y = y + y_off[:, pr * LANES:(pr + 1) * LANES] * _pair_out_scale(e_all, pair)
            wn = jnp.concatenate([(bg_t * _row_bcast(w_t, 2 * pair)).astype(BF16),
                                  (bg_t * _row_bcast(w_t, 2 * pair + 1)).astype(BF16)], axis=1)
            st_ref[:, ps] = st * dec_rows[:, ps] + _dot(wn, xbd)
            y_ref[:, ps] = (y + dsk_ref[:, ps] * xp).astype(y_ref.dtype)

    @pl.when(j == n_chunks - 1)
    def _():
        for pair in range(SSM_HEADS // 2):
            ps = slice(pair * LANES, (pair + 1) * LANES)
            st_out_ref[0, ps, :] = st_ref[:, ps].T


def _ssd_chunks(xbc, dt, xbc_small, init, cw, cbias, dtb, alog, dsk, emat, n_batch, n_chunks, first_block,
                meta_block, is_meta, out_dtype):
    cur = lambda b, j: (first_block + b * n_chunks + j, 0)
    out = lambda b, j: (b * n_chunks + j, 0)
    return pl.pallas_call(
        functools.partial(_ssd_chunk_body, is_meta),
        grid=(n_batch, n_chunks),
        in_specs=[pl.BlockSpec((BLOCK, CONV_DIM), cur), pl.BlockSpec((BLOCK, LANES), cur),
                  pl.BlockSpec((BLOCK, CONV_DIM), lambda b, j: (meta_block, 0)), _const_spec(init.shape),
                  _const_spec(cw.shape), _const_spec(cbias.shape), _const_spec(dtb.shape),
                  _const_spec(alog.shape), _const_spec(dsk.shape), _const_spec(emat.shape)],
        out_specs=[pl.BlockSpec((BLOCK, D_INNER), out),
                   pl.BlockSpec((1, D_INNER, D_STATE), lambda b, j: (b, 0, 0))],
        out_shape=[jax.ShapeDtypeStruct((n_batch * n_chunks * BLOCK, D_INNER), out_dtype),
                   jax.ShapeDtypeStruct((n_batch, D_INNER, D_STATE), F32)],
        scratch_shapes=[pltpu.VMEM((CONV_HALO, CONV_DIM), F32), pltpu.VMEM((BLOCK, CONV_DIM), F32),
                        pltpu.VMEM((D_STATE, D_INNER), F32)],
        compiler_params=_params(("parallel", "arbitrary")),
        name="ssd_meta" if is_meta else "ssd_prompt",
    )(xbc, dt, xbc_small, init, cw, cbias, dtb, alog, dsk, emat)


def _ssd_sample_body(*refs):
    nst_ref = refs[-2]

    @pl.when(pl.program_id(1) == 0)
    def _():
        _ssd_sample_step(*refs)

    @pl.when(pl.program_id(1) > 0)
    def _():
        nst_ref[...] = jnp.zeros(nst_ref.shape, F32)


def _ssd_sample_step(xbc_ref, dt_ref, cs_ref, st_ref, cw_ref, cbias_ref, dtb_ref, alog_ref, dsk_ref, e_ref,
                     *rest):
    y_ref, nst_ref, yt_ref = rest[-3:]
    n_seq = st_ref.shape[0]
    n_new = xbc_ref.shape[0] // n_seq
    n_cs = CONV_W - 1
    qr = n_seq * n_new
    q = LANES
    kpad = 64
    assert qr + n_seq * n_cs <= kpad and n_new >= n_cs

    x_new = xbc_ref[...]
    xc = jnp.concatenate([x_new, cs_ref[...], jnp.zeros((kpad - qr - n_seq * n_cs, CONV_DIM), F32)], axis=0)
    r = lax.broadcasted_iota(jnp.int32, (qr, kpad), 0)
    c = lax.broadcasted_iota(jnp.int32, (qr, kpad), 1)
    s_of_r = r // n_new
    t_of_r = r % n_new
    conv = cbias_ref[...] + cw_ref[CONV_W - 1:CONV_W, :] * x_new
    for k in range(1, CONV_W):
        target = jnp.where(t_of_r >= k, r - k, qr + n_cs * s_of_r + t_of_r + n_cs - k)
        conv = conv + cw_ref[CONV_W - 1 - k:CONV_W - k, :] * _exact_left((c == target).astype(BF16), xc)
    act = jnp.concatenate([_silu(conv), jnp.zeros((q - qr, CONV_DIM), F32)], axis=0)
    xh = act[:, :D_INNER]
    bm = act[:, D_INNER:D_INNER + SSM_GROUPS * D_STATE]
    cm = act[:, D_INNER + SSM_GROUPS * D_STATE:]
    ri = lax.broadcasted_iota(jnp.int32, (q, q), 0)
    ci = lax.broadcasted_iota(jnp.int32, (q, q), 1)
    dt = jnp.concatenate([_softplus(dt_ref[...] + dtb_ref[...]), jnp.zeros((q - qr, LANES), F32)], axis=0)
    dt = jnp.where(ci < SSM_HEADS, dt, 0.0)
    a = dt * (-jnp.exp(alog_ref[...]))
    cmask = (ri >= ci) & (ri // n_new == ci // n_new) & (ri < qr)
    a_cum = _exact_left(cmask.astype(BF16), a)
    a2 = a_cum * LOG2E
    a2_t = a2.T
    dt_t = dt.T
    e_all = jnp.exp2(a2)
    last = ((ci == (ri // n_new) * n_new + n_new - 1) & (ri < qr)).astype(BF16)
    a2_last = _exact_left(last, a_cum) * LOG2E
    w_exp = _exact_right(dt * jnp.exp2(a2_last - a2), e_ref[...])
    dec_exp = _exact_right(jnp.exp2(a2_last), e_ref[...])
    xw = xh * w_exp

    yt_ref[...] = jnp.zeros(yt_ref.shape, F32)
    colseq = ci // n_new
    rowseq = ri // n_new
    for g in range(SSM_GROUPS):
        gs = slice(g * D_STATE, (g + 1) * D_STATE)
        hs = slice(g * GROUP_WIDTH, (g + 1) * GROUP_WIDTH)
        cg_t = cm[:, gs].T.astype(BF16)
        bg = bm[:, gs].astype(BF16)
        xw_t = jnp.concatenate(
            [xw[:, g * GROUP_WIDTH + i * LANES:g * GROUP_WIDTH + (i + 1) * LANES].T
             for i in range(GROUP_WIDTH // LANES)], axis=0).astype(BF16)
        dec_t = jnp.concatenate(
            [dec_exp[:, g * GROUP_WIDTH + i * LANES:g * GROUP_WIDTH + (i + 1) * LANES].T
             for i in range(GROUP_WIDTH // LANES)], axis=0)
        zero = jnp.zeros((q, q), BF16)
        for s in range(n_seq):
            st0 = st_ref[s, hs, :]
            yt_ref[hs, :] += _dot(st0.astype(BF16), jnp.where(colseq == s, cg_t, zero))
            inc = _dot(xw_t, jnp.where(rowseq == s, bg, zero))
            dcol = jnp.broadcast_to(dec_t[:, s * n_new:s * n_new + 1], (GROUP_WIDTH, q))
            nst_ref[s, hs, :] = st0 * dcol + inc

    for g in range(SSM_GROUPS):
        gs = slice(g * D_STATE, (g + 1) * D_STATE)
        cb = _dot_nt(cm[:, gs].astype(BF16), bm[:, gs].astype(BF16))
        for pr in range(PAIRS_PER_GROUP):
            pair = g * PAIRS_PER_GROUP + pr
            ps = slice(pair * LANES, (pair + 1) * LANES)
            xp = xh[:, ps]
            y = _dot(_intra_pair(cb, cmask, a2, a2_t, dt_t, pair), _pair_blockdiag(xp))
            y = y + yt_ref[ps, :].T * _pair_out_scale(e_all, pair) + dsk_ref[:, ps] * xp
            y_ref[:, ps] = y[0:qr, :]


def _ssd_sample(xbc, dt, cs, st, cw, cbias, dtb, alog, dsk, emat, n_new, layer, nst_prev=None):
    n_layers, n_seq = st.shape[0], st.shape[1]
    sb = SAMPLE_SEQS
    n_pass = n_layers if nst_prev is None else 1
    rows = lambda cols, per: pl.BlockSpec((sb * per, cols), lambda i, t: (i, 0))
    st_spec = pl.BlockSpec((None, sb, D_INNER, D_STATE), lambda i, t: (layer, i, 0, 0))
    nst_spec = pl.BlockSpec((None, sb, D_INNER, D_STATE), lambda i, t: ((layer + t) % n_layers, i, 0, 0))
    in_specs = [rows(CONV_DIM, n_new), rows(LANES, n_new), rows(CONV_DIM, CONV_W - 1), st_spec,
                _const_spec(cw.shape), _const_spec(cbias.shape), _const_spec(dtb.shape),
                _const_spec(alog.shape), _const_spec(dsk.shape), _const_spec(emat.shape)]
    args = [xbc, dt, cs, st, cw, cbias, dtb, alog, dsk, emat]
    aliases = {}
    if nst_prev is not None:
        in_specs.append(pl.BlockSpec(memory_space=pl.ANY))
        aliases = {len(args): 1}
        args.append(nst_prev)
    return pl.pallas_call(
        _ssd_sample_body,
        grid=(n_seq // sb, n_pass),
        in_specs=in_specs,
        out_specs=[rows(D_INNER, n_new), nst_spec],
        out_shape=[jax.ShapeDtypeStruct((n_seq * n_new, D_INNER), F32),
                   jax.ShapeDtypeStruct(st.shape, F32)],
        input_output_aliases=aliases,
        scratch_shapes=[pltpu.VMEM((D_INNER, LANES), F32)],
        compiler_params=_params(("parallel", "arbitrary")),
        name="ssd_sample",
    )(*args)


def _ssm_out_body(x_ref, y_ref, z_ref, nw_ref, w_ref, o_ref):
    y = y_ref[...].astype(F32) * _silu(z_ref[...].astype(F32))
    parts = []
    for g in range(SSM_GROUPS):
        yg = y[:, g * GROUP_WIDTH:(g + 1) * GROUP_WIDTH]
        parts.append(yg * lax.rsqrt(jnp.mean(yg * yg, axis=-1, keepdims=True) + EPS))
    yn = (jnp.concatenate(parts, axis=1) * nw_ref[...]).astype(BF16)
    o_ref[...] = x_ref[...] + _dot(yn, w_ref[...])


def _ssm_out(x, y, z, nw, w, tm):
    n = x.shape[0]
    return pl.pallas_call(
        _ssm_out_body,
        grid=(n // tm,),
        in_specs=[_row_spec(D_MODEL, tm), _row_spec(D_INNER, tm), _row_spec(D_INNER, tm),
                  _const_spec(nw.shape), _const_spec(w.shape)],
        out_specs=_row_spec(D_MODEL, tm),
        out_shape=jax.ShapeDtypeStruct(x.shape, F32),
        compiler_params=_params(("parallel",)),
        name="ssm_out",
    )(x, y, z, nw, w)


def _pad_lanes(v):
    return jnp.pad(v.astype(F32), (0, LANES - v.shape[0]))[None, :]


def kernel(x_prompt, x_sample, cache_k, cache_v, state_conv, state_ssm, meta_tokens, norm_w,
           w_attn_in, q_norm_w, k_norm_w, attn_sinks, w_attn_out, w_ssm_in, conv_w, conv_b,
           dt_bias, a_log, d_skip, ssm_norm_w, w_ssm_out):
    n_batch, seq, _ = x_prompt.shape
    n_seq, n_new, _ = x_sample.shape
    wbuf = cache_k.shape[2]
    n_blocks = seq // BLOCK
    n_rows = n_seq * n_new
    assert seq % BLOCK == 0 and wbuf == WINDOW and n_rows % BLOCK == 0 and n_new == 4
    meta_block = n_rows // BLOCK
    small_rows = n_rows + BLOCK

    xp = x_prompt.reshape(n_batch * seq, D_MODEL)
    xs = jnp.concatenate([x_sample.reshape(n_rows, D_MODEL), meta_tokens.astype(F32),
                          jnp.zeros((BLOCK - N_META, D_MODEL), F32)], axis=0)

    emat = jnp.pad(jnp.repeat(jnp.eye(SSM_HEADS, dtype=BF16), SSM_HEAD_DIM, axis=1),
                   ((0, LANES - SSM_HEADS), (0, 0)))
    zero_state = jnp.zeros((1, D_INNER, D_STATE), F32)

    n_attn, n_ssm = cache_k.shape[0], state_ssm.shape[0]
    ck_all = cache_k.reshape(n_attn, n_seq, wbuf * N_KV_HEADS, HEAD_DIM)
    cv_all = cache_v.reshape(n_attn, n_seq, wbuf * N_KV_HEADS, HEAD_DIM)
    st_all = state_ssm.reshape(n_ssm, n_seq, D_INNER, D_STATE)
    new_ck = new_cv = new_st = None

    kp_l, vp_l, cp_l, sp_l, cs_l = [], [], [], [], []
    for i in range(DEPTH):
        l = i // 2
        nw = norm_w[i][None, :]
        if i % 2 == 0:
            w_in = w_attn_in[l].astype(BF16)
            w_out = w_attn_out[l].astype(BF16)
            qn, kn = q_norm_w[l][None, :], k_norm_w[l][None, :]
            sinks = attn_sinks[l].astype(F32)

            qs, ks, vs, gs = _attn_in(xs, nw, w_in, qn, kn, F32, small_rows)
            o_meta = _attn_blocks(qs, ks, vs, ks, vs, sinks, 1, 1, meta_block, meta_block, True, F32)
            qq = qs[:n_rows].reshape(n_seq, n_new, N_KV_HEADS, GQA_GROUP * HEAD_DIM).transpose(0, 2, 1, 3)
            o, new_ck, new_cv = _attn_sample(qq.reshape(n_seq, N_HEADS * n_new, HEAD_DIM),
                                             ks[:n_rows].reshape(n_seq, n_new * N_KV_HEADS, HEAD_DIM),
                                             vs[:n_rows].reshape(n_seq, n_new * N_KV_HEADS, HEAD_DIM),
                                             ck_all, cv_all, sinks, l, new_ck, new_cv)
            o = o.reshape(n_seq, N_KV_HEADS, n_new, GQA_GROUP * HEAD_DIM).transpose(0, 2, 1, 3)
            o = jnp.concatenate([o.reshape(n_rows, ATTN_WIDTH), o_meta], axis=0)

            q, k, v, g = _attn_in(xp, nw, w_in, qn, kn, BF16, ROW_TILE)
            op = _attn_blocks(q, k, v, ks, vs, sinks, n_batch, n_blocks, 0, meta_block, False, BF16)
            xp = _attn_out(xp, op, g, w_out, ROW_TILE)
            xs = _attn_out(xs, o, gs, w_out, small_rows)
            kp_l.append(k.reshape(n_batch, seq, KV_WIDTH)[:, -WINDOW:].reshape(n_batch, WINDOW, N_KV_HEADS, HEAD_DIM))
            vp_l.append(v.reshape(n_batch, seq, KV_WIDTH)[:, -WINDOW:].reshape(n_batch, WINDOW, N_KV_HEADS, HEAD_DIM))
        else:
            w_in = w_ssm_in[l][:, :D_INNER + CONV_DIM].astype(BF16)
            w_dt = jnp.pad(w_ssm_in[l][:, D_INNER + CONV_DIM:], ((0, 0), (0, LANES - SSM_HEADS))).astype(BF16)
            w_out = w_ssm_out[l].astype(BF16)
            cw, cbias = conv_w[l].astype(F32), conv_b[l][None, :].astype(F32)
            dtb, alog = _pad_lanes(dt_bias[l]), _pad_lanes(a_log[l])
            dsk = jnp.repeat(d_skip[l].astype(F32), SSM_HEAD_DIM)[None, :]
            snw = ssm_norm_w[l][None, :].astype(F32)
            consts = (cw, cbias, dtb, alog, dsk, emat)

            zs, xbcs, dts = _ssm_in(xs, nw, w_in, w_dt, F32, small_rows)
            y_meta, st_meta = _ssd_chunks(xbcs, dts, xbcs, zero_state, *consts, 1, 1, meta_block, meta_block,
                                          True, F32)
            y, new_st = _ssd_sample(xbcs, dts, state_conv[l].reshape(n_seq * (CONV_W - 1), CONV_DIM), st_all,
                                    *consts, n_new, l, new_st)
            y = jnp.concatenate([y, y_meta], axis=0)

            z, xbc, dt = _ssm_in(xp, nw, w_in, w_dt, BF16, ROW_TILE)
            yp, st = _ssd_chunks(xbc, dt, xbcs, st_meta, *consts, n_batch, n_blocks, 0, meta_block, False, BF16)
            xp = _ssm_out(xp, yp, z, snw, w_out, ROW_TILE)
            xs = _ssm_out(xs, y, zs, snw, w_out, small_rows)
            cp_l.append(xbc.reshape(n_batch, seq, CONV_DIM)[:, -(CONV_W - 1):])
            sp_l.append(st.reshape(n_batch, SSM_HEADS, SSM_HEAD_DIM, D_STATE))
            cs_l.append(xbcs[:n_rows].reshape(n_seq, n_new, CONV_DIM)[:, -(CONV_W - 1):])

    y_prompt = xp.reshape(n_batch, seq, D_MODEL)
    y_sample = xs[:n_rows].reshape(n_seq, n_new, D_MODEL)
    return (y_prompt, y_sample,
            jnp.stack(kp_l), jnp.stack(vp_l), jnp.stack(cp_l), jnp.stack(sp_l),
            new_ck.reshape(cache_k.shape), new_cv.reshape(cache_v.shape), jnp.stack(cs_l),
            new_st.reshape(state_ssm.shape))
```

```python
import functools
import math

import jax
import jax.numpy as jnp
from jax import lax
from jax.experimental import pallas as pl
from jax.experimental.pallas import tpu as pltpu

F32 = jnp.float32
BF16 = jnp.bfloat16

D_MODEL = 1024
DEPTH = 4
N_META = 16
WINDOW = 128
BLOCK = 128
HEAD_DIM = 128
ATTN_WIDTH = 2 * D_MODEL
N_HEADS = ATTN_WIDTH // HEAD_DIM
N_KV_HEADS = 4
GQA_GROUP = N_HEADS // N_KV_HEADS
KV_WIDTH = N_KV_HEADS * HEAD_DIM
D_INNER = 2 * D_MODEL
SSM_HEAD_DIM = 64
SSM_HEADS = D_INNER // SSM_HEAD_DIM
D_STATE = 128
SSM_GROUPS = 4
GROUP_WIDTH = D_INNER // SSM_GROUPS
PAIRS_PER_GROUP = SSM_HEADS // SSM_GROUPS // 2
CONV_W = 4
CONV_DIM = D_INNER + 2 * SSM_GROUPS * D_STATE
EPS = 1e-6
NEG = -1e30
LOG2E = math.log2(math.e)
QK_SCALE_LOG2 = HEAD_DIM ** -0.5 * LOG2E

LANES = 128
ROW_TILE = 512
SAMPLE_SEQS = 8
ATTN_SUB_BLOCKS = 4
SSD_SUB_CHUNKS = 2
CONV_HALO = 16
V7X_VMEM_BYTES = 64 * 1024 * 1024
VMEM_LIMIT = V7X_VMEM_BYTES * 7 // 8


def _dot(a, b):
    return jnp.dot(a, b, preferred_element_type=F32)


def _dot_nt(a, b):
    return lax.dot_general(a, b, (((1,), (1,)), ((), ())), preferred_element_type=F32)


def _rmsnorm(x, w):
    return x * lax.rsqrt(jnp.mean(x * x, axis=-1, keepdims=True) + EPS) * w


def _silu(x):
    return x * jax.nn.sigmoid(x)


def _softplus(x):
    return jnp.maximum(x, 0.0) + jnp.log1p(jnp.exp(-jnp.abs(x)))


def _split3(x):
    hi = x.astype(BF16)
    r = x - hi.astype(F32)
    mid = r.astype(BF16)
    lo = (r - mid.astype(F32)).astype(BF16)
    return hi, mid, lo


def _exact_left(p, x):
    hi, mid, lo = _split3(x)
    return _dot(p, hi) + _dot(p, mid) + _dot(p, lo)


def _exact_right(x, e):
    hi, mid, lo = _split3(x)
    return _dot(hi, e) + _dot(mid, e) + _dot(lo, e)


def _params(semantics):
    return pltpu.CompilerParams(dimension_semantics=semantics, vmem_limit_bytes=VMEM_LIMIT)


def _const_spec(shape):
    nd = len(shape)
    return pl.BlockSpec(shape, lambda *_: (0,) * nd)


def _row_spec(cols, tm):
    return pl.BlockSpec((tm, cols), lambda i: (i, 0))


def _layer_spec(w, layer):
    return pl.BlockSpec((None,) + w.shape[1:], lambda *_: (layer, 0, 0))


def _attn_in_body(x_ref, nw_ref, w_ref, qn_ref, kn_ref, q_ref, k_ref, v_ref, g_ref):
    h = _rmsnorm(x_ref[...], nw_ref[...]).astype(BF16)
    q = _dot(h, w_ref[:, :ATTN_WIDTH])
    for hd in range(N_HEADS):
        sl = slice(hd * HEAD_DIM, (hd + 1) * HEAD_DIM)
        q_ref[:, sl] = (_rmsnorm(q[:, sl], qn_ref[...]) * QK_SCALE_LOG2).astype(q_ref.dtype)
    k = _dot(h, w_ref[:, ATTN_WIDTH:ATTN_WIDTH + KV_WIDTH])
    for hd in range(N_KV_HEADS):
        sl = slice(hd * HEAD_DIM, (hd + 1) * HEAD_DIM)
        k_ref[:, sl] = _rmsnorm(k[:, sl], kn_ref[...])
    v_ref[...] = _dot(h, w_ref[:, ATTN_WIDTH + KV_WIDTH:ATTN_WIDTH + 2 * KV_WIDTH])
    g_ref[...] = _dot(h, w_ref[:, ATTN_WIDTH + 2 * KV_WIDTH:]).astype(g_ref.dtype)


def _attn_in(x, nw, w, layer, qn, kn, act_dtype, tm):
    n = x.shape[0]
    return pl.pallas_call(
        _attn_in_body,
        grid=(n // tm,),
        in_specs=[_row_spec(D_MODEL, tm), _const_spec(nw.shape), _layer_spec(w, layer),
                  _const_spec(qn.shape), _const_spec(kn.shape)],
        out_specs=[_row_spec(ATTN_WIDTH, tm), _row_spec(KV_WIDTH, tm), _row_spec(KV_WIDTH, tm),
                   _row_spec(ATTN_WIDTH, tm)],
        out_shape=[jax.ShapeDtypeStruct((n, ATTN_WIDTH), act_dtype),
                   jax.ShapeDtypeStruct((n, KV_WIDTH), F32),
                   jax.ShapeDtypeStruct((n, KV_WIDTH), F32),
                   jax.ShapeDtypeStruct((n, ATTN_WIDTH), act_dtype)],
        compiler_params=_params(("parallel",)),
        name="attn_in",
    )(x, nw, w, qn, kn)


def _attn_block_body(is_meta, n_sub, sink_ref, q_ref, kc_ref, kp_ref, vc_ref, vp_ref, km_ref, vm_ref, o_ref):
    j = pl.program_id(1)
    rows = GQA_GROUP * BLOCK
    rowg = lax.broadcasted_iota(jnp.int32, (rows, 1), 0) // BLOCK
    row = lax.broadcasted_iota(jnp.int32, (rows, 2 * BLOCK), 0) & (BLOCK - 1)
    col = lax.broadcasted_iota(jnp.int32, (rows, 2 * BLOCK), 1)
    cur_ok = (col >= BLOCK) & (col - BLOCK <= row)

    def block(r0, kprev, vprev, n_prev):
        mask = cur_ok
        if n_prev:
            mask = mask | ((col < n_prev) & (col > row - (BLOCK - n_prev)))
        for kv in range(N_KV_HEADS):
            sl = slice(kv * HEAD_DIM, (kv + 1) * HEAD_DIM)
            keys = jnp.concatenate([kprev(sl), kc_ref[r0:r0 + BLOCK, sl]], axis=0).astype(BF16)
            vals = jnp.concatenate([vprev(sl), vc_ref[r0:r0 + BLOCK, sl]], axis=0).astype(BF16)
            base = kv * GQA_GROUP * HEAD_DIM
            q4 = jnp.concatenate(
                [q_ref[r0:r0 + BLOCK, base + g * HEAD_DIM:base + (g + 1) * HEAD_DIM] for g in range(GQA_GROUP)],
                axis=0).astype(BF16)
            sink = jnp.zeros((rows, 1), F32)
            for g in range(GQA_GROUP):
                sink = jnp.where(rowg == g, sink_ref[kv * GQA_GROUP + g] * LOG2E, sink)
            s = jnp.where(mask, _dot_nt(q4, keys), NEG)
            m = jnp.maximum(jnp.max(s, axis=-1, keepdims=True), sink)
            p = jnp.exp2(s - m)
            denom = jnp.sum(p, axis=-1, keepdims=True) + jnp.exp2(sink - m)
            o = _dot(p.astype(BF16), vals) / denom
            for g in range(GQA_GROUP):
                o_ref[r0:r0 + BLOCK, base + g * HEAD_DIM:base + (g + 1) * HEAD_DIM] = (
                    o[g * BLOCK:(g + 1) * BLOCK].astype(o_ref.dtype))

    if is_meta:
        block(0, lambda sl: km_ref[:, sl], lambda sl: vm_ref[:, sl], 0)
    else:
        @pl.when(j == 0)
        def _():
            block(0, lambda sl: km_ref[:, sl], lambda sl: vm_ref[:, sl], N_META)

        @pl.when(j > 0)
        def _():
            block(0, lambda sl: kp_ref[:, sl], lambda sl: vp_ref[:, sl], BLOCK)

    for sub in range(1, n_sub):
        p0 = (sub - 1) * BLOCK
        block(sub * BLOCK, lambda sl, p0=p0: kc_ref[p0:p0 + BLOCK, sl], lambda sl, p0=p0: vc_ref[p0:p0 + BLOCK, sl],
              BLOCK)


def _attn_blocks(q, k, v, k_small, v_small, sinks, n_batch, n_blocks, n_sub, first_block, meta_block, is_meta,
                 out_dtype):
    assert n_blocks % n_sub == 0 and first_block % n_sub == 0
    n_steps = n_blocks // n_sub
    rows = n_sub * BLOCK
    cur = lambda b, j: (first_block // n_sub + b * n_steps + j, 0)
    prev = lambda b, j: (jnp.maximum(first_block + (b * n_steps + j) * n_sub - 1, 0), 0)
    meta = lambda b, j: (meta_block, 0)
    out = lambda b, j: (b * n_steps + j, 0)
    return pl.pallas_call(
        functools.partial(_attn_block_body, is_meta, n_sub),
        grid=(n_batch, n_steps),
        in_specs=[pl.BlockSpec(memory_space=pltpu.SMEM),
                  pl.BlockSpec((rows, ATTN_WIDTH), cur),
                  pl.BlockSpec((rows, KV_WIDTH), cur), pl.BlockSpec((BLOCK, KV_WIDTH), prev),
                  pl.BlockSpec((rows, KV_WIDTH), cur), pl.BlockSpec((BLOCK, KV_WIDTH), prev),
                  pl.BlockSpec((BLOCK, KV_WIDTH), meta), pl.BlockSpec((BLOCK, KV_WIDTH), meta)],
        out_specs=pl.BlockSpec((rows, ATTN_WIDTH), out),
        out_shape=jax.ShapeDtypeStruct((n_batch * n_blocks * BLOCK, ATTN_WIDTH), out_dtype),
        compiler_params=_params(("parallel", "parallel")),
        name="attn_meta" if is_meta else "attn_prompt",
    )(sinks, q, k, k, v, v, k_small, v_small)


def _attn_sample_body(fill_layers, sink_ref, q_ref, k_ref, v_ref, ck_ref, cv_ref, *rest):
    o_ref, nk_ref, nv_ref = rest[-3:]
    if fill_layers:
        layer, n_layers = fill_layers
        for other in range(n_layers):
            if other != layer:
                nk_ref[other] = jnp.zeros(nk_ref.shape[1:], F32)
                nv_ref[other] = jnp.zeros(nv_ref.shape[1:], F32)
        nk_ref, nv_ref = nk_ref.at[layer], nv_ref.at[layer]
    n_seq, rows, _ = q_ref.shape
    n_cache = ck_ref.shape[1]
    n_fresh = k_ref.shape[1]
    wbuf = n_cache // N_KV_HEADS
    per_kv = rows // N_KV_HEADS
    row = lax.broadcasted_iota(jnp.int32, (rows, n_cache + n_fresh), 0)
    col = lax.broadcasted_iota(jnp.int32, (rows, n_cache + n_fresh), 1)
    qpos = (row % per_kv) // GQA_GROUP
    kpos = jnp.where(col < n_cache, col // N_KV_HEADS - wbuf, (col - n_cache) // N_KV_HEADS)
    diff = qpos - kpos
    mask = (col % N_KV_HEADS == row // per_kv) & (diff >= 0) & (diff < WINDOW)
    row1 = lax.broadcasted_iota(jnp.int32, (rows, 1), 0)
    sink = jnp.zeros((rows, 1), F32)
    for hd in range(N_HEADS):
        sink = jnp.where((row1 // per_kv == hd // GQA_GROUP) & (row1 % GQA_GROUP == hd % GQA_GROUP),
                         sink_ref[hd] * LOG2E, sink)
    for s in range(n_seq):
        keys = jnp.concatenate([ck_ref[s], k_ref[s]], axis=0).astype(BF16)
        vals = jnp.concatenate([cv_ref[s], v_ref[s]], axis=0).astype(BF16)
        sc = _dot_nt(q_ref[s].astype(BF16), keys)
        sc = jnp.where(mask, sc, NEG)
        m = jnp.maximum(jnp.max(sc, axis=-1, keepdims=True), sink)
        p = jnp.exp2(sc - m)
        denom = jnp.sum(p, axis=-1, keepdims=True) + jnp.exp2(sink - m)
        o_ref[s] = _dot(p.astype(BF16), vals) / denom
        nk_ref[s, 0:n_cache - n_fresh, :] = ck_ref[s, n_fresh:, :]
        nk_ref[s, n_cache - n_fresh:, :] = k_ref[s]
        nv_ref[s, 0:n_cache - n_fresh, :] = cv_ref[s, n_fresh:, :]
        nv_ref[s, n_cache - n_fresh:, :] = v_ref[s]


def _attn_sample(q, k, v, ck, cv, sinks, layer, nk_prev=None, nv_prev=None):
    sb = SAMPLE_SEQS
    n_seq = q.shape[0]
    n_layers = ck.shape[0]
    blk = lambda a: pl.BlockSpec((sb,) + a.shape[1:], lambda i: (i,) + (0,) * (a.ndim - 1))
    cache_spec = pl.BlockSpec((None, sb) + ck.shape[2:], lambda i: (layer, i, 0, 0))
    in_specs = [pl.BlockSpec(memory_space=pltpu.SMEM), blk(q), blk(k), blk(v), cache_spec, cache_spec]
    args = [sinks, q, k, v, ck, cv]
    if nk_prev is None:
        aliases, fill_layers = {}, (layer, n_layers)
        new_spec = pl.BlockSpec((n_layers, sb) + ck.shape[2:], lambda i: (0, i, 0, 0))
    else:
        in_specs += [pl.BlockSpec(memory_space=pl.ANY)] * 2
        aliases, fill_layers = {len(args): 1, len(args) + 1: 2}, None
        new_spec = cache_spec
        args += [nk_prev, nv_prev]
    return pl.pallas_call(
        functools.partial(_attn_sample_body, fill_layers),
        grid=(n_seq // sb,),
        in_specs=in_specs,
        out_specs=[blk(q), new_spec, new_spec],
        out_shape=[jax.ShapeDtypeStruct(q.shape, F32),
                   jax.ShapeDtypeStruct(ck.shape, F32),
                   jax.ShapeDtypeStruct(cv.shape, F32)],
        input_output_aliases=aliases,
        compiler_params=_params(("parallel",)),
        name="attn_sample",
    )(*args)


def _attn_out_body(x_ref, o_ref, g_ref, w_ref, y_ref):
    a = o_ref[...].astype(F32) * _silu(g_ref[...].astype(F32))
    y_ref[...] = x_ref[...] + _dot(a.astype(BF16), w_ref[...])


def _attn_out(x, o, g, w, layer, tm):
    n = x.shape[0]
    return pl.pallas_call(
        _attn_out_body,
        grid=(n // tm,),
        in_specs=[_row_spec(D_MODEL, tm), _row_spec(ATTN_WIDTH, tm), _row_spec(ATTN_WIDTH, tm),
                  _layer_spec(w, layer)],
        out_specs=_row_spec(D_MODEL, tm),
        out_shape=jax.ShapeDtypeStruct(x.shape, F32),
        compiler_params=_params(("parallel",)),
        name="attn_out",
    )(x, o, g, w)


def _ssm_in_body(x_ref, nw_ref, w_ref, z_ref, xbc_ref, dt_ref):
    h = _rmsnorm(x_ref[...], nw_ref[...]).astype(BF16)
    z_ref[...] = _dot(h, w_ref[:, :D_INNER]).astype(z_ref.dtype)
    xbc_ref[...] = _dot(h, w_ref[:, D_INNER:D_INNER + CONV_DIM])
    dt_ref[:, :SSM_HEADS] = _dot(h, w_ref[:, D_INNER + CONV_DIM:])
    dt_ref[:, SSM_HEADS:] = jnp.zeros((dt_ref.shape[0], LANES - SSM_HEADS), F32)


def _ssm_in(x, nw, w, layer, act_dtype, tm):
    n = x.shape[0]
    return pl.pallas_call(
        _ssm_in_body,
        grid=(n // tm,),
        in_specs=[_row_spec(D_MODEL, tm), _const_spec(nw.shape), _layer_spec(w, layer)],
        out_specs=[_row_spec(D_INNER, tm), _row_spec(CONV_DIM, tm), _row_spec(LANES, tm)],
        out_shape=[jax.ShapeDtypeStruct((n, D_INNER), act_dtype),
                   jax.ShapeDtypeStruct((n, CONV_DIM), F32),
                   jax.ShapeDtypeStruct((n, LANES), F32)],
        compiler_params=_params(("parallel",)),
        name="ssm_in",
    )(x, nw, w)


def _lane_bcast(x, h):
    return jnp.broadcast_to(x[:, h:h + 1], (x.shape[0], LANES))


def _row_bcast(x, h):
    return jnp.broadcast_to(x[h:h + 1, :], (LANES, x.shape[1]))


def _pair_blockdiag(xp):
    lane = lax.broadcasted_iota(jnp.int32, xp.shape, 1)
    xb = xp.astype(BF16)
    zero = jnp.zeros_like(xb)
    return jnp.concatenate([jnp.where(lane < SSM_HEAD_DIM, xb, zero),
                            jnp.where(lane >= SSM_HEAD_DIM, xb, zero)], axis=0)


def _pair_select(a0, a1):
    lane = lax.broadcasted_iota(jnp.int32, a0.shape, 1)
    return jnp.where(lane < SSM_HEAD_DIM, a0, a1)


def _intra_pair(cb, cmask, a2, a2_t, dt_t, pair):
    ms = []
    for hh in range(2):
        h = 2 * pair + hh
        seg = _lane_bcast(a2, h) - _row_bcast(a2_t, h)
        dec = jnp.exp2(jnp.where(cmask, seg, NEG))
        ms.append((cb * dec * _row_bcast(dt_t, h)).astype(BF16))
    return jnp.concatenate(ms, axis=1)


def _pair_out_scale(e_all, pair):
    return _pair_select(_lane_bcast(e_all, 2 * pair), _lane_bcast(e_all, 2 * pair + 1))


def _ssd_chunk_body(is_meta, n_sub, xbc_ref, dt_ref, mtile_ref, init_ref, cw_ref, cbias_ref, dtb_ref, alog_ref,
                    dsk_ref, e_ref, y_ref, st_out_ref, halo_ref, act_ref, st_ref):
    j = pl.program_id(1)
    n_steps = pl.num_programs(1)
    q = BLOCK
    halo = CONV_HALO

    @pl.when(j == 0)
    def _():
        if is_meta:
            halo_ref[...] = jnp.zeros((halo, CONV_DIM), F32)
            st_ref[...] = jnp.zeros(st_ref.shape, F32)
        else:
            halo_ref[...] = mtile_ref[0:halo, :]
            for pair in range(SSM_HEADS // 2):
                ps = slice(pair * LANES, (pair + 1) * LANES)
                st_ref[:, ps] = init_ref[0, ps, :].T

    def chunk(r0):
        n_sh = CONV_W - 1
        r = lax.broadcasted_iota(jnp.int32, (q, n_sh * q), 0)
        c = lax.broadcasted_iota(jnp.int32, (q, n_sh * q), 1)
        sel = (c % q == r - (c // q + 1)).astype(BF16)
        top = 8
        rh = lax.broadcasted_iota(jnp.int32, (top, n_sh * halo), 0)
        ch = lax.broadcasted_iota(jnp.int32, (top, n_sh * halo), 1)
        sel_halo = (ch % halo == halo + rh - (ch // halo + 1)).astype(BF16)

        def taps(x, cs):
            return jnp.concatenate([(cw_ref[CONV_W - 1 - k:CONV_W - k, cs] * x).astype(BF16)
                                    for k in range(1, CONV_W)], axis=0)

        width = 512
        for cc in range(CONV_DIM // width):
            cs = slice(cc * width, (cc + 1) * width)
            xc = xbc_ref[r0:r0 + q, cs]
            conv = cbias_ref[:, cs] + cw_ref[CONV_W - 1:CONV_W, cs] * xc + _dot(sel, taps(xc, cs))
            act_ref[:, cs] = _silu(conv)
            conv_top = conv[0:top, :] + _dot(sel_halo, taps(halo_ref[:, cs], cs))
            act_ref[0:top, cs] = _silu(conv_top)
        halo_ref[...] = xbc_ref[r0 + q - halo:r0 + q, :]

        ri = lax.broadcasted_iota(jnp.int32, (q, q), 0)
        ci = lax.broadcasted_iota(jnp.int32, (q, q), 1)
        dt_ok = ci < SSM_HEADS
        if is_meta:
            dt_ok = dt_ok & (ri < N_META)
        dt = jnp.where(dt_ok, _softplus(dt_ref[r0:r0 + q, :] + dtb_ref[...]), 0.0)
        a = dt * (-jnp.exp(alog_ref[...]))
        cmask = ri >= ci
        a2 = _exact_left(cmask.astype(BF16), a) * LOG2E
        a2_t = a2.T
        dt_t = dt.T
        e_all = jnp.exp2(a2)
        w_t = dt_t * jnp.exp2(jnp.broadcast_to(a2_t[:, q - 1:q], (q, q)) - a2_t)
        dec_rows = _exact_right(jnp.broadcast_to(e_all[q - 1:q, :], (8, LANES)), e_ref[...])[0:1, :]
        rowi = lax.broadcasted_iota(jnp.int32, (q, LANES), 0)

        for g in range(SSM_GROUPS):
            gs = slice(D_INNER + g * D_STATE, D_INNER + (g + 1) * D_STATE)
            bg = act_ref[:, gs]
            cg = act_ref[:, SSM_GROUPS * D_STATE + gs.start:SSM_GROUPS * D_STATE + gs.stop].astype(BF16)
            cb = _dot_nt(cg, bg.astype(BF16))
            bg_t = bg.T
            y_off = _dot(cg, st_ref[:, g * GROUP_WIDTH:(g + 1) * GROUP_WIDTH].astype(BF16))
            for pr in range(PAIRS_PER_GROUP):
                pair = g * PAIRS_PER_GROUP + pr
                ps = slice(pair * LANES, (pair + 1) * LANES)
                xp = act_ref[:, ps]
                if is_meta:
                    xp = jnp.where(rowi < N_META, xp, 0.0)
                xbd = _pair_blockdiag(xp)
                y = _dot(_intra_pair(cb, cmask, a2, a2_t, dt_t, pair), xbd)
                st = st_ref[:, ps]
                y = y + y_off[:, pr * LANES:(pr + 1) * LANES] * _pair_out_scale(e_all, pair)
                wn = jnp.concatenate([(bg_t * _row_bcast(w_t, 2 * pair)).astype(BF16),
                                      (bg_t * _row_bcast(w_t, 2 * pair + 1)).astype(BF16)], axis=1)
                st_ref[:, ps] = st * dec_rows[:, ps] + _dot(wn, xbd)
                y_ref[r0:r0 + q, ps] = (y + dsk_ref[:, ps] * xp).astype(y_ref.dtype)

    for sub in range(n_sub):
        chunk(sub * q)

    @pl.when(j == n_steps - 1)
    def _():
        for pair in range(SSM_HEADS // 2):
            ps = slice(pair * LANES, (pair + 1) * LANES)
            st_out_ref[0, ps, :] = st_ref[:, ps].T


def _ssd_chunks(xbc, dt, xbc_small, init, cw, cbias, dtb, alog, dsk, emat, n_batch, n_chunks, n_sub, first_block,
                meta_block, is_meta, out_dtype):
    assert n_chunks % n_sub == 0 and first_block % n_sub == 0
    n_steps = n_chunks // n_sub
    rows = n_sub * BLOCK
    cur = lambda b, j: (first_block // n_sub + b * n_steps + j, 0)
    out = lambda b, j: (b * n_steps + j, 0)
    return pl.pallas_call(
        functools.partial(_ssd_chunk_body, is_meta, n_sub),
        grid=(n_batch, n_steps),
        in_specs=[pl.BlockSpec((rows, CONV_DIM), cur), pl.BlockSpec((rows, LANES), cur),
                  pl.BlockSpec((BLOCK, CONV_DIM), lambda b, j: (meta_block, 0)), _const_spec(init.shape),
                  _const_spec(cw.shape), _const_spec(cbias.shape), _const_spec(dtb.shape),
                  _const_spec(alog.shape), _const_spec(dsk.shape), _const_spec(emat.shape)],
        out_specs=[pl.BlockSpec((rows, D_INNER), out),
                   pl.BlockSpec((1, D_INNER, D_STATE), lambda b, j: (b, 0, 0))],
        out_shape=[jax.ShapeDtypeStruct((n_batch * n_chunks * BLOCK, D_INNER), out_dtype),
                   jax.ShapeDtypeStruct((n_batch, D_INNER, D_STATE), F32)],
        scratch_shapes=[pltpu.VMEM((CONV_HALO, CONV_DIM), F32), pltpu.VMEM((BLOCK, CONV_DIM), F32),
                        pltpu.VMEM((D_STATE, D_INNER), F32)],
        compiler_params=_params(("parallel", "arbitrary")),
        name="ssd_meta" if is_meta else "ssd_prompt",
    )(xbc, dt, xbc_small, init, cw, cbias, dtb, alog, dsk, emat)


def _ssd_sample_body(*refs):
    nst_ref = refs[-2]

    last = pl.num_programs(1) - 1

    @pl.when(pl.program_id(1) == last)
    def _():
        _ssd_sample_step(*refs)

    @pl.when(pl.program_id(1) < last)
    def _():
        nst_ref[...] = jnp.zeros(nst_ref.shape, F32)


def _ssd_sample_step(xbc_ref, dt_ref, cs_ref, st_ref, cw_ref, cbias_ref, dtb_ref, alog_ref, dsk_ref, e_ref,
                     *rest):
    y_ref, nst_ref, yt_ref = rest[-3:]
    n_seq = st_ref.shape[0]
    n_new = xbc_ref.shape[0] // n_seq
    n_cs = CONV_W - 1
    qr = n_seq * n_new
    q = LANES
    kpad = 64
    assert qr + n_seq * n_cs <= kpad and n_new >= n_cs

    x_new = xbc_ref[...]
    xc = jnp.concatenate([x_new, cs_ref[...], jnp.zeros((kpad - qr - n_seq * n_cs, CONV_DIM), F32)], axis=0)
    r = lax.broadcasted_iota(jnp.int32, (qr, kpad), 0)
    c = lax.broadcasted_iota(jnp.int32, (qr, kpad), 1)
    s_of_r = r // n_new
    t_of_r = r % n_new
    conv = cbias_ref[...] + cw_ref[CONV_W - 1:CONV_W, :] * x_new
    for k in range(1, CONV_W):
        target = jnp.where(t_of_r >= k, r - k, qr + n_cs * s_of_r + t_of_r + n_cs - k)
        conv = conv + cw_ref[CONV_W - 1 - k:CONV_W - k, :] * _exact_left((c == target).astype(BF16), xc)
    act = jnp.concatenate([_silu(conv), jnp.zeros((q - qr, CONV_DIM), F32)], axis=0)
    xh = act[:, :D_INNER]
    bm = act[:, D_INNER:D_INNER + SSM_GROUPS * D_STATE]
    cm = act[:, D_INNER + SSM_GROUPS * D_STATE:]
    ri = lax.broadcasted_iota(jnp.int32, (q, q), 0)
    ci = lax.broadcasted_iota(jnp.int32, (q, q), 1)
    dt = jnp.concatenate([_softplus(dt_ref[...] + dtb_ref[...]), jnp.zeros((q - qr, LANES), F32)], axis=0)
    dt = jnp.where(ci < SSM_HEADS, dt, 0.0)
    a = dt * (-jnp.exp(alog_ref[...]))
    cmask = (ri >= ci) & (ri // n_new == ci // n_new) & (ri < qr)
    a_cum = _exact_left(cmask.astype(BF16), a)
    a2 = a_cum * LOG2E
    a2_t = a2.T
    dt_t = dt.T
    e_all = jnp.exp2(a2)
    last = ((ci == (ri // n_new) * n_new + n_new - 1) & (ri < qr)).astype(BF16)
    a2_last = _exact_left(last, a_cum) * LOG2E
    w_exp = _exact_right(dt * jnp.exp2(a2_last - a2), e_ref[...])
    dec_exp = _exact_right(jnp.exp2(a2_last), e_ref[...])
    xw = xh * w_exp

    yt_ref[...] = jnp.zeros(yt_ref.shape, F32)
    colseq = ci // n_new
    rowseq = ri // n_new
    for g in range(SSM_GROUPS):
        gs = slice(g * D_STATE, (g + 1) * D_STATE)
        hs = slice(g * GROUP_WIDTH, (g + 1) * GROUP_WIDTH)
        cg_t = cm[:, gs].T.astype(BF16)
        bg = bm[:, gs].astype(BF16)
        xw_t = jnp.concatenate(
            [xw[:, g * GROUP_WIDTH + i * LANES:g * GROUP_WIDTH + (i + 1) * LANES].T
             for i in range(GROUP_WIDTH // LANES)], axis=0).astype(BF16)
        dec_t = jnp.concatenate(
            [dec_exp[:, g * GROUP_WIDTH + i * LANES:g * GROUP_WIDTH + (i + 1) * LANES].T
             for i in range(GROUP_WIDTH // LANES)], axis=0)
        zero = jnp.zeros((q, q), BF16)
        for s in range(n_seq):
            st0 = st_ref[s, hs, :]
            yt_ref[hs, :] += _dot(st0.astype(BF16), jnp.where(colseq == s, cg_t, zero))
            inc = _dot(xw_t, jnp.where(rowseq == s, bg, zero))
            dcol = jnp.broadcast_to(dec_t[:, s * n_new:s * n_new + 1], (GROUP_WIDTH, q))
            nst_ref[s, hs, :] = st0 * dcol + inc

    for g in range(SSM_GROUPS):
        gs = slice(g * D_STATE, (g + 1) * D_STATE)
        cb = _dot_nt(cm[:, gs].astype(BF16), bm[:, gs].astype(BF16))
        for pr in range(PAIRS_PER_GROUP):
            pair = g * PAIRS_PER_GROUP + pr
            ps = slice(pair * LANES, (pair + 1) * LANES)
            xp = xh[:, ps]
            y = _dot(_intra_pair(cb, cmask, a2, a2_t, dt_t, pair), _pair_blockdiag(xp))
            y = y + yt_ref[ps, :].T * _pair_out_scale(e_all, pair) + dsk_ref[:, ps] * xp
            y_ref[:, ps] = y[0:qr, :]


def _ssd_sample(xbc, dt, cs, st, cw, cbias, dtb, alog, dsk, emat, n_new, layer, nst_prev=None):
    n_layers, n_seq = st.shape[0], st.shape[1]
    sb = SAMPLE_SEQS
    n_pass = n_layers if nst_prev is None else 1
    rows = lambda cols, per: pl.BlockSpec((sb * per, cols), lambda i, t: (i, 0))
    st_spec = pl.BlockSpec((None, sb, D_INNER, D_STATE), lambda i, t: (layer, i, 0, 0))
    nst_spec = pl.BlockSpec((None, sb, D_INNER, D_STATE), lambda i, t: ((layer + t + 1) % n_pass, i, 0, 0))
    if nst_prev is not None:
        nst_spec = st_spec
    in_specs = [rows(CONV_DIM, n_new), rows(LANES, n_new), rows(CONV_DIM, CONV_W - 1), st_spec,
                _const_spec(cw.shape), _const_spec(cbias.shape), _const_spec(dtb.shape),
                _const_spec(alog.shape), _const_spec(dsk.shape), _const_spec(emat.shape)]
    args = [xbc, dt, cs, st, cw, cbias, dtb, alog, dsk, emat]
    aliases = {}
    if nst_prev is not None:
        in_specs.append(pl.BlockSpec(memory_space=pl.ANY))
        aliases = {len(args): 1}
        args.append(nst_prev)
    return pl.pallas_call(
        _ssd_sample_body,
        grid=(n_seq // sb, n_pass),
        in_specs=in_specs,
        out_specs=[rows(D_INNER, n_new), nst_spec],
        out_shape=[jax.ShapeDtypeStruct((n_seq * n_new, D_INNER), F32),
                   jax.ShapeDtypeStruct(st.shape, F32)],
        input_output_aliases=aliases,
        scratch_shapes=[pltpu.VMEM((D_INNER, LANES), F32)],
        compiler_params=_params(("parallel", "arbitrary")),
        name="ssd_sample",
    )(*args)


def _ssm_out_body(x_ref, y_ref, z_ref, nw_ref, w_ref, o_ref):
    y = y_ref[...].astype(F32) * _silu(z_ref[...].astype(F32))
    parts = []
    for g in range(SSM_GROUPS):
        yg = y[:, g * GROUP_WIDTH:(g + 1) * GROUP_WIDTH]
        parts.append(yg * lax.rsqrt(jnp.mean(yg * yg, axis=-1, keepdims=True) + EPS))
    yn = (jnp.concatenate(parts, axis=1) * nw_ref[...]).astype(BF16)
    o_ref[...] = x_ref[...] + _dot(yn, w_ref[...])


def _ssm_out(x, y, z, nw, w, layer, tm):
    n = x.shape[0]
    return pl.pallas_call(
        _ssm_out_body,
        grid=(n // tm,),
        in_specs=[_row_spec(D_MODEL, tm), _row_spec(D_INNER, tm), _row_spec(D_INNER, tm),
                  _const_spec(nw.shape), _layer_spec(w, layer)],
        out_specs=_row_spec(D_MODEL, tm),
        out_shape=jax.ShapeDtypeStruct(x.shape, F32),
        compiler_params=_params(("parallel",)),
        name="ssm_out",
    )(x, y, z, nw, w)


def _pad_lanes(v):
    return jnp.pad(v.astype(F32), (0, LANES - v.shape[0]))[None, :]


def kernel(x_prompt, x_sample, cache_k, cache_v, state_conv, state_ssm, meta_tokens, norm_w,
           w_attn_in, q_norm_w, k_norm_w, attn_sinks, w_attn_out, w_ssm_in, conv_w, conv_b,
           dt_bias, a_log, d_skip, ssm_norm_w, w_ssm_out):
    n_batch, seq, _ = x_prompt.shape
    n_seq, n_new, _ = x_sample.shape
    wbuf = cache_k.shape[2]
    n_blocks = seq // BLOCK
    n_rows = n_seq * n_new
    assert seq % BLOCK == 0 and wbuf == WINDOW and n_rows % BLOCK == 0 and n_new == 4
    meta_block = n_rows // BLOCK
    small_rows = n_rows + BLOCK

    xp = x_prompt.reshape(n_batch * seq, D_MODEL)
    xs = jnp.concatenate([x_sample.reshape(n_rows, D_MODEL), meta_tokens.astype(F32),
                          jnp.zeros((BLOCK - N_META, D_MODEL), F32)], axis=0)

    emat = jnp.pad(jnp.repeat(jnp.eye(SSM_HEADS, dtype=BF16), SSM_HEAD_DIM, axis=1),
                   ((0, LANES - SSM_HEADS), (0, 0)))
    zero_state = jnp.zeros((1, D_INNER, D_STATE), F32)

    w_attn_in_b, w_attn_out_b = w_attn_in.astype(BF16), w_attn_out.astype(BF16)
    w_ssm_in_b, w_ssm_out_b = w_ssm_in.astype(BF16), w_ssm_out.astype(BF16)
    n_attn, n_ssm = cache_k.shape[0], state_ssm.shape[0]
    ck_all = cache_k.reshape(n_attn, n_seq, wbuf * N_KV_HEADS, HEAD_DIM)
    cv_all = cache_v.reshape(n_attn, n_seq, wbuf * N_KV_HEADS, HEAD_DIM)
    st_all = state_ssm.reshape(n_ssm, n_seq, D_INNER, D_STATE)
    new_ck = new_cv = new_st = None

    kp_l, vp_l, cp_l, sp_l, cs_l = [], [], [], [], []
    for i in range(DEPTH):
        l = i // 2
        nw = norm_w[i][None, :]
        if i % 2 == 0:
            qn, kn = q_norm_w[l][None, :], k_norm_w[l][None, :]
            sinks = attn_sinks[l].astype(F32)

            qs, ks, vs, gs = _attn_in(xs, nw, w_attn_in_b, l, qn, kn, F32, small_rows)
            o_meta = _attn_blocks(qs, ks, vs, ks, vs, sinks, 1, 1, 1, meta_block, meta_block, True, F32)
            qq = qs[:n_rows].reshape(n_seq, n_new, N_KV_HEADS, GQA_GROUP * HEAD_DIM).transpose(0, 2, 1, 3)
            o, new_ck, new_cv = _attn_sample(qq.reshape(n_seq, N_HEADS * n_new, HEAD_DIM),
                                             ks[:n_rows].reshape(n_seq, n_new * N_KV_HEADS, HEAD_DIM),
                                             vs[:n_rows].reshape(n_seq, n_new * N_KV_HEADS, HEAD_DIM),
                                             ck_all, cv_all, sinks, l, new_ck, new_cv)
            o = o.reshape(n_seq, N_KV_HEADS, n_new, GQA_GROUP * HEAD_DIM).transpose(0, 2, 1, 3)
            o = jnp.concatenate([o.reshape(n_rows, ATTN_WIDTH), o_meta], axis=0)

            q, k, v, g = _attn_in(xp, nw, w_attn_in_b, l, qn, kn, BF16, ROW_TILE)
            op = _attn_blocks(q, k, v, ks, vs, sinks, n_batch, n_blocks, ATTN_SUB_BLOCKS, 0, meta_block, False, BF16)
            xp = _attn_out(xp, op, g, w_attn_out_b, l, ROW_TILE)
            xs = _attn_out(xs, o, gs, w_attn_out_b, l, small_rows)
            kp_l.append(k.reshape(n_batch, seq, KV_WIDTH)[:, -WINDOW:].reshape(n_batch, WINDOW, N_KV_HEADS, HEAD_DIM))
            vp_l.append(v.reshape(n_batch, seq, KV_WIDTH)[:, -WINDOW:].reshape(n_batch, WINDOW, N_KV_HEADS, HEAD_DIM))
        else:
            cw, cbias = conv_w[l].astype(F32), conv_b[l][None, :].astype(F32)
            dtb, alog = _pad_lanes(dt_bias[l]), _pad_lanes(a_log[l])
            dsk = jnp.repeat(d_skip[l].astype(F32), SSM_HEAD_DIM)[None, :]
            snw = ssm_norm_w[l][None, :].astype(F32)
            consts = (cw, cbias, dtb, alog, dsk, emat)

            zs, xbcs, dts = _ssm_in(xs, nw, w_ssm_in_b, l, F32, small_rows)
            y_meta, st_meta = _ssd_chunks(xbcs, dts, xbcs, zero_state, *consts, 1, 1, 1, meta_block, meta_block,
                                          True, F32)
            y, new_st = _ssd_sample(xbcs, dts, state_conv[l].reshape(n_seq * (CONV_W - 1), CONV_DIM), st_all,
                                    *consts, n_new, l, new_st)
            y = jnp.concatenate([y, y_meta], axis=0)

            z, xbc, dt = _ssm_in(xp, nw, w_ssm_in_b, l, BF16, ROW_TILE)
            yp, st = _ssd_chunks(xbc, dt, xbcs, st_meta, *consts, n_batch, n_blocks, SSD_SUB_CHUNKS, 0, meta_block,
                                 False, BF16)
            xp = _ssm_out(xp, yp, z, snw, w_ssm_out_b, l, ROW_TILE)
            xs = _ssm_out(xs, y, zs, snw, w_ssm_out_b, l, small_rows)
            cp_l.append(xbc.reshape(n_batch, seq, CONV_DIM)[:, -(CONV_W - 1):])
            sp_l.append(st.reshape(n_batch, SSM_HEADS, SSM_HEAD_DIM, D_STATE))
            cs_l.append(xbcs[:n_rows].reshape(n_seq, n_new, CONV_DIM)[:, -(CONV_W - 1):])

    y_prompt = xp.reshape(n_batch, seq, D_MODEL)
    y_sample = xs[:n_rows].reshape(n_seq, n_new, D_MODEL)
    return (y_prompt, y_sample,
            jnp.stack(kp_l), jnp.stack(vp_l), jnp.stack(cp_l), jnp.stack(sp_l),
            new_ck.reshape(cache_k.shape), new_cv.reshape(cache_v.shape), jnp.stack(cs_l),
            new_st.reshape(state_ssm.shape))
```

```python
import functools
import math

import jax
import jax.numpy as jnp
from jax import lax
from jax.experimental import pallas as pl
from jax.experimental.pallas import tpu as pltpu

F32 = jnp.float32
BF16 = jnp.bfloat16

D_MODEL = 1024
DEPTH = 4
N_META = 16
WINDOW = 128
BLOCK = 128
HEAD_DIM = 128
ATTN_WIDTH = 2 * D_MODEL
N_HEADS = ATTN_WIDTH // HEAD_DIM
N_KV_HEADS = 4
GQA_GROUP = N_HEADS // N_KV_HEADS
KV_WIDTH = N_KV_HEADS * HEAD_DIM
D_INNER = 2 * D_MODEL
SSM_HEAD_DIM = 64
SSM_HEADS = D_INNER // SSM_HEAD_DIM
D_STATE = 128
SSM_GROUPS = 4
GROUP_WIDTH = D_INNER // SSM_GROUPS
PAIRS_PER_GROUP = SSM_HEADS // SSM_GROUPS // 2
CONV_W = 4
CONV_DIM = D_INNER + 2 * SSM_GROUPS * D_STATE
EPS = 1e-6
NEG = -1e30
LOG2E = math.log2(math.e)
QK_SCALE_LOG2 = HEAD_DIM ** -0.5 * LOG2E

LANES = 128
ROW_TILE = 512
SAMPLE_SEQS = 8
ATTN_SUB_BLOCKS = 8
SSD_SUB_CHUNKS = 1
CONV_HALO = 16
V7X_VMEM_BYTES = 64 * 1024 * 1024
VMEM_LIMIT = V7X_VMEM_BYTES * 7 // 8


def _dot(a, b):
    return jnp.dot(a, b, preferred_element_type=F32)


def _dot_nt(a, b):
    return lax.dot_general(a, b, (((1,), (1,)), ((), ())), preferred_element_type=F32)


def _rmsnorm(x, w):
    return x * lax.rsqrt(jnp.mean(x * x, axis=-1, keepdims=True) + EPS) * w


def _silu(x):
    return x * jax.nn.sigmoid(x)


def _softplus(x):
    return jnp.maximum(x, 0.0) + jnp.log1p(jnp.exp(-jnp.abs(x)))


def _split3(x):
    hi = x.astype(BF16)
    r = x - hi.astype(F32)
    mid = r.astype(BF16)
    lo = (r - mid.astype(F32)).astype(BF16)
    return hi, mid, lo


def _exact_left(p, x):
    hi, mid, lo = _split3(x)
    return _dot(p, hi) + _dot(p, mid) + _dot(p, lo)


def _exact_right(x, e):
    hi, mid, lo = _split3(x)
    return _dot(hi, e) + _dot(mid, e) + _dot(lo, e)


def _params(semantics):
    return pltpu.CompilerParams(dimension_semantics=semantics, vmem_limit_bytes=VMEM_LIMIT)


def _const_spec(shape):
    nd = len(shape)
    return pl.BlockSpec(shape, lambda *_: (0,) * nd)


def _row_spec(cols, tm):
    return pl.BlockSpec((tm, cols), lambda i: (i, 0))


def _layer_spec(w, layer):
    return pl.BlockSpec((None,) + w.shape[1:], lambda *_: (layer, 0, 0))


def _attn_in_body(x_ref, nw_ref, w_ref, qn_ref, kn_ref, q_ref, k_ref, v_ref, g_ref):
    h = _rmsnorm(x_ref[...], nw_ref[...]).astype(BF16)
    q = _dot(h, w_ref[:, :ATTN_WIDTH])
    for hd in range(N_HEADS):
        sl = slice(hd * HEAD_DIM, (hd + 1) * HEAD_DIM)
        q_ref[:, sl] = (_rmsnorm(q[:, sl], qn_ref[...]) * QK_SCALE_LOG2).astype(q_ref.dtype)
    k = _dot(h, w_ref[:, ATTN_WIDTH:ATTN_WIDTH + KV_WIDTH])
    for hd in range(N_KV_HEADS):
        sl = slice(hd * HEAD_DIM, (hd + 1) * HEAD_DIM)
        k_ref[:, sl] = _rmsnorm(k[:, sl], kn_ref[...])
    v_ref[...] = _dot(h, w_ref[:, ATTN_WIDTH + KV_WIDTH:ATTN_WIDTH + 2 * KV_WIDTH])
    g_ref[...] = _dot(h, w_ref[:, ATTN_WIDTH + 2 * KV_WIDTH:]).astype(g_ref.dtype)


def _attn_in(x, nw, depth, w, qn, kn, layer, act_dtype, tm):
    n = x.shape[0]
    return pl.pallas_call(
        _attn_in_body,
        grid=(n // tm,),
        in_specs=[_row_spec(D_MODEL, tm), _layer_spec(nw, depth), _layer_spec(w, layer),
                  _layer_spec(qn, layer), _layer_spec(kn, layer)],
        out_specs=[_row_spec(ATTN_WIDTH, tm), _row_spec(KV_WIDTH, tm), _row_spec(KV_WIDTH, tm),
                   _row_spec(ATTN_WIDTH, tm)],
        out_shape=[jax.ShapeDtypeStruct((n, ATTN_WIDTH), act_dtype),
                   jax.ShapeDtypeStruct((n, KV_WIDTH), F32),
                   jax.ShapeDtypeStruct((n, KV_WIDTH), F32),
                   jax.ShapeDtypeStruct((n, ATTN_WIDTH), act_dtype)],
        compiler_params=_params(("parallel",)),
        name="attn_in",
    )(x, nw, w, qn, kn)


def _attn_block_body(is_meta, n_sub, layer, sink_ref, q_ref, kc_ref, kp_ref, vc_ref, vp_ref, km_ref, vm_ref, o_ref):
    j = pl.program_id(1)
    rows = GQA_GROUP * BLOCK
    rowg = lax.broadcasted_iota(jnp.int32, (rows, 1), 0) // BLOCK
    row = lax.broadcasted_iota(jnp.int32, (rows, 2 * BLOCK), 0) & (BLOCK - 1)
    col = lax.broadcasted_iota(jnp.int32, (rows, 2 * BLOCK), 1)
    cur_ok = (col >= BLOCK) & (col - BLOCK <= row)

    def block(r0, kprev, vprev, n_prev):
        mask = cur_ok
        if n_prev:
            mask = mask | ((col < n_prev) & (col > row - (BLOCK - n_prev)))
        for kv in range(N_KV_HEADS):
            sl = slice(kv * HEAD_DIM, (kv + 1) * HEAD_DIM)
            keys = jnp.concatenate([kprev(sl), kc_ref[r0:r0 + BLOCK, sl]], axis=0).astype(BF16)
            vals = jnp.concatenate([vprev(sl), vc_ref[r0:r0 + BLOCK, sl]], axis=0).astype(BF16)
            base = kv * GQA_GROUP * HEAD_DIM
            q4 = jnp.concatenate(
                [q_ref[r0:r0 + BLOCK, base + g * HEAD_DIM:base + (g + 1) * HEAD_DIM] for g in range(GQA_GROUP)],
                axis=0).astype(BF16)
            sink = jnp.zeros((rows, 1), F32)
            for g in range(GQA_GROUP):
                sink = jnp.where(rowg == g, sink_ref[layer, kv * GQA_GROUP + g] * LOG2E, sink)
            s = jnp.where(mask, _dot_nt(q4, keys), NEG)
            m = jnp.maximum(jnp.max(s, axis=-1, keepdims=True), sink)
            p = jnp.exp2(s - m)
            denom = jnp.sum(p, axis=-1, keepdims=True) + jnp.exp2(sink - m)
            o = _dot(p.astype(BF16), vals) / denom
            for g in range(GQA_GROUP):
                o_ref[r0:r0 + BLOCK, base + g * HEAD_DIM:base + (g + 1) * HEAD_DIM] = (
                    o[g * BLOCK:(g + 1) * BLOCK].astype(o_ref.dtype))

    if is_meta:
        block(0, lambda sl: km_ref[:, sl], lambda sl: vm_ref[:, sl], 0)
    else:
        @pl.when(j == 0)
        def _():
            block(0, lambda sl: km_ref[:, sl], lambda sl: vm_ref[:, sl], N_META)

        @pl.when(j > 0)
        def _():
            block(0, lambda sl: kp_ref[:, sl], lambda sl: vp_ref[:, sl], BLOCK)

    for sub in range(1, n_sub):
        p0 = (sub - 1) * BLOCK
        block(sub * BLOCK, lambda sl, p0=p0: kc_ref[p0:p0 + BLOCK, sl], lambda sl, p0=p0: vc_ref[p0:p0 + BLOCK, sl],
              BLOCK)


def _attn_blocks(q, k, v, k_small, v_small, sinks, layer, n_batch, n_blocks, n_sub, first_block, meta_block,
                 is_meta, out_dtype):
    assert n_blocks % n_sub == 0 and first_block % n_sub == 0
    n_steps = n_blocks // n_sub
    rows = n_sub * BLOCK
    cur = lambda b, j: (first_block // n_sub + b * n_steps + j, 0)
    prev = lambda b, j: (jnp.maximum(first_block + (b * n_steps + j) * n_sub - 1, 0), 0)
    meta = lambda b, j: (meta_block, 0)
    out = lambda b, j: (b * n_steps + j, 0)
    return pl.pallas_call(
        functools.partial(_attn_block_body, is_meta, n_sub, layer),
        grid=(n_batch, n_steps),
        in_specs=[pl.BlockSpec(memory_space=pltpu.SMEM),
                  pl.BlockSpec((rows, ATTN_WIDTH), cur),
                  pl.BlockSpec((rows, KV_WIDTH), cur), pl.BlockSpec((BLOCK, KV_WIDTH), prev),
                  pl.BlockSpec((rows, KV_WIDTH), cur), pl.BlockSpec((BLOCK, KV_WIDTH), prev),
                  pl.BlockSpec((BLOCK, KV_WIDTH), meta), pl.BlockSpec((BLOCK, KV_WIDTH), meta)],
        out_specs=pl.BlockSpec((rows, ATTN_WIDTH), out),
        out_shape=jax.ShapeDtypeStruct((n_batch * n_blocks * BLOCK, ATTN_WIDTH), out_dtype),
        compiler_params=_params(("parallel", "parallel")),
        name="attn_meta" if is_meta else "attn_prompt",
    )(sinks, q, k, k, v, v, k_small, v_small)


def _attn_sample_body(fill_layers, layer, sink_ref, q_ref, k_ref, v_ref, ck_ref, cv_ref, *rest):
    o_ref, nk_ref, nv_ref = rest[-3:]
    if fill_layers:
        for other in range(fill_layers):
            if other != layer:
                nk_ref[other] = jnp.zeros(nk_ref.shape[1:], F32)
                nv_ref[other] = jnp.zeros(nv_ref.shape[1:], F32)
        nk_ref, nv_ref = nk_ref.at[layer], nv_ref.at[layer]
    n_seq, rows, _ = q_ref.shape
    n_cache = ck_ref.shape[1]
    n_fresh = k_ref.shape[1]
    wbuf = n_cache // N_KV_HEADS
    per_kv = rows // N_KV_HEADS
    row = lax.broadcasted_iota(jnp.int32, (rows, n_cache + n_fresh), 0)
    col = lax.broadcasted_iota(jnp.int32, (rows, n_cache + n_fresh), 1)
    qpos = (row % per_kv) // GQA_GROUP
    kpos = jnp.where(col < n_cache, col // N_KV_HEADS - wbuf, (col - n_cache) // N_KV_HEADS)
    diff = qpos - kpos
    mask = (col % N_KV_HEADS == row // per_kv) & (diff >= 0) & (diff < WINDOW)
    row1 = lax.broadcasted_iota(jnp.int32, (rows, 1), 0)
    sink = jnp.zeros((rows, 1), F32)
    for hd in range(N_HEADS):
        sink = jnp.where((row1 // per_kv == hd // GQA_GROUP) & (row1 % GQA_GROUP == hd % GQA_GROUP),
                         sink_ref[layer, hd] * LOG2E, sink)
    for s in range(n_seq):
        keys = jnp.concatenate([ck_ref[s], k_ref[s]], axis=0).astype(BF16)
        vals = jnp.concatenate([cv_ref[s], v_ref[s]], axis=0).astype(BF16)
        sc = _dot_nt(q_ref[s].astype(BF16), keys)
        sc = jnp.where(mask, sc, NEG)
        m = jnp.maximum(jnp.max(sc, axis=-1, keepdims=True), sink)
        p = jnp.exp2(sc - m)
        denom = jnp.sum(p, axis=-1, keepdims=True) + jnp.exp2(sink - m)
        o_ref[s] = _dot(p.astype(BF16), vals) / denom
        nk_ref[s, 0:n_cache - n_fresh, :] = ck_ref[s, n_fresh:, :]
        nk_ref[s, n_cache - n_fresh:, :] = k_ref[s]
        nv_ref[s, 0:n_cache - n_fresh, :] = cv_ref[s, n_fresh:, :]
        nv_ref[s, n_cache - n_fresh:, :] = v_ref[s]


def _attn_sample(q, k, v, ck, cv, sinks, layer, nk_prev=None, nv_prev=None):
    sb = SAMPLE_SEQS
    n_seq = q.shape[0]
    n_layers = ck.shape[0]
    blk = lambda a: pl.BlockSpec((sb,) + a.shape[1:], lambda i: (i,) + (0,) * (a.ndim - 1))
    cache_spec = pl.BlockSpec((None, sb) + ck.shape[2:], lambda i: (layer, i, 0, 0))
    in_specs = [pl.BlockSpec(memory_space=pltpu.SMEM), blk(q), blk(k), blk(v), cache_spec, cache_spec]
    args = [sinks, q, k, v, ck, cv]
    if nk_prev is None:
        aliases, fill_layers = {}, n_layers
        new_spec = pl.BlockSpec((n_layers, sb) + ck.shape[2:], lambda i: (0, i, 0, 0))
    else:
        in_specs += [pl.BlockSpec(memory_space=pl.ANY)] * 2
        aliases, fill_layers = {len(args): 1, len(args) + 1: 2}, None
        new_spec = cache_spec
        args += [nk_prev, nv_prev]
    return pl.pallas_call(
        functools.partial(_attn_sample_body, fill_layers, layer),
        grid=(n_seq // sb,),
        in_specs=in_specs,
        out_specs=[blk(q), new_spec, new_spec],
        out_shape=[jax.ShapeDtypeStruct(q.shape, F32),
                   jax.ShapeDtypeStruct(ck.shape, F32),
                   jax.ShapeDtypeStruct(cv.shape, F32)],
        input_output_aliases=aliases,
        compiler_params=_params(("parallel",)),
        name="attn_sample",
    )(*args)


def _attn_out_body(x_ref, o_ref, g_ref, w_ref, y_ref):
    a = o_ref[...].astype(F32) * _silu(g_ref[...].astype(F32))
    y_ref[...] = x_ref[...] + _dot(a.astype(BF16), w_ref[...])


def _attn_out(x, o, g, w, layer, tm):
    n = x.shape[0]
    return pl.pallas_call(
        _attn_out_body,
        grid=(n // tm,),
        in_specs=[_row_spec(D_MODEL, tm), _row_spec(ATTN_WIDTH, tm), _row_spec(ATTN_WIDTH, tm),
                  _layer_spec(w, layer)],
        out_specs=_row_spec(D_MODEL, tm),
        out_shape=jax.ShapeDtypeStruct(x.shape, F32),
        compiler_params=_params(("parallel",)),
        name="attn_out",
    )(x, o, g, w)


def _ssm_in_body(x_ref, nw_ref, w_ref, z_ref, xbc_ref, dt_ref):
    h = _rmsnorm(x_ref[...], nw_ref[...]).astype(BF16)
    z_ref[...] = _dot(h, w_ref[:, :D_INNER]).astype(z_ref.dtype)
    xbc_ref[...] = _dot(h, w_ref[:, D_INNER:D_INNER + CONV_DIM])
    dt_ref[:, :SSM_HEADS] = _dot(h, w_ref[:, D_INNER + CONV_DIM:])
    dt_ref[:, SSM_HEADS:] = jnp.zeros((dt_ref.shape[0], LANES - SSM_HEADS), F32)


def _ssm_in(x, nw, depth, w, layer, act_dtype, tm):
    n = x.shape[0]
    return pl.pallas_call(
        _ssm_in_body,
        grid=(n // tm,),
        in_specs=[_row_spec(D_MODEL, tm), _layer_spec(nw, depth), _layer_spec(w, layer)],
        out_specs=[_row_spec(D_INNER, tm), _row_spec(CONV_DIM, tm), _row_spec(LANES, tm)],
        out_shape=[jax.ShapeDtypeStruct((n, D_INNER), act_dtype),
                   jax.ShapeDtypeStruct((n, CONV_DIM), F32),
                   jax.ShapeDtypeStruct((n, LANES), F32)],
        compiler_params=_params(("parallel",)),
        name="ssm_in",
    )(x, nw, w)


def _lane_bcast(x, h):
    return jnp.broadcast_to(x[:, h:h + 1], (x.shape[0], LANES))


def _row_bcast(x, h):
    return jnp.broadcast_to(x[h:h + 1, :], (LANES, x.shape[1]))


def _pair_blockdiag(xp):
    lane = lax.broadcasted_iota(jnp.int32, xp.shape, 1)
    xb = xp.astype(BF16)
    zero = jnp.zeros_like(xb)
    return jnp.concatenate([jnp.where(lane < SSM_HEAD_DIM, xb, zero),
                            jnp.where(lane >= SSM_HEAD_DIM, xb, zero)], axis=0)


def _pair_select(a0, a1):
    lane = lax.broadcasted_iota(jnp.int32, a0.shape, 1)
    return jnp.where(lane < SSM_HEAD_DIM, a0, a1)


def _intra_pair(cb, cmask, a2, a2_t, dt_t, pair):
    ms = []
    for hh in range(2):
        h = 2 * pair + hh
        seg = _lane_bcast(a2, h) - _row_bcast(a2_t, h)
        dec = jnp.exp2(jnp.where(cmask, seg, NEG))
        ms.append((cb * dec * _row_bcast(dt_t, h)).astype(BF16))
    return jnp.concatenate(ms, axis=1)


def _pair_out_scale(e_all, pair):
    return _pair_select(_lane_bcast(e_all, 2 * pair), _lane_bcast(e_all, 2 * pair + 1))


def _ssd_chunk_body(is_meta, n_sub, xbc_ref, dt_ref, mtile_ref, init_ref, cw_ref, cbias_ref, dtb_ref, alog_ref,
                    dsk_ref, e_ref, y_ref, st_out_ref, halo_ref, act_ref, st_ref):
    j = pl.program_id(1)
    n_steps = pl.num_programs(1)
    q = BLOCK
    halo = CONV_HALO

    @pl.when(j == 0)
    def _():
        if is_meta:
            halo_ref[...] = jnp.zeros((halo, CONV_DIM), F32)
            st_ref[...] = jnp.zeros(st_ref.shape, F32)
        else:
            halo_ref[...] = mtile_ref[0:halo, :]
            for pair in range(SSM_HEADS // 2):
                ps = slice(pair * LANES, (pair + 1) * LANES)
                st_ref[:, ps] = init_ref[0, ps, :].T

    def chunk(r0):
        n_sh = CONV_W - 1
        r = lax.broadcasted_iota(jnp.int32, (q, n_sh * q), 0)
        c = lax.broadcasted_iota(jnp.int32, (q, n_sh * q), 1)
        sel = (c % q == r - (c // q + 1)).astype(BF16)
        top = 8
        rh = lax.broadcasted_iota(jnp.int32, (top, n_sh * halo), 0)
        ch = lax.broadcasted_iota(jnp.int32, (top, n_sh * halo), 1)
        sel_halo = (ch % halo == halo + rh - (ch // halo + 1)).astype(BF16)

        def taps(x, cs):
            return jnp.concatenate([(cw_ref[CONV_W - 1 - k:CONV_W - k, cs] * x).astype(BF16)
                                    for k in range(1, CONV_W)], axis=0)

        width = 512
        for cc in range(CONV_DIM // width):
            cs = slice(cc * width, (cc + 1) * width)
            xc = xbc_ref[r0:r0 + q, cs]
            conv = cbias_ref[:, cs] + cw_ref[CONV_W - 1:CONV_W, cs] * xc + _dot(sel, taps(xc, cs))
            act_ref[:, cs] = _silu(conv)
            conv_top = conv[0:top, :] + _dot(sel_halo, taps(halo_ref[:, cs], cs))
            act_ref[0:top, cs] = _silu(conv_top)
        halo_ref[...] = xbc_ref[r0 + q - halo:r0 + q, :]

        ri = lax.broadcasted_iota(jnp.int32, (q, q), 0)
        ci = lax.broadcasted_iota(jnp.int32, (q, q), 1)
        dt_ok = ci < SSM_HEADS
        if is_meta:
            dt_ok = dt_ok & (ri < N_META)
        dt = jnp.where(dt_ok, _softplus(dt_ref[r0:r0 + q, :] + dtb_ref[...]), 0.0)
        a = dt * (-jnp.exp(alog_ref[...]))
        cmask = ri >= ci
        a2 = _exact_left(cmask.astype(BF16), a) * LOG2E
        a2_t = a2.T
        dt_t = dt.T
        e_all = jnp.exp2(a2)
        w_t = dt_t * jnp.exp2(jnp.broadcast_to(a2_t[:, q - 1:q], (q, q)) - a2_t)
        dec_rows = _exact_right(jnp.broadcast_to(e_all[q - 1:q, :], (8, LANES)), e_ref[...])[0:1, :]
        rowi = lax.broadcasted_iota(jnp.int32, (q, LANES), 0)

        for g in range(SSM_GROUPS):
            gs = slice(D_INNER + g * D_STATE, D_INNER + (g + 1) * D_STATE)
            bg = act_ref[:, gs]
            cg = act_ref[:, SSM_GROUPS * D_STATE + gs.start:SSM_GROUPS * D_STATE + gs.stop].astype(BF16)
            cb = _dot_nt(cg, bg.astype(BF16))
            bg_t = bg.T
            y_off = _dot(cg, st_ref[:, g * GROUP_WIDTH:(g + 1) * GROUP_WIDTH].astype(BF16))
            for pr in range(PAIRS_PER_GROUP):
                pair = g * PAIRS_PER_GROUP + pr
                ps = slice(pair * LANES, (pair + 1) * LANES)
                xp = act_ref[:, ps]
                if is_meta:
                    xp = jnp.where(rowi < N_META, xp, 0.0)
                xbd = _pair_blockdiag(xp)
                y = _dot(_intra_pair(cb, cmask, a2, a2_t, dt_t, pair), xbd)
                st = st_ref[:, ps]
                y = y + y_off[:, pr * LANES:(pr + 1) * LANES] * _pair_out_scale(e_all, pair)
                wn = jnp.concatenate([(bg_t * _row_bcast(w_t, 2 * pair)).astype(BF16),
                                      (bg_t * _row_bcast(w_t, 2 * pair + 1)).astype(BF16)], axis=1)
                st_ref[:, ps] = st * dec_rows[:, ps] + _dot(wn, xbd)
                y_ref[r0:r0 + q, ps] = (y + dsk_ref[:, ps] * xp).astype(y_ref.dtype)

    for sub in range(n_sub):
        chunk(sub * q)

    @pl.when(j == n_steps - 1)
    def _():
        for pair in range(SSM_HEADS // 2):
            ps = slice(pair * LANES, (pair + 1) * LANES)
            st_out_ref[0, ps, :] = st_ref[:, ps].T


def _ssd_chunks(xbc, dt, xbc_small, init, cw, cbias, dtb, alog, dsk, emat, layer, n_batch, n_chunks, n_sub,
                first_block, meta_block, is_meta, out_dtype):
    assert n_chunks % n_sub == 0 and first_block % n_sub == 0
    n_steps = n_chunks // n_sub
    rows = n_sub * BLOCK
    cur = lambda b, j: (first_block // n_sub + b * n_steps + j, 0)
    out = lambda b, j: (b * n_steps + j, 0)
    return pl.pallas_call(
        functools.partial(_ssd_chunk_body, is_meta, n_sub),
        grid=(n_batch, n_steps),
        in_specs=[pl.BlockSpec((rows, CONV_DIM), cur), pl.BlockSpec((rows, LANES), cur),
                  pl.BlockSpec((BLOCK, CONV_DIM), lambda b, j: (meta_block, 0)), _const_spec(init.shape),
                  _layer_spec(cw, layer), _layer_spec(cbias, layer), _layer_spec(dtb, layer),
                  _layer_spec(alog, layer), _layer_spec(dsk, layer), _const_spec(emat.shape)],
        out_specs=[pl.BlockSpec((rows, D_INNER), out),
                   pl.BlockSpec((1, D_INNER, D_STATE), lambda b, j: (b, 0, 0))],
        out_shape=[jax.ShapeDtypeStruct((n_batch * n_chunks * BLOCK, D_INNER), out_dtype),
                   jax.ShapeDtypeStruct((n_batch, D_INNER, D_STATE), F32)],
        scratch_shapes=[pltpu.VMEM((CONV_HALO, CONV_DIM), F32), pltpu.VMEM((BLOCK, CONV_DIM), F32),
                        pltpu.VMEM((D_STATE, D_INNER), F32)],
        compiler_params=_params(("parallel", "arbitrary")),
        name="ssd_meta" if is_meta else "ssd_prompt",
    )(xbc, dt, xbc_small, init, cw, cbias, dtb, alog, dsk, emat)


def _ssd_sample_body(*refs):
    nst_ref = refs[-2]

    last = pl.num_programs(1) - 1

    @pl.when(pl.program_id(1) == last)
    def _():
        _ssd_sample_step(*refs)

    @pl.when(pl.program_id(1) < last)
    def _():
        nst_ref[...] = jnp.zeros(nst_ref.shape, F32)


def _ssd_sample_step(xbc_ref, dt_ref, cs_ref, st_ref, cw_ref, cbias_ref, dtb_ref, alog_ref, dsk_ref, e_ref,
                     *rest):
    y_ref, nst_ref, yt_ref = rest[-3:]
    n_seq = st_ref.shape[0]
    n_new = xbc_ref.shape[0] // n_seq
    n_cs = CONV_W - 1
    qr = n_seq * n_new
    q = LANES
    kpad = 64
    assert qr + n_seq * n_cs <= kpad and n_new >= n_cs

    x_new = xbc_ref[...]
    xc = jnp.concatenate([x_new, cs_ref[...], jnp.zeros((kpad - qr - n_seq * n_cs, CONV_DIM), F32)], axis=0)
    r = lax.broadcasted_iota(jnp.int32, (qr, kpad), 0)
    c = lax.broadcasted_iota(jnp.int32, (qr, kpad), 1)
    s_of_r = r // n_new
    t_of_r = r % n_new
    conv = cbias_ref[...] + cw_ref[CONV_W - 1:CONV_W, :] * x_new
    for k in range(1, CONV_W):
        target = jnp.where(t_of_r >= k, r - k, qr + n_cs * s_of_r + t_of_r + n_cs - k)
        conv = conv + cw_ref[CONV_W - 1 - k:CONV_W - k, :] * _exact_left((c == target).astype(BF16), xc)
    act = jnp.concatenate([_silu(conv), jnp.zeros((q - qr, CONV_DIM), F32)], axis=0)
    xh = act[:, :D_INNER]
    bm = act[:, D_INNER:D_INNER + SSM_GROUPS * D_STATE]
    cm = act[:, D_INNER + SSM_GROUPS * D_STATE:]
    ri = lax.broadcasted_iota(jnp.int32, (q, q), 0)
    ci = lax.broadcasted_iota(jnp.int32, (q, q), 1)
    dt = jnp.concatenate([_softplus(dt_ref[...] + dtb_ref[...]), jnp.zeros((q - qr, LANES), F32)], axis=0)
    dt = jnp.where(ci < SSM_HEADS, dt, 0.0)
    a = dt * (-jnp.exp(alog_ref[...]))
    cmask = (ri >= ci) & (ri // n_new == ci // n_new) & (ri < qr)
    a_cum = _exact_left(cmask.astype(BF16), a)
    a2 = a_cum * LOG2E
    a2_t = a2.T
    dt_t = dt.T
    e_all = jnp.exp2(a2)
    last = ((ci == (ri // n_new) * n_new + n_new - 1) & (ri < qr)).astype(BF16)
    a2_last = _exact_left(last, a_cum) * LOG2E
    w_exp = _exact_right(dt * jnp.exp2(a2_last - a2), e_ref[...])
    dec_exp = _exact_right(jnp.exp2(a2_last), e_ref[...])
    xw = xh * w_exp

    yt_ref[...] = jnp.zeros(yt_ref.shape, F32)
    colseq = ci // n_new
    rowseq = ri // n_new
    for g in range(SSM_GROUPS):
        gs = slice(g * D_STATE, (g + 1) * D_STATE)
        hs = slice(g * GROUP_WIDTH, (g + 1) * GROUP_WIDTH)
        cg_t = cm[:, gs].T.astype(BF16)
        bg = bm[:, gs].astype(BF16)
        xw_t = jnp.concatenate(
            [xw[:, g * GROUP_WIDTH + i * LANES:g * GROUP_WIDTH + (i + 1) * LANES].T
             for i in range(GROUP_WIDTH // LANES)], axis=0).astype(BF16)
        dec_t = jnp.concatenate(
            [dec_exp[:, g * GROUP_WIDTH + i * LANES:g * GROUP_WIDTH + (i + 1) * LANES].T
             for i in range(GROUP_WIDTH // LANES)], axis=0)
        zero = jnp.zeros((q, q), BF16)
        for s in range(n_seq):
            st0 = st_ref[s, hs, :]
            yt_ref[hs, :] += _dot(st0.astype(BF16), jnp.where(colseq == s, cg_t, zero))
            inc = _dot(xw_t, jnp.where(rowseq == s, bg, zero))
            dcol = jnp.broadcast_to(dec_t[:, s * n_new:s * n_new + 1], (GROUP_WIDTH, q))
            nst_ref[s, hs, :] = st0 * dcol + inc

    for g in range(SSM_GROUPS):
        gs = slice(g * D_STATE, (g + 1) * D_STATE)
        cb = _dot_nt(cm[:, gs].astype(BF16), bm[:, gs].astype(BF16))
        for pr in range(PAIRS_PER_GROUP):
            pair = g * PAIRS_PER_GROUP + pr
            ps = slice(pair * LANES, (pair + 1) * LANES)
            xp = xh[:, ps]
            y = _dot(_intra_pair(cb, cmask, a2, a2_t, dt_t, pair), _pair_blockdiag(xp))
            y = y + yt_ref[ps, :].T * _pair_out_scale(e_all, pair) + dsk_ref[:, ps] * xp
            y_ref[:, ps] = y[0:qr, :]


def _ssd_sample(xbc, dt, cs, st, cw, cbias, dtb, alog, dsk, emat, n_new, layer, nst_prev=None):
    n_layers, n_seq = st.shape[0], st.shape[1]
    sb = SAMPLE_SEQS
    n_pass = n_layers if nst_prev is None else 1
    rows = lambda cols, per: pl.BlockSpec((sb * per, cols), lambda i, t: (i, 0))
    st_spec = pl.BlockSpec((None, sb, D_INNER, D_STATE), lambda i, t: (layer, i, 0, 0))
    nst_spec = pl.BlockSpec((None, sb, D_INNER, D_STATE), lambda i, t: ((layer + t + 1) % n_pass, i, 0, 0))
    if nst_prev is not None:
        nst_spec = st_spec
    in_specs = [rows(CONV_DIM, n_new), rows(LANES, n_new), rows(CONV_DIM, CONV_W - 1), st_spec,
                _layer_spec(cw, layer), _layer_spec(cbias, layer), _layer_spec(dtb, layer),
                _layer_spec(alog, layer), _layer_spec(dsk, layer), _const_spec(emat.shape)]
    args = [xbc, dt, cs, st, cw, cbias, dtb, alog, dsk, emat]
    aliases = {}
    if nst_prev is not None:
        in_specs.append(pl.BlockSpec(memory_space=pl.ANY))
        aliases = {len(args): 1}
        args.append(nst_prev)
    return pl.pallas_call(
        _ssd_sample_body,
        grid=(n_seq // sb, n_pass),
        in_specs=in_specs,
        out_specs=[rows(D_INNER, n_new), nst_spec],
        out_shape=[jax.ShapeDtypeStruct((n_seq * n_new, D_INNER), F32),
                   jax.ShapeDtypeStruct(st.shape, F32)],
        input_output_aliases=aliases,
        scratch_shapes=[pltpu.VMEM((D_INNER, LANES), F32)],
        compiler_params=_params(("parallel", "arbitrary")),
        name="ssd_sample",
    )(*args)


def _ssm_out_body(x_ref, y_ref, z_ref, nw_ref, w_ref, o_ref):
    y = y_ref[...].astype(F32) * _silu(z_ref[...].astype(F32))
    parts = []
    for g in range(SSM_GROUPS):
        yg = y[:, g * GROUP_WIDTH:(g + 1) * GROUP_WIDTH]
        parts.append(yg * lax.rsqrt(jnp.mean(yg * yg, axis=-1, keepdims=True) + EPS))
    yn = (jnp.concatenate(parts, axis=1) * nw_ref[...]).astype(BF16)
    o_ref[...] = x_ref[...] + _dot(yn, w_ref[...])


def _ssm_out(x, y, z, nw, w, layer, tm):
    n = x.shape[0]
    return pl.pallas_call(
        _ssm_out_body,
        grid=(n // tm,),
        in_specs=[_row_spec(D_MODEL, tm), _row_spec(D_INNER, tm), _row_spec(D_INNER, tm),
                  _layer_spec(nw, layer), _layer_spec(w, layer)],
        out_specs=_row_spec(D_MODEL, tm),
        out_shape=jax.ShapeDtypeStruct(x.shape, F32),
        compiler_params=_params(("parallel",)),
        name="ssm_out",
    )(x, y, z, nw, w)


def _rows3(p, width=None):
    p = p.astype(F32)
    if width is not None:
        p = jnp.pad(p, ((0, 0), (0, width - p.shape[1])))
    return p[:, None, :]


def kernel(x_prompt, x_sample, cache_k, cache_v, state_conv, state_ssm, meta_tokens, norm_w,
           w_attn_in, q_norm_w, k_norm_w, attn_sinks, w_attn_out, w_ssm_in, conv_w, conv_b,
           dt_bias, a_log, d_skip, ssm_norm_w, w_ssm_out):
    n_batch, seq, _ = x_prompt.shape
    n_seq, n_new, _ = x_sample.shape
    wbuf = cache_k.shape[2]
    n_blocks = seq // BLOCK
    n_rows = n_seq * n_new
    assert seq % BLOCK == 0 and wbuf == WINDOW and n_rows % BLOCK == 0 and n_new == 4
    meta_block = n_rows // BLOCK
    small_rows = n_rows + BLOCK

    xp = x_prompt.reshape(n_batch * seq, D_MODEL)
    xs = jnp.concatenate([x_sample.reshape(n_rows, D_MODEL), meta_tokens.astype(F32),
                          jnp.zeros((BLOCK - N_META, D_MODEL), F32)], axis=0)

    emat = jnp.pad(jnp.repeat(jnp.eye(SSM_HEADS, dtype=BF16), SSM_HEAD_DIM, axis=1),
                   ((0, LANES - SSM_HEADS), (0, 0)))
    zero_state = jnp.zeros((1, D_INNER, D_STATE), F32)

    w_attn_in_b, w_attn_out_b = w_attn_in.astype(BF16), w_attn_out.astype(BF16)
    w_ssm_in_b, w_ssm_out_b = w_ssm_in.astype(BF16), w_ssm_out.astype(BF16)
    n_attn, n_ssm = cache_k.shape[0], state_ssm.shape[0]
    ck_all = cache_k.reshape(n_attn, n_seq, wbuf * N_KV_HEADS, HEAD_DIM)
    cv_all = cache_v.reshape(n_attn, n_seq, wbuf * N_KV_HEADS, HEAD_DIM)
    st_all = state_ssm.reshape(n_ssm, n_seq, D_INNER, D_STATE)
    new_ck = new_cv = new_st = None

    nw, qn, kn, sinks = _rows3(norm_w), _rows3(q_norm_w), _rows3(k_norm_w), attn_sinks.astype(F32)
    snw = _rows3(ssm_norm_w)
    consts = (conv_w.astype(F32), _rows3(conv_b), _rows3(dt_bias, LANES), _rows3(a_log, LANES),
              _rows3(jnp.repeat(d_skip, SSM_HEAD_DIM, axis=1)), emat)

    kp_l, vp_l, cp_l, sp_l, cs_l = [], [], [], [], []
    for i in range(DEPTH):
        l = i // 2
        if i % 2 == 0:
            qs, ks, vs, gs = _attn_in(xs, nw, i, w_attn_in_b, qn, kn, l, F32, small_rows)
            o_meta = _attn_blocks(qs, ks, vs, ks, vs, sinks, l, 1, 1, 1, meta_block, meta_block, True, F32)
            qq = qs[:n_rows].reshape(n_seq, n_new, N_KV_HEADS, GQA_GROUP * HEAD_DIM).transpose(0, 2, 1, 3)
            o, new_ck, new_cv = _attn_sample(qq.reshape(n_seq, N_HEADS * n_new, HEAD_DIM),
                                             ks[:n_rows].reshape(n_seq, n_new * N_KV_HEADS, HEAD_DIM),
                                             vs[:n_rows].reshape(n_seq, n_new * N_KV_HEADS, HEAD_DIM),
                                             ck_all, cv_all, sinks, l, new_ck, new_cv)
            o = o.reshape(n_seq, N_KV_HEADS, n_new, GQA_GROUP * HEAD_DIM).transpose(0, 2, 1, 3)
            o = jnp.concatenate([o.reshape(n_rows, ATTN_WIDTH), o_meta], axis=0)

            q, k, v, g = _attn_in(xp, nw, i, w_attn_in_b, qn, kn, l, BF16, ROW_TILE)
            op = _attn_blocks(q, k, v, ks, vs, sinks, l, n_batch, n_blocks, ATTN_SUB_BLOCKS, 0, meta_block, False,
                              BF16)
            xp = _attn_out(xp, op, g, w_attn_out_b, l, ROW_TILE)
            xs = _attn_out(xs, o, gs, w_attn_out_b, l, small_rows)
            kp_l.append(k.reshape(n_batch, seq, KV_WIDTH)[:, -WINDOW:].reshape(n_batch, WINDOW, N_KV_HEADS, HEAD_DIM))
            vp_l.append(v.reshape(n_batch, seq, KV_WIDTH)[:, -WINDOW:].reshape(n_batch, WINDOW, N_KV_HEADS, HEAD_DIM))
        else:
            zs, xbcs, dts = _ssm_in(xs, nw, i, w_ssm_in_b, l, F32, small_rows)
            y_meta, st_meta = _ssd_chunks(xbcs, dts, xbcs, zero_state, *consts, l, 1, 1, 1, meta_block, meta_block,
                                          True, F32)
            y, new_st = _ssd_sample(xbcs, dts, state_conv[l].reshape(n_seq * (CONV_W - 1), CONV_DIM), st_all,
                                    *consts, n_new, l, new_st)
            y = jnp.concatenate([y, y_meta], axis=0)

            z, xbc, dt = _ssm_in(xp, nw, i, w_ssm_in_b, l, BF16, ROW_TILE)
            yp, st = _ssd_chunks(xbc, dt, xbcs, st_meta, *consts, l, n_batch, n_blocks, SSD_SUB_CHUNKS, 0,
                                 meta_block, False, BF16)
            xp = _ssm_out(xp, yp, z, snw, w_ssm_out_b, l, ROW_TILE)
            xs = _ssm_out(xs, y, zs, snw, w_ssm_out_b, l, small_rows)
            cp_l.append(xbc.reshape(n_batch, seq, CONV_DIM)[:, -(CONV_W - 1):])
            sp_l.append(st.reshape(n_batch, SSM_HEADS, SSM_HEAD_DIM, D_STATE))
            cs_l.append(xbcs[:n_rows].reshape(n_seq, n_new, CONV_DIM)[:, -(CONV_W - 1):])

    y_prompt = xp.reshape(n_batch, seq, D_MODEL)
    y_sample = xs[:n_rows].reshape(n_seq, n_new, D_MODEL)
    return (y_prompt, y_sample,
            jnp.stack(kp_l), jnp.stack(vp_l), jnp.stack(cp_l), jnp.stack(sp_l),
            new_ck.reshape(cache_k.shape), new_cv.reshape(cache_v.shape), jnp.stack(cs_l),
            new_st.reshape(state_ssm.shape))
```

```python
import functools
import math

import jax
import jax.numpy as jnp
from jax import lax
from jax.experimental import pallas as pl
from jax.experimental.pallas import tpu as pltpu

F32 = jnp.float32
BF16 = jnp.bfloat16

D_MODEL = 1024
DEPTH = 4
N_META = 16
WINDOW = 128
BLOCK = 128
HEAD_DIM = 128
ATTN_WIDTH = 2 * D_MODEL
N_HEADS = ATTN_WIDTH // HEAD_DIM
N_KV_HEADS = 4
GQA_GROUP = N_HEADS // N_KV_HEADS
KV_WIDTH = N_KV_HEADS * HEAD_DIM
D_INNER = 2 * D_MODEL
SSM_HEAD_DIM = 64
SSM_HEADS = D_INNER // SSM_HEAD_DIM
D_STATE = 128
SSM_GROUPS = 4
GROUP_WIDTH = D_INNER // SSM_GROUPS
PAIRS_PER_GROUP = SSM_HEADS // SSM_GROUPS // 2
CONV_W = 4
CONV_DIM = D_INNER + 2 * SSM_GROUPS * D_STATE
EPS = 1e-6
NEG = -1e30
LOG2E = math.log2(math.e)
QK_SCALE_LOG2 = HEAD_DIM ** -0.5 * LOG2E

LANES = 128
ROW_TILE = 512
SAMPLE_SEQS = 8
ATTN_SUB_BLOCKS = 8
SSD_SUB_CHUNKS = 4
CONV_HALO = 16
V7X_VMEM_BYTES = 64 * 1024 * 1024
VMEM_LIMIT = V7X_VMEM_BYTES * 7 // 8


def _dot(a, b):
    return jnp.dot(a, b, preferred_element_type=F32)


def _dot_nt(a, b):
    return lax.dot_general(a, b, (((1,), (1,)), ((), ())), preferred_element_type=F32)


def _rmsnorm(x, w):
    return x * lax.rsqrt(jnp.mean(x * x, axis=-1, keepdims=True) + EPS) * w


def _silu(x):
    return x * jax.nn.sigmoid(x)


def _softplus(x):
    return jnp.maximum(x, 0.0) + jnp.log1p(jnp.exp(-jnp.abs(x)))


def _split3(x):
    hi = x.astype(BF16)
    r = x - hi.astype(F32)
    mid = r.astype(BF16)
    lo = (r - mid.astype(F32)).astype(BF16)
    return hi, mid, lo


def _exact_left(p, x):
    hi, mid, lo = _split3(x)
    return _dot(p, hi) + _dot(p, mid) + _dot(p, lo)


def _exact_right(x, e):
    hi, mid, lo = _split3(x)
    return _dot(hi, e) + _dot(mid, e) + _dot(lo, e)


def _params(semantics):
    return pltpu.CompilerParams(dimension_semantics=semantics, vmem_limit_bytes=VMEM_LIMIT)


def _const_spec(shape):
    nd = len(shape)
    return pl.BlockSpec(shape, lambda *_: (0,) * nd)


def _row_spec(cols, tm):
    return pl.BlockSpec((tm, cols), lambda i: (i, 0))


def _layer_spec(w, layer):
    return pl.BlockSpec((None,) + w.shape[1:], lambda *_: (layer, 0, 0))


def _attn_in_body(x_ref, nw_ref, w_ref, qn_ref, kn_ref, q_ref, k_ref, v_ref, g_ref):
    h = _rmsnorm(x_ref[...], nw_ref[...]).astype(BF16)
    q = _dot(h, w_ref[:, :ATTN_WIDTH])
    for hd in range(N_HEADS):
        sl = slice(hd * HEAD_DIM, (hd + 1) * HEAD_DIM)
        q_ref[:, sl] = (_rmsnorm(q[:, sl], qn_ref[...]) * QK_SCALE_LOG2).astype(q_ref.dtype)
    k = _dot(h, w_ref[:, ATTN_WIDTH:ATTN_WIDTH + KV_WIDTH])
    for hd in range(N_KV_HEADS):
        sl = slice(hd * HEAD_DIM, (hd + 1) * HEAD_DIM)
        k_ref[:, sl] = _rmsnorm(k[:, sl], kn_ref[...])
    v_ref[...] = _dot(h, w_ref[:, ATTN_WIDTH + KV_WIDTH:ATTN_WIDTH + 2 * KV_WIDTH])
    g_ref[...] = _dot(h, w_ref[:, ATTN_WIDTH + 2 * KV_WIDTH:]).astype(g_ref.dtype)


def _attn_in(x, nw, depth, w, qn, kn, layer, act_dtype, tm):
    n = x.shape[0]
    return pl.pallas_call(
        _attn_in_body,
        grid=(n // tm,),
        in_specs=[_row_spec(D_MODEL, tm), _layer_spec(nw, depth), _layer_spec(w, layer),
                  _layer_spec(qn, layer), _layer_spec(kn, layer)],
        out_specs=[_row_spec(ATTN_WIDTH, tm), _row_spec(KV_WIDTH, tm), _row_spec(KV_WIDTH, tm),
                   _row_spec(ATTN_WIDTH, tm)],
        out_shape=[jax.ShapeDtypeStruct((n, ATTN_WIDTH), act_dtype),
                   jax.ShapeDtypeStruct((n, KV_WIDTH), F32),
                   jax.ShapeDtypeStruct((n, KV_WIDTH), F32),
                   jax.ShapeDtypeStruct((n, ATTN_WIDTH), act_dtype)],
        compiler_params=_params(("parallel",)),
        name="attn_in",
    )(x, nw, w, qn, kn)


def _attn_block_body(is_meta, n_sub, layer, sink_ref, q_ref, kc_ref, kp_ref, vc_ref, vp_ref, km_ref, vm_ref, o_ref):
    j = pl.program_id(1)
    rows = GQA_GROUP * BLOCK
    rowg = lax.broadcasted_iota(jnp.int32, (rows, 1), 0) // BLOCK
    row = lax.broadcasted_iota(jnp.int32, (rows, 2 * BLOCK), 0) & (BLOCK - 1)
    col = lax.broadcasted_iota(jnp.int32, (rows, 2 * BLOCK), 1)
    cur_ok = (col >= BLOCK) & (col - BLOCK <= row)

    def block(r0, kprev, vprev, n_prev):
        mask = cur_ok
        if n_prev:
            mask = mask | ((col < n_prev) & (col > row - (BLOCK - n_prev)))
        for kv in range(N_KV_HEADS):
            sl = slice(kv * HEAD_DIM, (kv + 1) * HEAD_DIM)
            keys = jnp.concatenate([kprev(sl), kc_ref[r0:r0 + BLOCK, sl]], axis=0).astype(BF16)
            vals = jnp.concatenate([vprev(sl), vc_ref[r0:r0 + BLOCK, sl]], axis=0).astype(BF16)
            base = kv * GQA_GROUP * HEAD_DIM
            q4 = jnp.concatenate(
                [q_ref[r0:r0 + BLOCK, base + g * HEAD_DIM:base + (g + 1) * HEAD_DIM] for g in range(GQA_GROUP)],
                axis=0).astype(BF16)
            sink = jnp.zeros((rows, 1), F32)
            for g in range(GQA_GROUP):
                sink = jnp.where(rowg == g, sink_ref[layer, kv * GQA_GROUP + g] * LOG2E, sink)
            s = jnp.where(mask, _dot_nt(q4, keys), NEG)
            m = jnp.maximum(jnp.max(s, axis=-1, keepdims=True), sink)
            p = jnp.exp2(s - m)
            denom = jnp.sum(p, axis=-1, keepdims=True) + jnp.exp2(sink - m)
            o = _dot(p.astype(BF16), vals) / denom
            for g in range(GQA_GROUP):
                o_ref[r0:r0 + BLOCK, base + g * HEAD_DIM:base + (g + 1) * HEAD_DIM] = (
                    o[g * BLOCK:(g + 1) * BLOCK].astype(o_ref.dtype))

    if is_meta:
        block(0, lambda sl: km_ref[:, sl], lambda sl: vm_ref[:, sl], 0)
    else:
        @pl.when(j == 0)
        def _():
            block(0, lambda sl: km_ref[:, sl], lambda sl: vm_ref[:, sl], N_META)

        @pl.when(j > 0)
        def _():
            block(0, lambda sl: kp_ref[:, sl], lambda sl: vp_ref[:, sl], BLOCK)

    for sub in range(1, n_sub):
        p0 = (sub - 1) * BLOCK
        block(sub * BLOCK, lambda sl, p0=p0: kc_ref[p0:p0 + BLOCK, sl], lambda sl, p0=p0: vc_ref[p0:p0 + BLOCK, sl],
              BLOCK)


def _attn_blocks(q, k, v, k_small, v_small, sinks, layer, n_batch, n_blocks, n_sub, first_block, meta_block,
                 is_meta, out_dtype):
    assert n_blocks % n_sub == 0 and first_block % n_sub == 0
    n_steps = n_blocks // n_sub
    rows = n_sub * BLOCK
    cur = lambda b, j: (first_block // n_sub + b * n_steps + j, 0)
    prev = lambda b, j: (jnp.maximum(first_block + (b * n_steps + j) * n_sub - 1, 0), 0)
    meta = lambda b, j: (meta_block, 0)
    out = lambda b, j: (b * n_steps + j, 0)
    return pl.pallas_call(
        functools.partial(_attn_block_body, is_meta, n_sub, layer),
        grid=(n_batch, n_steps),
        in_specs=[pl.BlockSpec(memory_space=pltpu.SMEM),
                  pl.BlockSpec((rows, ATTN_WIDTH), cur),
                  pl.BlockSpec((rows, KV_WIDTH), cur), pl.BlockSpec((BLOCK, KV_WIDTH), prev),
                  pl.BlockSpec((rows, KV_WIDTH), cur), pl.BlockSpec((BLOCK, KV_WIDTH), prev),
                  pl.BlockSpec((BLOCK, KV_WIDTH), meta), pl.BlockSpec((BLOCK, KV_WIDTH), meta)],
        out_specs=pl.BlockSpec((rows, ATTN_WIDTH), out),
        out_shape=jax.ShapeDtypeStruct((n_batch * n_blocks * BLOCK, ATTN_WIDTH), out_dtype),
        compiler_params=_params(("parallel", "parallel")),
        name="attn_meta" if is_meta else "attn_prompt",
    )(sinks, q, k, k, v, v, k_small, v_small)


def _attn_sample_body(fill_layers, layer, sink_ref, q_ref, k_ref, v_ref, ck_ref, cv_ref, *rest):
    o_ref, nk_ref, nv_ref = rest[-3:]
    if fill_layers:
        for other in range(fill_layers):
            if other != layer:
                nk_ref[other] = jnp.zeros(nk_ref.shape[1:], F32)
                nv_ref[other] = jnp.zeros(nv_ref.shape[1:], F32)
        nk_ref, nv_ref = nk_ref.at[layer], nv_ref.at[layer]
    n_seq, rows, _ = q_ref.shape
    n_cache = ck_ref.shape[1]
    n_fresh = k_ref.shape[1]
    wbuf = n_cache // N_KV_HEADS
    per_kv = rows // N_KV_HEADS
    row = lax.broadcasted_iota(jnp.int32, (rows, n_cache + n_fresh), 0)
    col = lax.broadcasted_iota(jnp.int32, (rows, n_cache + n_fresh), 1)
    qpos = (row % per_kv) // GQA_GROUP
    kpos = jnp.where(col < n_cache, col // N_KV_HEADS - wbuf, (col - n_cache) // N_KV_HEADS)
    diff = qpos - kpos
    mask = (col % N_KV_HEADS == row // per_kv) & (diff >= 0) & (diff < WINDOW)
    row1 = lax.broadcasted_iota(jnp.int32, (rows, 1), 0)
    sink = jnp.zeros((rows, 1), F32)
    for hd in range(N_HEADS):
        sink = jnp.where((row1 // per_kv == hd // GQA_GROUP) & (row1 % GQA_GROUP == hd % GQA_GROUP),
                         sink_ref[layer, hd] * LOG2E, sink)
    for s in range(n_seq):
        keys = jnp.concatenate([ck_ref[s], k_ref[s]], axis=0).astype(BF16)
        vals = jnp.concatenate([cv_ref[s], v_ref[s]], axis=0).astype(BF16)
        sc = _dot_nt(q_ref[s].astype(BF16), keys)
        sc = jnp.where(mask, sc, NEG)
        m = jnp.maximum(jnp.max(sc, axis=-1, keepdims=True), sink)
        p = jnp.exp2(sc - m)
        denom = jnp.sum(p, axis=-1, keepdims=True) + jnp.exp2(sink - m)
        o_ref[s] = _dot(p.astype(BF16), vals) / denom
        nk_ref[s, 0:n_cache - n_fresh, :] = ck_ref[s, n_fresh:, :]
        nk_ref[s, n_cache - n_fresh:, :] = k_ref[s]
        nv_ref[s, 0:n_cache - n_fresh, :] = cv_ref[s, n_fresh:, :]
        nv_ref[s, n_cache - n_fresh:, :] = v_ref[s]


def _attn_sample(q, k, v, ck, cv, sinks, layer, nk_prev=None, nv_prev=None):
    sb = SAMPLE_SEQS
    n_seq = q.shape[0]
    n_layers = ck.shape[0]
    blk = lambda a: pl.BlockSpec((sb,) + a.shape[1:], lambda i: (i,) + (0,) * (a.ndim - 1))
    cache_spec = pl.BlockSpec((None, sb) + ck.shape[2:], lambda i: (layer, i, 0, 0))
    in_specs = [pl.BlockSpec(memory_space=pltpu.SMEM), blk(q), blk(k), blk(v), cache_spec, cache_spec]
    args = [sinks, q, k, v, ck, cv]
    if nk_prev is None:
        aliases, fill_layers = {}, n_layers
        new_spec = pl.BlockSpec((n_layers, sb) + ck.shape[2:], lambda i: (0, i, 0, 0))
    else:
        in_specs += [pl.BlockSpec(memory_space=pl.ANY)] * 2
        aliases, fill_layers = {len(args): 1, len(args) + 1: 2}, None
        new_spec = cache_spec
        args += [nk_prev, nv_prev]
    return pl.pallas_call(
        functools.partial(_attn_sample_body, fill_layers, layer),
        grid=(n_seq // sb,),
        in_specs=in_specs,
        out_specs=[blk(q), new_spec, new_spec],
        out_shape=[jax.ShapeDtypeStruct(q.shape, F32),
                   jax.ShapeDtypeStruct(ck.shape, F32),
                   jax.ShapeDtypeStruct(cv.shape, F32)],
        input_output_aliases=aliases,
        compiler_params=_params(("parallel",)),
        name="attn_sample",
    )(*args)


def _attn_out_body(x_ref, o_ref, g_ref, w_ref, y_ref):
    a = o_ref[...].astype(F32) * _silu(g_ref[...].astype(F32))
    y_ref[...] = x_ref[...] + _dot(a.astype(BF16), w_ref[...].astype(BF16))


def _attn_out(x, o, g, w, layer, tm):
    n = x.shape[0]
    return pl.pallas_call(
        _attn_out_body,
        grid=(n // tm,),
        in_specs=[_row_spec(D_MODEL, tm), _row_spec(ATTN_WIDTH, tm), _row_spec(ATTN_WIDTH, tm),
                  _layer_spec(w, layer)],
        out_specs=_row_spec(D_MODEL, tm),
        out_shape=jax.ShapeDtypeStruct(x.shape, F32),
        compiler_params=_params(("parallel",)),
        name="attn_out",
    )(x, o, g, w)


def _ssm_in_body(x_ref, nw_ref, w_ref, z_ref, xbc_ref, dt_ref):
    h = _rmsnorm(x_ref[...], nw_ref[...]).astype(BF16)
    z_ref[...] = _dot(h, w_ref[:, :D_INNER]).astype(z_ref.dtype)
    xbc_ref[...] = _dot(h, w_ref[:, D_INNER:D_INNER + CONV_DIM])
    dt_ref[:, :SSM_HEADS] = _dot(h, w_ref[:, D_INNER + CONV_DIM:])
    dt_ref[:, SSM_HEADS:] = jnp.zeros((dt_ref.shape[0], LANES - SSM_HEADS), F32)


def _ssm_in(x, nw, depth, w, layer, act_dtype, tm):
    n = x.shape[0]
    return pl.pallas_call(
        _ssm_in_body,
        grid=(n // tm,),
        in_specs=[_row_spec(D_MODEL, tm), _layer_spec(nw, depth), _layer_spec(w, layer)],
        out_specs=[_row_spec(D_INNER, tm), _row_spec(CONV_DIM, tm), _row_spec(LANES, tm)],
        out_shape=[jax.ShapeDtypeStruct((n, D_INNER), act_dtype),
                   jax.ShapeDtypeStruct((n, CONV_DIM), F32),
                   jax.ShapeDtypeStruct((n, LANES), F32)],
        compiler_params=_params(("parallel",)),
        name="ssm_in",
    )(x, nw, w)


def _lane_bcast(x, h):
    return jnp.broadcast_to(x[:, h:h + 1], (x.shape[0], LANES))


def _row_bcast(x, h):
    return jnp.broadcast_to(x[h:h + 1, :], (LANES, x.shape[1]))


def _pair_blockdiag(xp):
    lane = lax.broadcasted_iota(jnp.int32, xp.shape, 1)
    xb = xp.astype(BF16)
    zero = jnp.zeros_like(xb)
    return jnp.concatenate([jnp.where(lane < SSM_HEAD_DIM, xb, zero),
                            jnp.where(lane >= SSM_HEAD_DIM, xb, zero)], axis=0)


def _pair_select(a0, a1):
    lane = lax.broadcasted_iota(jnp.int32, a0.shape, 1)
    return jnp.where(lane < SSM_HEAD_DIM, a0, a1)


def _intra_pair(cb, cmask, a2, a2_t, dt_t, pair):
    ms = []
    for hh in range(2):
        h = 2 * pair + hh
        seg = _lane_bcast(a2, h) - _row_bcast(a2_t, h)
        dec = jnp.exp2(jnp.where(cmask, seg, NEG))
        ms.append((cb * dec * _row_bcast(dt_t, h)).astype(BF16))
    return jnp.concatenate(ms, axis=1)


def _pair_out_scale(a2, pair):
    return jnp.exp2(_pair_select(_lane_bcast(a2, 2 * pair), _lane_bcast(a2, 2 * pair + 1)))


def _ssd_chunk_body(is_meta, n_sub, xbc_ref, dt_ref, mtile_ref, init_ref, cw_ref, cbias_ref, dtb_ref, alog_ref,
                    dsk_ref, e_ref, y_ref, st_out_ref, halo_ref, act_ref, st_ref):
    j = pl.program_id(1)
    n_steps = pl.num_programs(1)
    q = BLOCK
    halo = CONV_HALO

    @pl.when(j == 0)
    def _():
        if is_meta:
            halo_ref[...] = jnp.zeros((halo, CONV_DIM), F32)
            st_ref[...] = jnp.zeros(st_ref.shape, F32)
        else:
            halo_ref[...] = mtile_ref[0:halo, :]
            for pair in range(SSM_HEADS // 2):
                ps = slice(pair * LANES, (pair + 1) * LANES)
                st_ref[:, ps] = init_ref[0, ps, :].T

    def chunk(r0):
        n_sh = CONV_W - 1
        r = lax.broadcasted_iota(jnp.int32, (q, n_sh * q), 0)
        c = lax.broadcasted_iota(jnp.int32, (q, n_sh * q), 1)
        sel = (c % q == r - (c // q + 1)).astype(BF16)
        top = 8
        rh = lax.broadcasted_iota(jnp.int32, (top, n_sh * halo), 0)
        ch = lax.broadcasted_iota(jnp.int32, (top, n_sh * halo), 1)
        sel_halo = (ch % halo == halo + rh - (ch // halo + 1)).astype(BF16)

        def taps(x, cs):
            return jnp.concatenate([(cw_ref[CONV_W - 1 - k:CONV_W - k, cs] * x).astype(BF16)
                                    for k in range(1, CONV_W)], axis=0)

        width = 512
        for cc in range(CONV_DIM // width):
            cs = slice(cc * width, (cc + 1) * width)
            xc = xbc_ref[pl.ds(r0, q), cs]
            conv = cbias_ref[:, cs] + cw_ref[CONV_W - 1:CONV_W, cs] * xc + _dot(sel, taps(xc, cs))
            act_ref[:, cs] = _silu(conv)
            conv_top = conv[0:top, :] + _dot(sel_halo, taps(halo_ref[:, cs], cs))
            act_ref[0:top, cs] = _silu(conv_top)
        halo_ref[...] = xbc_ref[pl.ds(r0 + q - halo, halo), :]

        ri = lax.broadcasted_iota(jnp.int32, (q, q), 0)
        ci = lax.broadcasted_iota(jnp.int32, (q, q), 1)
        dt_ok = ci < SSM_HEADS
        if is_meta:
            dt_ok = dt_ok & (ri < N_META)
        dt = jnp.where(dt_ok, _softplus(dt_ref[pl.ds(r0, q), :] + dtb_ref[...]), 0.0)
        a = dt * (-jnp.exp(alog_ref[...]))
        cmask = ri >= ci
        a2 = _exact_left(cmask.astype(BF16), a) * LOG2E
        a2_t = a2.T
        dt_t = dt.T
        w_t = dt_t * jnp.exp2(jnp.broadcast_to(a2_t[:, q - 1:q], (q, q)) - a2_t)
        dec_rows = _exact_right(jnp.exp2(jnp.broadcast_to(a2[q - 1:q, :], (8, LANES))), e_ref[...])[0:1, :]
        rowi = lax.broadcasted_iota(jnp.int32, (q, LANES), 0)

        for g in range(SSM_GROUPS):
            gs = slice(D_INNER + g * D_STATE, D_INNER + (g + 1) * D_STATE)
            bg = act_ref[:, gs]
            cg = act_ref[:, SSM_GROUPS * D_STATE + gs.start:SSM_GROUPS * D_STATE + gs.stop].astype(BF16)
            cb = _dot_nt(cg, bg.astype(BF16))
            bg_t = bg.T
            y_off = _dot(cg, st_ref[:, g * GROUP_WIDTH:(g + 1) * GROUP_WIDTH].astype(BF16))
            for pr in range(PAIRS_PER_GROUP):
                pair = g * PAIRS_PER_GROUP + pr
                ps = slice(pair * LANES, (pair + 1) * LANES)
                xp = act_ref[:, ps]
                if is_meta:
                    xp = jnp.where(rowi < N_META, xp, 0.0)
                xbd = _pair_blockdiag(xp)
                y = _dot(_intra_pair(cb, cmask, a2, a2_t, dt_t, pair), xbd)
                st = st_ref[:, ps]
                y = y + y_off[:, pr * LANES:(pr + 1) * LANES] * _pair_out_scale(a2, pair)
                wn = jnp.concatenate([(bg_t * _row_bcast(w_t, 2 * pair)).astype(BF16),
                                      (bg_t * _row_bcast(w_t, 2 * pair + 1)).astype(BF16)], axis=1)
                st_ref[:, ps] = st * dec_rows[:, ps] + _dot(wn, xbd)
                y_ref[pl.ds(r0, q), ps] = (y + dsk_ref[:, ps] * xp).astype(y_ref.dtype)

    if n_sub == 1:
        chunk(0)
    else:
        def sub_chunk(sub, carry):
            chunk(pl.multiple_of(sub * q, q))
            return carry

        lax.fori_loop(0, n_sub, sub_chunk, 0)

    @pl.when(j == n_steps - 1)
    def _():
        for pair in range(SSM_HEADS // 2):
            ps = slice(pair * LANES, (pair + 1) * LANES)
            st_out_ref[0, ps, :] = st_ref[:, ps].T


def _ssd_chunks(xbc, dt, xbc_small, init, cw, cbias, dtb, alog, dsk, emat, layer, n_batch, n_chunks, n_sub,
                first_block, meta_block, is_meta, out_dtype):
    assert n_chunks % n_sub == 0 and first_block % n_sub == 0
    n_steps = n_chunks // n_sub
    rows = n_sub * BLOCK
    cur = lambda b, j: (first_block // n_sub + b * n_steps + j, 0)
    out = lambda b, j: (b * n_steps + j, 0)
    return pl.pallas_call(
        functools.partial(_ssd_chunk_body, is_meta, n_sub),
        grid=(n_batch, n_steps),
        in_specs=[pl.BlockSpec((rows, CONV_DIM), cur), pl.BlockSpec((rows, LANES), cur),
                  pl.BlockSpec((BLOCK, CONV_DIM), lambda b, j: (meta_block, 0)), _const_spec(init.shape),
                  _layer_spec(cw, layer), _layer_spec(cbias, layer), _layer_spec(dtb, layer),
                  _layer_spec(alog, layer), _layer_spec(dsk, layer), _const_spec(emat.shape)],
        out_specs=[pl.BlockSpec((rows, D_INNER), out),
                   pl.BlockSpec((1, D_INNER, D_STATE), lambda b, j: (b, 0, 0))],
        out_shape=[jax.ShapeDtypeStruct((n_batch * n_chunks * BLOCK, D_INNER), out_dtype),
                   jax.ShapeDtypeStruct((n_batch, D_INNER, D_STATE), F32)],
        scratch_shapes=[pltpu.VMEM((CONV_HALO, CONV_DIM), F32), pltpu.VMEM((BLOCK, CONV_DIM), F32),
                        pltpu.VMEM((D_STATE, D_INNER), F32)],
        compiler_params=_params(("parallel", "arbitrary")),
        name="ssd_meta" if is_meta else "ssd_prompt",
    )(xbc, dt, xbc_small, init, cw, cbias, dtb, alog, dsk, emat)


def _ssd_sample_body(*refs):
    nst_ref = refs[-2]

    last = pl.num_programs(1) - 1

    @pl.when(pl.program_id(1) == last)
    def _():
        _ssd_sample_step(*refs)

    @pl.when(pl.program_id(1) < last)
    def _():
        nst_ref[...] = jnp.zeros(nst_ref.shape, F32)


def _ssd_sample_step(xbc_ref, dt_ref, cs_ref, st_ref, cw_ref, cbias_ref, dtb_ref, alog_ref, dsk_ref, e_ref,
                     *rest):
    y_ref, nst_ref, yt_ref = rest[-3:]
    n_seq = st_ref.shape[0]
    n_new = xbc_ref.shape[0] // n_seq
    n_cs = CONV_W - 1
    qr = n_seq * n_new
    q = LANES
    kpad = 64
    assert qr + n_seq * n_cs <= kpad and n_new >= n_cs

    x_new = xbc_ref[...]
    xc = jnp.concatenate([x_new, cs_ref[...], jnp.zeros((kpad - qr - n_seq * n_cs, CONV_DIM), F32)], axis=0)
    r = lax.broadcasted_iota(jnp.int32, (qr, kpad), 0)
    c = lax.broadcasted_iota(jnp.int32, (qr, kpad), 1)
    s_of_r = r // n_new
    t_of_r = r % n_new
    conv = cbias_ref[...] + cw_ref[CONV_W - 1:CONV_W, :] * x_new
    for k in range(1, CONV_W):
        target = jnp.where(t_of_r >= k, r - k, qr + n_cs * s_of_r + t_of_r + n_cs - k)
        conv = conv + cw_ref[CONV_W - 1 - k:CONV_W - k, :] * _exact_left((c == target).astype(BF16), xc)
    act = jnp.concatenate([_silu(conv), jnp.zeros((q - qr, CONV_DIM), F32)], axis=0)
    xh = act[:, :D_INNER]
    bm = act[:, D_INNER:D_INNER + SSM_GROUPS * D_STATE]
    cm = act[:, D_INNER + SSM_GROUPS * D_STATE:]
    ri = lax.broadcasted_iota(jnp.int32, (q, q), 0)
    ci = lax.broadcasted_iota(jnp.int32, (q, q), 1)
    dt = jnp.concatenate([_softplus(dt_ref[...] + dtb_ref[...]), jnp.zeros((q - qr, LANES), F32)], axis=0)
    dt = jnp.where(ci < SSM_HEADS, dt, 0.0)
    a = dt * (-jnp.exp(alog_ref[...]))
    cmask = (ri >= ci) & (ri // n_new == ci // n_new) & (ri < qr)
    a_cum = _exact_left(cmask.astype(BF16), a)
    a2 = a_cum * LOG2E
    a2_t = a2.T
    dt_t = dt.T
    last = ((ci == (ri // n_new) * n_new + n_new - 1) & (ri < qr)).astype(BF16)
    a2_last = _exact_left(last, a_cum) * LOG2E
    w_exp = _exact_right(dt * jnp.exp2(a2_last - a2), e_ref[...])
    dec_exp = _exact_right(jnp.exp2(a2_last), e_ref[...])
    xw = xh * w_exp

    yt_ref[...] = jnp.zeros(yt_ref.shape, F32)
    colseq = ci // n_new
    rowseq = ri // n_new
    for g in range(SSM_GROUPS):
        gs = slice(g * D_STATE, (g + 1) * D_STATE)
        hs = slice(g * GROUP_WIDTH, (g + 1) * GROUP_WIDTH)
        cg_t = cm[:, gs].T.astype(BF16)
        bg = bm[:, gs].astype(BF16)
        xw_t = jnp.concatenate(
            [xw[:, g * GROUP_WIDTH + i * LANES:g * GROUP_WIDTH + (i + 1) * LANES].T
             for i in range(GROUP_WIDTH // LANES)], axis=0).astype(BF16)
        dec_t = jnp.concatenate(
            [dec_exp[:, g * GROUP_WIDTH + i * LANES:g * GROUP_WIDTH + (i + 1) * LANES].T
             for i in range(GROUP_WIDTH // LANES)], axis=0)
        zero = jnp.zeros((q, q), BF16)
        for s in range(n_seq):
            st0 = st_ref[s, hs, :]
            yt_ref[hs, :] += _dot(st0.astype(BF16), jnp.where(colseq == s, cg_t, zero))
            inc = _dot(xw_t, jnp.where(rowseq == s, bg, zero))
            dcol = jnp.broadcast_to(dec_t[:, s * n_new:s * n_new + 1], (GROUP_WIDTH, q))
            nst_ref[s, hs, :] = st0 * dcol + inc

    for g in range(SSM_GROUPS):
        gs = slice(g * D_STATE, (g + 1) * D_STATE)
        cb = _dot_nt(cm[:, gs].astype(BF16), bm[:, gs].astype(BF16))
        for pr in range(PAIRS_PER_GROUP):
            pair = g * PAIRS_PER_GROUP + pr
            ps = slice(pair * LANES, (pair + 1) * LANES)
            xp = xh[:, ps]
            y = _dot(_intra_pair(cb, cmask, a2, a2_t, dt_t, pair), _pair_blockdiag(xp))
            y = y + yt_ref[ps, :].T * _pair_out_scale(a2, pair) + dsk_ref[:, ps] * xp
            y_ref[:, ps] = y[0:qr, :]


def _ssd_sample(xbc, dt, cs, st, cw, cbias, dtb, alog, dsk, emat, n_new, layer, nst_prev=None):
    n_layers, n_seq = st.shape[0], st.shape[1]
    sb = SAMPLE_SEQS
    n_pass = n_layers if nst_prev is None else 1
    rows = lambda cols, per: pl.BlockSpec((sb * per, cols), lambda i, t: (i, 0))
    st_spec = pl.BlockSpec((None, sb, D_INNER, D_STATE), lambda i, t: (layer, i, 0, 0))
    nst_spec = pl.BlockSpec((None, sb, D_INNER, D_STATE), lambda i, t: ((layer + t + 1) % n_pass, i, 0, 0))
    if nst_prev is not None:
        nst_spec = st_spec
    in_specs = [rows(CONV_DIM, n_new), rows(LANES, n_new), rows(CONV_DIM, CONV_W - 1), st_spec,
                _layer_spec(cw, layer), _layer_spec(cbias, layer), _layer_spec(dtb, layer),
                _layer_spec(alog, layer), _layer_spec(dsk, layer), _const_spec(emat.shape)]
    args = [xbc, dt, cs, st, cw, cbias, dtb, alog, dsk, emat]
    aliases = {}
    if nst_prev is not None:
        in_specs.append(pl.BlockSpec(memory_space=pl.ANY))
        aliases = {len(args): 1}
        args.append(nst_prev)
    return pl.pallas_call(
        _ssd_sample_body,
        grid=(n_seq // sb, n_pass),
        in_specs=in_specs,
        out_specs=[rows(D_INNER, n_new), nst_spec],
        out_shape=[jax.ShapeDtypeStruct((n_seq * n_new, D_INNER), F32),
                   jax.ShapeDtypeStruct(st.shape, F32)],
        input_output_aliases=aliases,
        scratch_shapes=[pltpu.VMEM((D_INNER, LANES), F32)],
        compiler_params=_params(("parallel", "arbitrary")),
        name="ssd_sample",
    )(*args)


def _ssm_out_body(x_ref, y_ref, z_ref, nw_ref, w_ref, o_ref):
    y = y_ref[...].astype(F32) * _silu(z_ref[...].astype(F32))
    parts = []
    for g in range(SSM_GROUPS):
        yg = y[:, g * GROUP_WIDTH:(g + 1) * GROUP_WIDTH]
        parts.append(yg * lax.rsqrt(jnp.mean(yg * yg, axis=-1, keepdims=True) + EPS))
    yn = (jnp.concatenate(parts, axis=1) * nw_ref[...]).astype(BF16)
    o_ref[...] = x_ref[...] + _dot(yn, w_ref[...].astype(BF16))


def _ssm_out(x, y, z, nw, w, layer, tm):
    n = x.shape[0]
    return pl.pallas_call(
        _ssm_out_body,
        grid=(n // tm,),
        in_specs=[_row_spec(D_MODEL, tm), _row_spec(D_INNER, tm), _row_spec(D_INNER, tm),
                  _layer_spec(nw, layer), _layer_spec(w, layer)],
        out_specs=_row_spec(D_MODEL, tm),
        out_shape=jax.ShapeDtypeStruct(x.shape, F32),
        compiler_params=_params(("parallel",)),
        name="ssm_out",
    )(x, y, z, nw, w)


def _rows3(p, width=None):
    p = p.astype(F32)
    if width is not None:
        p = jnp.pad(p, ((0, 0), (0, width - p.shape[1])))
    return p[:, None, :]


def kernel(x_prompt, x_sample, cache_k, cache_v, state_conv, state_ssm, meta_tokens, norm_w,
           w_attn_in, q_norm_w, k_norm_w, attn_sinks, w_attn_out, w_ssm_in, conv_w, conv_b,
           dt_bias, a_log, d_skip, ssm_norm_w, w_ssm_out):
    n_batch, seq, _ = x_prompt.shape
    n_seq, n_new, _ = x_sample.shape
    wbuf = cache_k.shape[2]
    n_blocks = seq // BLOCK
    n_rows = n_seq * n_new
    assert seq % BLOCK == 0 and wbuf == WINDOW and n_rows % BLOCK == 0 and n_new == 4
    meta_block = n_rows // BLOCK
    small_rows = n_rows + BLOCK

    xp = x_prompt.reshape(n_batch * seq, D_MODEL)
    xs = jnp.concatenate([x_sample.reshape(n_rows, D_MODEL), meta_tokens.astype(F32),
                          jnp.zeros((BLOCK - N_META, D_MODEL), F32)], axis=0)

    emat = jnp.pad(jnp.repeat(jnp.eye(SSM_HEADS, dtype=BF16), SSM_HEAD_DIM, axis=1),
                   ((0, LANES - SSM_HEADS), (0, 0)))
    zero_state = jnp.zeros((1, D_INNER, D_STATE), F32)

    w_attn_in_b, w_ssm_in_b = w_attn_in.astype(BF16), w_ssm_in.astype(BF16)
    w_attn_out_b, w_ssm_out_b = w_attn_out, w_ssm_out
    n_attn, n_ssm = cache_k.shape[0], state_ssm.shape[0]
    ck_all = cache_k.reshape(n_attn, n_seq, wbuf * N_KV_HEADS, HEAD_DIM)
    cv_all = cache_v.reshape(n_attn, n_seq, wbuf * N_KV_HEADS, HEAD_DIM)
    st_all = state_ssm.reshape(n_ssm, n_seq, D_INNER, D_STATE)
    new_ck = new_cv = new_st = None

    nw, qn, kn, sinks = _rows3(norm_w), _rows3(q_norm_w), _rows3(k_norm_w), attn_sinks.astype(F32)
    snw = _rows3(ssm_norm_w)
    consts = (conv_w.astype(F32), _rows3(conv_b), _rows3(dt_bias, LANES), _rows3(a_log, LANES),
              _rows3(jnp.repeat(d_skip, SSM_HEAD_DIM, axis=1)), emat)

    kp_l, vp_l, cp_l, sp_l, cs_l = [], [], [], [], []
    for i in range(DEPTH):
        l = i // 2
        if i % 2 == 0:
            qs, ks, vs, gs = _attn_in(xs, nw, i, w_attn_in_b, qn, kn, l, F32, small_rows)
            o_meta = _attn_blocks(qs, ks, vs, ks, vs, sinks, l, 1, 1, 1, meta_block, meta_block, True, F32)
            qq = qs[:n_rows].reshape(n_seq, n_new, N_KV_HEADS, GQA_GROUP * HEAD_DIM).transpose(0, 2, 1, 3)
            o, new_ck, new_cv = _attn_sample(qq.reshape(n_seq, N_HEADS * n_new, HEAD_DIM),
                                             ks[:n_rows].reshape(n_seq, n_new * N_KV_HEADS, HEAD_DIM),
                                             vs[:n_rows].reshape(n_seq, n_new * N_KV_HEADS, HEAD_DIM),
                                             ck_all, cv_all, sinks, l, new_ck, new_cv)
            o = o.reshape(n_seq, N_KV_HEADS, n_new, GQA_GROUP * HEAD_DIM).transpose(0, 2, 1, 3)
            o = jnp.concatenate([o.reshape(n_rows, ATTN_WIDTH), o_meta], axis=0)

            q, k, v, g = _attn_in(xp, nw, i, w_attn_in_b, qn, kn, l, BF16, ROW_TILE)
            op = _attn_blocks(q, k, v, ks, vs, sinks, l, n_batch, n_blocks, ATTN_SUB_BLOCKS, 0, meta_block, False,
                              BF16)
            xp = _attn_out(xp, op, g, w_attn_out_b, l, ROW_TILE)
            xs = _attn_out(xs, o, gs, w_attn_out_b, l, small_rows)
            kp_l.append(k.reshape(n_batch, seq, KV_WIDTH)[:, -WINDOW:].reshape(n_batch, WINDOW, N_KV_HEADS, HEAD_DIM))
            vp_l.append(v.reshape(n_batch, seq, KV_WIDTH)[:, -WINDOW:].reshape(n_batch, WINDOW, N_KV_HEADS, HEAD_DIM))
        else:
            zs, xbcs, dts = _ssm_in(xs, nw, i, w_ssm_in_b, l, F32, small_rows)
            y_meta, st_meta = _ssd_chunks(xbcs, dts, xbcs, zero_state, *consts, l, 1, 1, 1, meta_block, meta_block,
                                          True, F32)
            y, new_st = _ssd_sample(xbcs, dts, state_conv[l].reshape(n_seq * (CONV_W - 1), CONV_DIM), st_all,
                                    *consts, n_new, l, new_st)
            y = jnp.concatenate([y, y_meta], axis=0)

            z, xbc, dt = _ssm_in(xp, nw, i, w_ssm_in_b, l, BF16, ROW_TILE)
            yp, st = _ssd_chunks(xbc, dt, xbcs, st_meta, *consts, l, n_batch, n_blocks, SSD_SUB_CHUNKS, 0,
                                 meta_block, False, BF16)
            xp = _ssm_out(xp, yp, z, snw, w_ssm_out_b, l, ROW_TILE)
            xs = _ssm_out(xs, y, zs, snw, w_ssm_out_b, l, small_rows)
            cp_l.append(xbc.reshape(n_batch, seq, CONV_DIM)[:, -(CONV_W - 1):])
            sp_l.append(st.reshape(n_batch, SSM_HEADS, SSM_HEAD_DIM, D_STATE))
            cs_l.append(xbcs[:n_rows].reshape(n_seq, n_new, CONV_DIM)[:, -(CONV_W - 1):])

    y_prompt = xp.reshape(n_batch, seq, D_MODEL)
    y_sample = xs[:n_rows].reshape(n_seq, n_new, D_MODEL)
    return (y_prompt, y_sample,
            jnp.stack(kp_l), jnp.stack(vp_l), jnp.stack(cp_l), jnp.stack(sp_l),
            new_ck.reshape(cache_k.shape), new_cv.reshape(cache_v.shape), jnp.stack(cs_l),
            new_st.reshape(state_ssm.shape))
```

```python
import functools
import math

import jax
import jax.numpy as jnp
from jax import lax
from jax.experimental import pallas as pl
from jax.experimental.pallas import tpu as pltpu

F32 = jnp.float32
BF16 = jnp.bfloat16

D_MODEL = 1024
DEPTH = 4
N_META = 16
WINDOW = 128
BLOCK = 128
HEAD_DIM = 128
ATTN_WIDTH = 2 * D_MODEL
N_HEADS = ATTN_WIDTH // HEAD_DIM
N_KV_HEADS = 4
GQA_GROUP = N_HEADS // N_KV_HEADS
KV_WIDTH = N_KV_HEADS * HEAD_DIM
D_INNER = 2 * D_MODEL
SSM_HEAD_DIM = 64
SSM_HEADS = D_INNER // SSM_HEAD_DIM
D_STATE = 128
SSM_GROUPS = 4
GROUP_WIDTH = D_INNER // SSM_GROUPS
PAIRS_PER_GROUP = SSM_HEADS // SSM_GROUPS // 2
CONV_W = 4
CONV_DIM = D_INNER + 2 * SSM_GROUPS * D_STATE
EPS = 1e-6
NEG = -1e30
LOG2E = math.log2(math.e)
QK_SCALE_LOG2 = HEAD_DIM ** -0.5 * LOG2E

LANES = 128
ROW_TILE = 512
SAMPLE_SEQS = 8
ATTN_SUB_BLOCKS = 4
SSD_SUB_CHUNKS = 4
CONV_HALO = 16
V7X_VMEM_BYTES = 64 * 1024 * 1024
VMEM_LIMIT = V7X_VMEM_BYTES * 7 // 8


def _dot(a, b):
    return jnp.dot(a, b, preferred_element_type=F32)


def _dot_nt(a, b):
    return lax.dot_general(a, b, (((1,), (1,)), ((), ())), preferred_element_type=F32)


def _rmsnorm(x, w):
    return x * lax.rsqrt(jnp.mean(x * x, axis=-1, keepdims=True) + EPS) * w


def _silu(x):
    return x * jax.nn.sigmoid(x)


def _softplus(x):
    return jnp.maximum(x, 0.0) + jnp.log1p(jnp.exp(-jnp.abs(x)))


def _split3(x):
    hi = x.astype(BF16)
    r = x - hi.astype(F32)
    mid = r.astype(BF16)
    lo = (r - mid.astype(F32)).astype(BF16)
    return hi, mid, lo


def _exact_left(p, x):
    hi, mid, lo = _split3(x)
    return _dot(p, hi) + _dot(p, mid) + _dot(p, lo)


def _exact_right(x, e):
    hi, mid, lo = _split3(x)
    return _dot(hi, e) + _dot(mid, e) + _dot(lo, e)


def _params(semantics):
    return pltpu.CompilerParams(dimension_semantics=semantics, vmem_limit_bytes=VMEM_LIMIT)


def _const_spec(shape):
    nd = len(shape)
    return pl.BlockSpec(shape, lambda *_: (0,) * nd)


def _row_spec(cols, tm):
    return pl.BlockSpec((tm, cols), lambda i: (i, 0))


def _layer_spec(w, layer):
    return pl.BlockSpec((None,) + w.shape[1:], lambda *_: (layer, 0, 0))


def _attn_in_body(x_ref, nw_ref, w_ref, qn_ref, kn_ref, q_ref, k_ref, v_ref, g_ref):
    h = _rmsnorm(x_ref[...], nw_ref[...]).astype(BF16)
    q = _dot(h, w_ref[:, :ATTN_WIDTH])
    for hd in range(N_HEADS):
        sl = slice(hd * HEAD_DIM, (hd + 1) * HEAD_DIM)
        q_ref[:, sl] = (_rmsnorm(q[:, sl], qn_ref[...]) * QK_SCALE_LOG2).astype(q_ref.dtype)
    k = _dot(h, w_ref[:, ATTN_WIDTH:ATTN_WIDTH + KV_WIDTH])
    for hd in range(N_KV_HEADS):
        sl = slice(hd * HEAD_DIM, (hd + 1) * HEAD_DIM)
        k_ref[:, sl] = _rmsnorm(k[:, sl], kn_ref[...])
    v_ref[...] = _dot(h, w_ref[:, ATTN_WIDTH + KV_WIDTH:ATTN_WIDTH + 2 * KV_WIDTH])
    g_ref[...] = _dot(h, w_ref[:, ATTN_WIDTH + 2 * KV_WIDTH:]).astype(g_ref.dtype)


def _attn_in(x, nw, depth, w, qn, kn, layer, act_dtype, tm):
    n = x.shape[0]
    return pl.pallas_call(
        _attn_in_body,
        grid=(n // tm,),
        in_specs=[_row_spec(D_MODEL, tm), _layer_spec(nw, depth), _layer_spec(w, layer),
                  _layer_spec(qn, layer), _layer_spec(kn, layer)],
        out_specs=[_row_spec(ATTN_WIDTH, tm), _row_spec(KV_WIDTH, tm), _row_spec(KV_WIDTH, tm),
                   _row_spec(ATTN_WIDTH, tm)],
        out_shape=[jax.ShapeDtypeStruct((n, ATTN_WIDTH), act_dtype),
                   jax.ShapeDtypeStruct((n, KV_WIDTH), F32),
                   jax.ShapeDtypeStruct((n, KV_WIDTH), F32),
                   jax.ShapeDtypeStruct((n, ATTN_WIDTH), act_dtype)],
        compiler_params=_params(("parallel",)),
        name="attn_in",
    )(x, nw, w, qn, kn)


def _attn_block_body(is_meta, n_sub, layer, fused, sink_ref, q_ref, kc_ref, kp_ref, vc_ref, vp_ref, km_ref, vm_ref,
                     *rest):
    if fused:
        g_ref, x_ref, w_ref, xo_ref, a_ref = rest
    else:
        o_ref, = rest
    j = pl.program_id(1)
    rows = GQA_GROUP * BLOCK
    rowg = lax.broadcasted_iota(jnp.int32, (rows, 1), 0) // BLOCK
    row = lax.broadcasted_iota(jnp.int32, (rows, 2 * BLOCK), 0) & (BLOCK - 1)
    col = lax.broadcasted_iota(jnp.int32, (rows, 2 * BLOCK), 1)
    cur_ok = (col >= BLOCK) & (col - BLOCK <= row)

    def block(r0, kprev, vprev, n_prev):
        mask = cur_ok
        if n_prev:
            mask = mask | ((col < n_prev) & (col > row - (BLOCK - n_prev)))
        for kv in range(N_KV_HEADS):
            sl = slice(kv * HEAD_DIM, (kv + 1) * HEAD_DIM)
            keys = jnp.concatenate([kprev(sl), kc_ref[r0:r0 + BLOCK, sl]], axis=0).astype(BF16)
            vals = jnp.concatenate([vprev(sl), vc_ref[r0:r0 + BLOCK, sl]], axis=0).astype(BF16)
            base = kv * GQA_GROUP * HEAD_DIM
            q4 = jnp.concatenate(
                [q_ref[r0:r0 + BLOCK, base + g * HEAD_DIM:base + (g + 1) * HEAD_DIM] for g in range(GQA_GROUP)],
                axis=0).astype(BF16)
            sink = jnp.zeros((rows, 1), F32)
            for g in range(GQA_GROUP):
                sink = jnp.where(rowg == g, sink_ref[layer, kv * GQA_GROUP + g] * LOG2E, sink)
            s = jnp.where(mask, _dot_nt(q4, keys), NEG)
            m = jnp.maximum(jnp.max(s, axis=-1, keepdims=True), sink)
            p = jnp.exp2(s - m)
            denom = jnp.sum(p, axis=-1, keepdims=True) + jnp.exp2(sink - m)
            o = _dot(p.astype(BF16), vals) / denom
            for g in range(GQA_GROUP):
                cols = slice(base + g * HEAD_DIM, base + (g + 1) * HEAD_DIM)
                og = o[g * BLOCK:(g + 1) * BLOCK]
                if fused:
                    a_ref[r0:r0 + BLOCK, cols] = (og * _silu(g_ref[r0:r0 + BLOCK, cols].astype(F32))).astype(BF16)
                else:
                    o_ref[r0:r0 + BLOCK, cols] = og.astype(o_ref.dtype)

    if is_meta:
        block(0, lambda sl: km_ref[:, sl], lambda sl: vm_ref[:, sl], 0)
    else:
        @pl.when(j == 0)
        def _():
            block(0, lambda sl: km_ref[:, sl], lambda sl: vm_ref[:, sl], N_META)

        @pl.when(j > 0)
        def _():
            block(0, lambda sl: kp_ref[:, sl], lambda sl: vp_ref[:, sl], BLOCK)

    for sub in range(1, n_sub):
        p0 = (sub - 1) * BLOCK
        block(sub * BLOCK, lambda sl, p0=p0: kc_ref[p0:p0 + BLOCK, sl], lambda sl, p0=p0: vc_ref[p0:p0 + BLOCK, sl],
              BLOCK)

    if fused:
        xo_ref[...] = x_ref[...] + _dot(a_ref[...], w_ref[...].astype(BF16))


def _attn_blocks(q, k, v, k_small, v_small, sinks, layer, n_batch, n_blocks, n_sub, first_block, meta_block,
                 is_meta, out_dtype, fused=None):
    assert n_blocks % n_sub == 0 and first_block % n_sub == 0
    n_steps = n_blocks // n_sub
    rows = n_sub * BLOCK
    cur = lambda b, j: (first_block // n_sub + b * n_steps + j, 0)
    prev = lambda b, j: (jnp.maximum(first_block + (b * n_steps + j) * n_sub - 1, 0), 0)
    meta = lambda b, j: (meta_block, 0)
    out = lambda b, j: (b * n_steps + j, 0)
    in_specs = [pl.BlockSpec(memory_space=pltpu.SMEM),
                pl.BlockSpec((rows, ATTN_WIDTH), cur),
                pl.BlockSpec((rows, KV_WIDTH), cur), pl.BlockSpec((BLOCK, KV_WIDTH), prev),
                pl.BlockSpec((rows, KV_WIDTH), cur), pl.BlockSpec((BLOCK, KV_WIDTH), prev),
                pl.BlockSpec((BLOCK, KV_WIDTH), meta), pl.BlockSpec((BLOCK, KV_WIDTH), meta)]
    args = [sinks, q, k, k, v, v, k_small, v_small]
    out_width, scratch = ATTN_WIDTH, []
    if fused:
        gate, x, w_out = fused
        assert first_block == 0
        in_specs += [pl.BlockSpec((rows, ATTN_WIDTH), cur), pl.BlockSpec((rows, D_MODEL), cur),
                     _layer_spec(w_out, layer)]
        args += [gate, x, w_out]
        out_width, scratch = D_MODEL, [pltpu.VMEM((rows, ATTN_WIDTH), BF16)]
    return pl.pallas_call(
        functools.partial(_attn_block_body, is_meta, n_sub, layer, bool(fused)),
        grid=(n_batch, n_steps),
        in_specs=in_specs,
        out_specs=pl.BlockSpec((rows, out_width), out),
        out_shape=jax.ShapeDtypeStruct((n_batch * n_blocks * BLOCK, out_width), out_dtype),
        scratch_shapes=scratch,
        compiler_params=_params(("parallel", "parallel")),
        name="attn_meta" if is_meta else "attn_prompt",
    )(*args)


def _attn_sample_body(fill_layers, layer, sink_ref, q_ref, k_ref, v_ref, ck_ref, cv_ref, *rest):
    o_ref, nk_ref, nv_ref = rest[-3:]
    if fill_layers:
        for other in range(fill_layers):
            if other != layer:
                nk_ref[other] = jnp.zeros(nk_ref.shape[1:], F32)
                nv_ref[other] = jnp.zeros(nv_ref.shape[1:], F32)
        nk_ref, nv_ref = nk_ref.at[layer], nv_ref.at[layer]
    n_seq, rows, _ = q_ref.shape
    n_cache = ck_ref.shape[1]
    n_fresh = k_ref.shape[1]
    wbuf = n_cache // N_KV_HEADS
    per_kv = rows // N_KV_HEADS
    row = lax.broadcasted_iota(jnp.int32, (rows, n_cache + n_fresh), 0)
    col = lax.broadcasted_iota(jnp.int32, (rows, n_cache + n_fresh), 1)
    qpos = (row % per_kv) // GQA_GROUP
    kpos = jnp.where(col < n_cache, col // N_KV_HEADS - wbuf, (col - n_cache) // N_KV_HEADS)
    diff = qpos - kpos
    mask = (col % N_KV_HEADS == row // per_kv) & (diff >= 0) & (diff < WINDOW)
    row1 = lax.broadcasted_iota(jnp.int32, (rows, 1), 0)
    sink = jnp.zeros((rows, 1), F32)
    for hd in range(N_HEADS):
        sink = jnp.where((row1 // per_kv == hd // GQA_GROUP) & (row1 % GQA_GROUP == hd % GQA_GROUP),
                         sink_ref[layer, hd] * LOG2E, sink)
    for s in range(n_seq):
        keys = jnp.concatenate([ck_ref[s], k_ref[s]], axis=0).astype(BF16)
        vals = jnp.concatenate([cv_ref[s], v_ref[s]], axis=0).astype(BF16)
        sc = _dot_nt(q_ref[s].astype(BF16), keys)
        sc = jnp.where(mask, sc, NEG)
        m = jnp.maximum(jnp.max(sc, axis=-1, keepdims=True), sink)
        p = jnp.exp2(sc - m)
        denom = jnp.sum(p, axis=-1, keepdims=True) + jnp.exp2(sink - m)
        o_ref[s] = _dot(p.astype(BF16), vals) / denom
        nk_ref[s, 0:n_cache - n_fresh, :] = ck_ref[s, n_fresh:, :]
        nk_ref[s, n_cache - n_fresh:, :] = k_ref[s]
        nv_ref[s, 0:n_cache - n_fresh, :] = cv_ref[s, n_fresh:, :]
        nv_ref[s, n_cache - n_fresh:, :] = v_ref[s]


def _attn_sample(q, k, v, ck, cv, sinks, layer, nk_prev=None, nv_prev=None):
    sb = SAMPLE_SEQS
    n_seq = q.shape[0]
    n_layers = ck.shape[0]
    blk = lambda a: pl.BlockSpec((sb,) + a.shape[1:], lambda i: (i,) + (0,) * (a.ndim - 1))
    cache_spec = pl.BlockSpec((None, sb) + ck.shape[2:], lambda i: (layer, i, 0, 0))
    in_specs = [pl.BlockSpec(memory_space=pltpu.SMEM), blk(q), blk(k), blk(v), cache_spec, cache_spec]
    args = [sinks, q, k, v, ck, cv]
    if nk_prev is None:
        aliases, fill_layers = {}, n_layers
        new_spec = pl.BlockSpec((n_layers, sb) + ck.shape[2:], lambda i: (0, i, 0, 0))
    else:
        in_specs += [pl.BlockSpec(memory_space=pl.ANY)] * 2
        aliases, fill_layers = {len(args): 1, len(args) + 1: 2}, None
        new_spec = cache_spec
        args += [nk_prev, nv_prev]
    return pl.pallas_call(
        functools.partial(_attn_sample_body, fill_layers, layer),
        grid=(n_seq // sb,),
        in_specs=in_specs,
        out_specs=[blk(q), new_spec, new_spec],
        out_shape=[jax.ShapeDtypeStruct(q.shape, F32),
                   jax.ShapeDtypeStruct(ck.shape, F32),
                   jax.ShapeDtypeStruct(cv.shape, F32)],
        input_output_aliases=aliases,
        compiler_params=_params(("parallel",)),
        name="attn_sample",
    )(*args)


def _attn_out_body(x_ref, o_ref, g_ref, w_ref, y_ref):
    a = o_ref[...].astype(F32) * _silu(g_ref[...].astype(F32))
    y_ref[...] = x_ref[...] + _dot(a.astype(BF16), w_ref[...].astype(BF16))


def _attn_out(x, o, g, w, layer, tm):
    n = x.shape[0]
    return pl.pallas_call(
        _attn_out_body,
        grid=(n // tm,),
        in_specs=[_row_spec(D_MODEL, tm), _row_spec(ATTN_WIDTH, tm), _row_spec(ATTN_WIDTH, tm),
                  _layer_spec(w, layer)],
        out_specs=_row_spec(D_MODEL, tm),
        out_shape=jax.ShapeDtypeStruct(x.shape, F32),
        compiler_params=_params(("parallel",)),
        name="attn_out",
    )(x, o, g, w)


def _ssm_in_body(x_ref, nw_ref, w_ref, z_ref, xbc_ref, dt_ref):
    h = _rmsnorm(x_ref[...], nw_ref[...]).astype(BF16)
    z_ref[...] = _dot(h, w_ref[:, :D_INNER]).astype(z_ref.dtype)
    xbc_ref[...] = _dot(h, w_ref[:, D_INNER:D_INNER + CONV_DIM])
    dt_ref[:, :SSM_HEADS] = _dot(h, w_ref[:, D_INNER + CONV_DIM:])
    dt_ref[:, SSM_HEADS:] = jnp.zeros((dt_ref.shape[0], LANES - SSM_HEADS), F32)


def _ssm_in(x, nw, depth, w, layer, act_dtype, tm):
    n = x.shape[0]
    return pl.pallas_call(
        _ssm_in_body,
        grid=(n // tm,),
        in_specs=[_row_spec(D_MODEL, tm), _layer_spec(nw, depth), _layer_spec(w, layer)],
        out_specs=[_row_spec(D_INNER, tm), _row_spec(CONV_DIM, tm), _row_spec(LANES, tm)],
        out_shape=[jax.ShapeDtypeStruct((n, D_INNER), act_dtype),
                   jax.ShapeDtypeStruct((n, CONV_DIM), F32),
                   jax.ShapeDtypeStruct((n, LANES), F32)],
        compiler_params=_params(("parallel",)),
        name="ssm_in",
    )(x, nw, w)


def _lane_bcast(x, h):
    return jnp.broadcast_to(x[:, h:h + 1], (x.shape[0], LANES))


def _row_bcast(x, h):
    return jnp.broadcast_to(x[h:h + 1, :], (LANES, x.shape[1]))


def _pair_blockdiag(xp):
    lane = lax.broadcasted_iota(jnp.int32, xp.shape, 1)
    xb = xp.astype(BF16)
    zero = jnp.zeros_like(xb)
    return jnp.concatenate([jnp.where(lane < SSM_HEAD_DIM, xb, zero),
                            jnp.where(lane >= SSM_HEAD_DIM, xb, zero)], axis=0)


def _pair_select(a0, a1):
    lane = lax.broadcasted_iota(jnp.int32, a0.shape, 1)
    return jnp.where(lane < SSM_HEAD_DIM, a0, a1)


def _intra_pair(cb, cmask, a2, a2_t, dt_t, pair):
    ms = []
    for hh in range(2):
        h = 2 * pair + hh
        seg = _lane_bcast(a2, h) - _row_bcast(a2_t, h)
        dec = jnp.exp2(jnp.where(cmask, seg, NEG))
        ms.append((cb * dec * _row_bcast(dt_t, h)).astype(BF16))
    return jnp.concatenate(ms, axis=1)


def _pair_out_scale(a2, pair):
    return jnp.exp2(_pair_select(_lane_bcast(a2, 2 * pair), _lane_bcast(a2, 2 * pair + 1)))


def _ssd_chunk_body(is_meta, n_sub, xbc_ref, dt_ref, mtile_ref, init_ref, cw_ref, cbias_ref, dtb_ref, alog_ref,
                    dsk_ref, e_ref, y_ref, st_out_ref, halo_ref, act_ref, st_ref):
    j = pl.program_id(1)
    n_steps = pl.num_programs(1)
    q = BLOCK
    halo = CONV_HALO

    @pl.when(j == 0)
    def _():
        if is_meta:
            halo_ref[...] = jnp.zeros((halo, CONV_DIM), F32)
            st_ref[...] = jnp.zeros(st_ref.shape, F32)
        else:
            halo_ref[...] = mtile_ref[0:halo, :]
            for pair in range(SSM_HEADS // 2):
                ps = slice(pair * LANES, (pair + 1) * LANES)
                st_ref[:, ps] = init_ref[0, ps, :].T

    def chunk(r0):
        n_sh = CONV_W - 1
        r = lax.broadcasted_iota(jnp.int32, (q, n_sh * q), 0)
        c = lax.broadcasted_iota(jnp.int32, (q, n_sh * q), 1)
        sel = (c % q == r - (c // q + 1)).astype(BF16)
        top = 8
        rh = lax.broadcasted_iota(jnp.int32, (top, n_sh * halo), 0)
        ch = lax.broadcasted_iota(jnp.int32, (top, n_sh * halo), 1)
        sel_halo = (ch % halo == halo + rh - (ch // halo + 1)).astype(BF16)

        def taps(x, cs):
            return jnp.concatenate([(cw_ref[CONV_W - 1 - k:CONV_W - k, cs] * x).astype(BF16)
                                    for k in range(1, CONV_W)], axis=0)

        width = 512
        for cc in range(CONV_DIM // width):
            cs = slice(cc * width, (cc + 1) * width)
            xc = xbc_ref[pl.ds(r0, q), cs]
            conv = cbias_ref[:, cs] + cw_ref[CONV_W - 1:CONV_W, cs] * xc + _dot(sel, taps(xc, cs))
            act_ref[:, cs] = _silu(conv)
            conv_top = conv[0:top, :] + _dot(sel_halo, taps(halo_ref[:, cs], cs))
            act_ref[0:top, cs] = _silu(conv_top)
        halo_ref[...] = xbc_ref[pl.ds(r0 + q - halo, halo), :]

        ri = lax.broadcasted_iota(jnp.int32, (q, q), 0)
        ci = lax.broadcasted_iota(jnp.int32, (q, q), 1)
        dt_ok = ci < SSM_HEADS
        if is_meta:
            dt_ok = dt_ok & (ri < N_META)
        dt = jnp.where(dt_ok, _softplus(dt_ref[pl.ds(r0, q), :] + dtb_ref[...]), 0.0)
        a = dt * (-jnp.exp(alog_ref[...]))
        cmask = ri >= ci
        a2 = _exact_left(cmask.astype(BF16), a) * LOG2E
        a2_t = a2.T
        dt_t = dt.T
        w_t = dt_t * jnp.exp2(jnp.broadcast_to(a2_t[:, q - 1:q], (q, q)) - a2_t)
        dec_rows = _exact_right(jnp.exp2(jnp.broadcast_to(a2[q - 1:q, :], (8, LANES))), e_ref[...])[0:1, :]
        rowi = lax.broadcasted_iota(jnp.int32, (q, LANES), 0)

        for g in range(SSM_GROUPS):
            gs = slice(D_INNER + g * D_STATE, D_INNER + (g + 1) * D_STATE)
            bg = act_ref[:, gs]
            cg = act_ref[:, SSM_GROUPS * D_STATE + gs.start:SSM_GROUPS * D_STATE + gs.stop].astype(BF16)
            cb = _dot_nt(cg, bg.astype(BF16))
            bg_t = bg.T
            y_off = _dot(cg, st_ref[:, g * GROUP_WIDTH:(g + 1) * GROUP_WIDTH].astype(BF16))
            for pr in range(PAIRS_PER_GROUP):
                pair = g * PAIRS_PER_GROUP + pr
                ps = slice(pair * LANES, (pair + 1) * LANES)
                xp = act_ref[:, ps]
                if is_meta:
                    xp = jnp.where(rowi < N_META, xp, 0.0)
                xbd = _pair_blockdiag(xp)
                y = _dot(_intra_pair(cb, cmask, a2, a2_t, dt_t, pair), xbd)
                st = st_ref[:, ps]
                y = y + y_off[:, pr * LANES:(pr + 1) * LANES] * _pair_out_scale(a2, pair)
                wn = jnp.concatenate([(bg_t * _row_bcast(w_t, 2 * pair)).astype(BF16),
                                      (bg_t * _row_bcast(w_t, 2 * pair + 1)).astype(BF16)], axis=1)
                st_ref[:, ps] = st * dec_rows[:, ps] + _dot(wn, xbd)
                y_ref[pl.ds(r0, q), ps] = (y + dsk_ref[:, ps] * xp).astype(y_ref.dtype)

    if n_sub == 1:
        chunk(0)
    else:
        def sub_chunk(sub, carry):
            chunk(pl.multiple_of(sub * q, q))
            return carry

        lax.fori_loop(0, n_sub, sub_chunk, 0)

    @pl.when(j == n_steps - 1)
    def _():
        for pair in range(SSM_HEADS // 2):
            ps = slice(pair * LANES, (pair + 1) * LANES)
            st_out_ref[0, ps, :] = st_ref[:, ps].T


def _ssd_chunks(xbc, dt, xbc_small, init, cw, cbias, dtb, alog, dsk, emat, layer, n_batch, n_chunks, n_sub,
                first_block, meta_block, is_meta, out_dtype):
    assert n_chunks % n_sub == 0 and first_block % n_sub == 0
    n_steps = n_chunks // n_sub
    rows = n_sub * BLOCK
    cur = lambda b, j: (first_block // n_sub + b * n_steps + j, 0)
    out = lambda b, j: (b * n_steps + j, 0)
    return pl.pallas_call(
        functools.partial(_ssd_chunk_body, is_meta, n_sub),
        grid=(n_batch, n_steps),
        in_specs=[pl.BlockSpec((rows, CONV_DIM), cur), pl.BlockSpec((rows, LANES), cur),
                  pl.BlockSpec((BLOCK, CONV_DIM), lambda b, j: (meta_block, 0)), _const_spec(init.shape),
                  _layer_spec(cw, layer), _layer_spec(cbias, layer), _layer_spec(dtb, layer),
                  _layer_spec(alog, layer), _layer_spec(dsk, layer), _const_spec(emat.shape)],
        out_specs=[pl.BlockSpec((rows, D_INNER), out),
                   pl.BlockSpec((1, D_INNER, D_STATE), lambda b, j: (b, 0, 0))],
        out_shape=[jax.ShapeDtypeStruct((n_batch * n_chunks * BLOCK, D_INNER), out_dtype),
                   jax.ShapeDtypeStruct((n_batch, D_INNER, D_STATE), F32)],
        scratch_shapes=[pltpu.VMEM((CONV_HALO, CONV_DIM), F32), pltpu.VMEM((BLOCK, CONV_DIM), F32),
                        pltpu.VMEM((D_STATE, D_INNER), F32)],
        compiler_params=_params(("parallel", "arbitrary")),
        name="ssd_meta" if is_meta else "ssd_prompt",
    )(xbc, dt, xbc_small, init, cw, cbias, dtb, alog, dsk, emat)


def _ssd_sample_body(*refs):
    nst_ref = refs[-2]

    last = pl.num_programs(1) - 1

    @pl.when(pl.program_id(1) == last)
    def _():
        _ssd_sample_step(*refs)

    @pl.when(pl.program_id(1) < last)
    def _():
        nst_ref[...] = jnp.zeros(nst_ref.shape, F32)


def _ssd_sample_step(xbc_ref, dt_ref, cs_ref, st_ref, cw_ref, cbias_ref, dtb_ref, alog_ref, dsk_ref, e_ref,
                     *rest):
    y_ref, nst_ref, yt_ref = rest[-3:]
    n_seq = st_ref.shape[0]
    n_new = xbc_ref.shape[0] // n_seq
    n_cs = CONV_W - 1
    qr = n_seq * n_new
    q = LANES
    kpad = 64
    assert qr + n_seq * n_cs <= kpad and n_new >= n_cs

    x_new = xbc_ref[...]
    xc = jnp.concatenate([x_new, cs_ref[...], jnp.zeros((kpad - qr - n_seq * n_cs, CONV_DIM), F32)], axis=0)
    r = lax.broadcasted_iota(jnp.int32, (qr, kpad), 0)
    c = lax.broadcasted_iota(jnp.int32, (qr, kpad), 1)
    s_of_r = r // n_new
    t_of_r = r % n_new
    conv = cbias_ref[...] + cw_ref[CONV_W - 1:CONV_W, :] * x_new
    for k in range(1, CONV_W):
        target = jnp.where(t_of_r >= k, r - k, qr + n_cs * s_of_r + t_of_r + n_cs - k)
        conv = conv + cw_ref[CONV_W - 1 - k:CONV_W - k, :] * _exact_left((c == target).astype(BF16), xc)
    act = jnp.concatenate([_silu(conv), jnp.zeros((q - qr, CONV_DIM), F32)], axis=0)
    xh = act[:, :D_INNER]
    bm = act[:, D_INNER:D_INNER + SSM_GROUPS * D_STATE]
    cm = act[:, D_INNER + SSM_GROUPS * D_STATE:]
    ri = lax.broadcasted_iota(jnp.int32, (q, q), 0)
    ci = lax.broadcasted_iota(jnp.int32, (q, q), 1)
    dt = jnp.concatenate([_softplus(dt_ref[...] + dtb_ref[...]), jnp.zeros((q - qr, LANES), F32)], axis=0)
    dt = jnp.where(ci < SSM_HEADS, dt, 0.0)
    a = dt * (-jnp.exp(alog_ref[...]))
    cmask = (ri >= ci) & (ri // n_new == ci // n_new) & (ri < qr)
    a_cum = _exact_left(cmask.astype(BF16), a)
    a2 = a_cum * LOG2E
    a2_t = a2.T
    dt_t = dt.T
    last = ((ci == (ri // n_new) * n_new + n_new - 1) & (ri < qr)).astype(BF16)
    a2_last = _exact_left(last, a_cum) * LOG2E
    w_exp = _exact_right(dt * jnp.exp2(a2_last - a2), e_ref[...])
    dec_exp = _exact_right(jnp.exp2(a2_last), e_ref[...])
    xw = xh * w_exp

    yt_ref[...] = jnp.zeros(yt_ref.shape, F32)
    colseq = ci // n_new
    rowseq = ri // n_new
    for g in range(SSM_GROUPS):
        gs = slice(g * D_STATE, (g + 1) * D_STATE)
        hs = slice(g * GROUP_WIDTH, (g + 1) * GROUP_WIDTH)
        cg_t = cm[:, gs].T.astype(BF16)
        bg = bm[:, gs].astype(BF16)
        xw_t = jnp.concatenate(
            [xw[:, g * GROUP_WIDTH + i * LANES:g * GROUP_WIDTH + (i + 1) * LANES].T
             for i in range(GROUP_WIDTH // LANES)], axis=0).astype(BF16)
        dec_t = jnp.concatenate(
            [dec_exp[:, g * GROUP_WIDTH + i * LANES:g * GROUP_WIDTH + (i + 1) * LANES].T
             for i in range(GROUP_WIDTH // LANES)], axis=0)
        zero = jnp.zeros((q, q), BF16)
        for s in range(n_seq):
            st0 = st_ref[s, hs, :]
            yt_ref[hs, :] += _dot(st0.astype(BF16), jnp.where(colseq == s, cg_t, zero))
            inc = _dot(xw_t, jnp.where(rowseq == s, bg, zero))
            dcol = jnp.broadcast_to(dec_t[:, s * n_new:s * n_new + 1], (GROUP_WIDTH, q))
            nst_ref[s, hs, :] = st0 * dcol + inc

    for g in range(SSM_GROUPS):
        gs = slice(g * D_STATE, (g + 1) * D_STATE)
        cb = _dot_nt(cm[:, gs].astype(BF16), bm[:, gs].astype(BF16))
        for pr in range(PAIRS_PER_GROUP):
            pair = g * PAIRS_PER_GROUP + pr
            ps = slice(pair * LANES, (pair + 1) * LANES)
            xp = xh[:, ps]
            y = _dot(_intra_pair(cb, cmask, a2, a2_t, dt_t, pair), _pair_blockdiag(xp))
            y = y + yt_ref[ps, :].T * _pair_out_scale(a2, pair) + dsk_ref[:, ps] * xp
            y_ref[:, ps] = y[0:qr, :]


def _ssd_sample(xbc, dt, cs, st, cw, cbias, dtb, alog, dsk, emat, n_new, layer, nst_prev=None):
    n_layers, n_seq = st.shape[0], st.shape[1]
    sb = SAMPLE_SEQS
    n_pass = n_layers if nst_prev is None else 1
    rows = lambda cols, per: pl.BlockSpec((sb * per, cols), lambda i, t: (i, 0))
    st_spec = pl.BlockSpec((None, sb, D_INNER, D_STATE), lambda i, t: (layer, i, 0, 0))
    nst_spec = pl.BlockSpec((None, sb, D_INNER, D_STATE), lambda i, t: ((layer + t + 1) % n_pass, i, 0, 0))
    if nst_prev is not None:
        nst_spec = st_spec
    in_specs = [rows(CONV_DIM, n_new), rows(LANES, n_new), rows(CONV_DIM, CONV_W - 1), st_spec,
                _layer_spec(cw, layer), _layer_spec(cbias, layer), _layer_spec(dtb, layer),
                _layer_spec(alog, layer), _layer_spec(dsk, layer), _const_spec(emat.shape)]
    args = [xbc, dt, cs, st, cw, cbias, dtb, alog, dsk, emat]
    aliases = {}
    if nst_prev is not None:
        in_specs.append(pl.BlockSpec(memory_space=pl.ANY))
        aliases = {len(args): 1}
        args.append(nst_prev)
    return pl.pallas_call(
        _ssd_sample_body,
        grid=(n_seq // sb, n_pass),
        in_specs=in_specs,
        out_specs=[rows(D_INNER, n_new), nst_spec],
        out_shape=[jax.ShapeDtypeStruct((n_seq * n_new, D_INNER), F32),
                   jax.ShapeDtypeStruct(st.shape, F32)],
        input_output_aliases=aliases,
        scratch_shapes=[pltpu.VMEM((D_INNER, LANES), F32)],
        compiler_params=_params(("parallel", "arbitrary")),
        name="ssd_sample",
    )(*args)


def _ssm_out_body(x_ref, y_ref, z_ref, nw_ref, w_ref, o_ref):
    y = y_ref[...].astype(F32) * _silu(z_ref[...].astype(F32))
    parts = []
    for g in range(SSM_GROUPS):
        yg = y[:, g * GROUP_WIDTH:(g + 1) * GROUP_WIDTH]
        parts.append(yg * lax.rsqrt(jnp.mean(yg * yg, axis=-1, keepdims=True) + EPS))
    yn = (jnp.concatenate(parts, axis=1) * nw_ref[...]).astype(BF16)
    o_ref[...] = x_ref[...] + _dot(yn, w_ref[...].astype(BF16))


def _ssm_out(x, y, z, nw, w, layer, tm):
    n = x.shape[0]
    return pl.pallas_call(
        _ssm_out_body,
        grid=(n // tm,),
        in_specs=[_row_spec(D_MODEL, tm), _row_spec(D_INNER, tm), _row_spec(D_INNER, tm),
                  _layer_spec(nw, layer), _layer_spec(w, layer)],
        out_specs=_row_spec(D_MODEL, tm),
        out_shape=jax.ShapeDtypeStruct(x.shape, F32),
        compiler_params=_params(("parallel",)),
        name="ssm_out",
    )(x, y, z, nw, w)


def _rows3(p, width=None):
    p = p.astype(F32)
    if width is not None:
        p = jnp.pad(p, ((0, 0), (0, width - p.shape[1])))
    return p[:, None, :]


def kernel(x_prompt, x_sample, cache_k, cache_v, state_conv, state_ssm, meta_tokens, norm_w,
           w_attn_in, q_norm_w, k_norm_w, attn_sinks, w_attn_out, w_ssm_in, conv_w, conv_b,
           dt_bias, a_log, d_skip, ssm_norm_w, w_ssm_out):
    n_batch, seq, _ = x_prompt.shape
    n_seq, n_new, _ = x_sample.shape
    wbuf = cache_k.shape[2]
    n_blocks = seq // BLOCK
    n_rows = n_seq * n_new
    assert seq % BLOCK == 0 and wbuf == WINDOW and n_rows % BLOCK == 0 and n_new == 4
    meta_block = n_rows // BLOCK
    small_rows = n_rows + BLOCK

    xp = x_prompt.reshape(n_batch * seq, D_MODEL)
    xs = jnp.concatenate([x_sample.reshape(n_rows, D_MODEL), meta_tokens.astype(F32),
                          jnp.zeros((BLOCK - N_META, D_MODEL), F32)], axis=0)

    emat = jnp.pad(jnp.repeat(jnp.eye(SSM_HEADS, dtype=BF16), SSM_HEAD_DIM, axis=1),
                   ((0, LANES - SSM_HEADS), (0, 0)))
    zero_state = jnp.zeros((1, D_INNER, D_STATE), F32)

    w_attn_in_b, w_ssm_in_b = w_attn_in.astype(BF16), w_ssm_in.astype(BF16)
    w_attn_out_b, w_ssm_out_b = w_attn_out, w_ssm_out
    n_attn, n_ssm = cache_k.shape[0], state_ssm.shape[0]
    ck_all = cache_k.reshape(n_attn, n_seq, wbuf * N_KV_HEADS, HEAD_DIM)
    cv_all = cache_v.reshape(n_attn, n_seq, wbuf * N_KV_HEADS, HEAD_DIM)
    st_all = state_ssm.reshape(n_ssm, n_seq, D_INNER, D_STATE)
    new_ck = new_cv = new_st = None

    nw, qn, kn, sinks = _rows3(norm_w), _rows3(q_norm_w), _rows3(k_norm_w), attn_sinks.astype(F32)
    snw = _rows3(ssm_norm_w)
    consts = (conv_w.astype(F32), _rows3(conv_b), _rows3(dt_bias, LANES), _rows3(a_log, LANES),
              _rows3(jnp.repeat(d_skip, SSM_HEAD_DIM, axis=1)), emat)

    kp_l, vp_l, cp_l, sp_l, cs_l = [], [], [], [], []
    for i in range(DEPTH):
        l = i // 2
        if i % 2 == 0:
            qs, ks, vs, gs = _attn_in(xs, nw, i, w_attn_in_b, qn, kn, l, F32, small_rows)
            o_meta = _attn_blocks(qs, ks, vs, ks, vs, sinks, l, 1, 1, 1, meta_block, meta_block, True, F32)
            qq = qs[:n_rows].reshape(n_seq, n_new, N_KV_HEADS, GQA_GROUP * HEAD_DIM).transpose(0, 2, 1, 3)
            o, new_ck, new_cv = _attn_sample(qq.reshape(n_seq, N_HEADS * n_new, HEAD_DIM),
                                             ks[:n_rows].reshape(n_seq, n_new * N_KV_HEADS, HEAD_DIM),
                                             vs[:n_rows].reshape(n_seq, n_new * N_KV_HEADS, HEAD_DIM),
                                             ck_all, cv_all, sinks, l, new_ck, new_cv)
            o = o.reshape(n_seq, N_KV_HEADS, n_new, GQA_GROUP * HEAD_DIM).transpose(0, 2, 1, 3)
            o = jnp.concatenate([o.reshape(n_rows, ATTN_WIDTH), o_meta], axis=0)

            q, k, v, g = _attn_in(xp, nw, i, w_attn_in_b, qn, kn, l, BF16, ROW_TILE)
            xp = _attn_blocks(q, k, v, ks, vs, sinks, l, n_batch, n_blocks, ATTN_SUB_BLOCKS, 0, meta_block, False,
                              F32, fused=(g, xp, w_attn_out_b))
            xs = _attn_out(xs, o, gs, w_attn_out_b, l, small_rows)
            kp_l.append(k.reshape(n_batch, seq, KV_WIDTH)[:, -WINDOW:].reshape(n_batch, WINDOW, N_KV_HEADS, HEAD_DIM))
            vp_l.append(v.reshape(n_batch, seq, KV_WIDTH)[:, -WINDOW:].reshape(n_batch, WINDOW, N_KV_HEADS, HEAD_DIM))
        else:
            zs, xbcs, dts = _ssm_in(xs, nw, i, w_ssm_in_b, l, F32, small_rows)
            y_meta, st_meta = _ssd_chunks(xbcs, dts, xbcs, zero_state, *consts, l, 1, 1, 1, meta_block, meta_block,
                                          True, F32)
            y, new_st = _ssd_sample(xbcs, dts, state_conv[l].reshape(n_seq * (CONV_W - 1), CONV_DIM), st_all,
                                    *consts, n_new, l, new_st)
            y = jnp.concatenate([y, y_meta], axis=0)

            z, xbc, dt = _ssm_in(xp, nw, i, w_ssm_in_b, l, BF16, ROW_TILE)
            yp, st = _ssd_chunks(xbc, dt, xbcs, st_meta, *consts, l, n_batch, n_blocks, SSD_SUB_CHUNKS, 0,
                                 meta_block, False, BF16)
            xp = _ssm_out(xp, yp, z, snw, w_ssm_out_b, l, ROW_TILE)
            xs = _ssm_out(xs, y, zs, snw, w_ssm_out_b, l, small_rows)
            cp_l.append(xbc.reshape(n_batch, seq, CONV_DIM)[:, -(CONV_W - 1):])
            sp_l.append(st.reshape(n_batch, SSM_HEADS, SSM_HEAD_DIM, D_STATE))
            cs_l.append(xbcs[:n_rows].reshape(n_seq, n_new, CONV_DIM)[:, -(CONV_W - 1):])

    y_prompt = xp.reshape(n_batch, seq, D_MODEL)
    y_sample = xs[:n_rows].reshape(n_seq, n_new, D_MODEL)
    return (y_prompt, y_sample,
            jnp.stack(kp_l), jnp.stack(vp_l), jnp.stack(cp_l), jnp.stack(sp_l),
            new_ck.reshape(cache_k.shape), new_cv.reshape(cache_v.shape), jnp.stack(cs_l),
            new_st.reshape(state_ssm.shape))
```

```python
import functools
import math

import jax
import jax.numpy as jnp
from jax import lax
from jax.experimental import pallas as pl
from jax.experimental.pallas import tpu as pltpu

F32 = jnp.float32
BF16 = jnp.bfloat16

D_MODEL = 1024
DEPTH = 4
N_META = 16
WINDOW = 128
BLOCK = 128
HEAD_DIM = 128
ATTN_WIDTH = 2 * D_MODEL
N_HEADS = ATTN_WIDTH // HEAD_DIM
N_KV_HEADS = 4
GQA_GROUP = N_HEADS // N_KV_HEADS
KV_WIDTH = N_KV_HEADS * HEAD_DIM
D_INNER = 2 * D_MODEL
SSM_HEAD_DIM = 64
SSM_HEADS = D_INNER // SSM_HEAD_DIM
D_STATE = 128
SSM_GROUPS = 4
GROUP_WIDTH = D_INNER // SSM_GROUPS
PAIRS_PER_GROUP = SSM_HEADS // SSM_GROUPS // 2
CONV_W = 4
CONV_DIM = D_INNER + 2 * SSM_GROUPS * D_STATE
EPS = 1e-6
NEG = -1e30
LOG2E = math.log2(math.e)
QK_SCALE_LOG2 = HEAD_DIM ** -0.5 * LOG2E

LANES = 128
ROW_TILE = 512
SAMPLE_SEQS = 8
ATTN_SUB_BLOCKS = 8
SSD_SUB_CHUNKS = 8
CONV_HALO = 16
V7X_VMEM_BYTES = 64 * 1024 * 1024
VMEM_LIMIT = V7X_VMEM_BYTES * 7 // 8


def _dot(a, b):
    return jnp.dot(a, b, preferred_element_type=F32)


def _dot_nt(a, b):
    return lax.dot_general(a, b, (((1,), (1,)), ((), ())), preferred_element_type=F32)


def _rmsnorm(x, w):
    return x * lax.rsqrt(jnp.mean(x * x, axis=-1, keepdims=True) + EPS) * w


def _silu(x):
    return x * jax.nn.sigmoid(x)


def _softplus(x):
    return jnp.maximum(x, 0.0) + jnp.log1p(jnp.exp(-jnp.abs(x)))


def _split3(x):
    hi = x.astype(BF16)
    r = x - hi.astype(F32)
    mid = r.astype(BF16)
    lo = (r - mid.astype(F32)).astype(BF16)
    return hi, mid, lo


def _exact_left(p, x):
    hi, mid, lo = _split3(x)
    return _dot(p, hi) + _dot(p, mid) + _dot(p, lo)


def _exact_right(x, e):
    hi, mid, lo = _split3(x)
    return _dot(hi, e) + _dot(mid, e) + _dot(lo, e)


def _params(semantics):
    return pltpu.CompilerParams(dimension_semantics=semantics, vmem_limit_bytes=VMEM_LIMIT)


def _const_spec(shape):
    nd = len(shape)
    return pl.BlockSpec(shape, lambda *_: (0,) * nd)


def _row_spec(cols, tm):
    return pl.BlockSpec((tm, cols), lambda i: (i, 0))


def _layer_spec(w, layer):
    return pl.BlockSpec((None,) + w.shape[1:], lambda *_: (layer, 0, 0))


def _attn_in_body(x_ref, nw_ref, w_ref, qn_ref, kn_ref, q_ref, k_ref, v_ref, g_ref):
    h = _rmsnorm(x_ref[...], nw_ref[...]).astype(BF16)
    q = _dot(h, w_ref[:, :ATTN_WIDTH])
    for hd in range(N_HEADS):
        sl = slice(hd * HEAD_DIM, (hd + 1) * HEAD_DIM)
        q_ref[:, sl] = (_rmsnorm(q[:, sl], qn_ref[...]) * QK_SCALE_LOG2).astype(q_ref.dtype)
    k = _dot(h, w_ref[:, ATTN_WIDTH:ATTN_WIDTH + KV_WIDTH])
    for hd in range(N_KV_HEADS):
        sl = slice(hd * HEAD_DIM, (hd + 1) * HEAD_DIM)
        k_ref[:, sl] = _rmsnorm(k[:, sl], kn_ref[...])
    v_ref[...] = _dot(h, w_ref[:, ATTN_WIDTH + KV_WIDTH:ATTN_WIDTH + 2 * KV_WIDTH])
    g_ref[...] = _dot(h, w_ref[:, ATTN_WIDTH + 2 * KV_WIDTH:]).astype(g_ref.dtype)


def _attn_in(x, nw, depth, w, qn, kn, layer, act_dtype, tm):
    n = x.shape[0]
    return pl.pallas_call(
        _attn_in_body,
        grid=(n // tm,),
        in_specs=[_row_spec(D_MODEL, tm), _layer_spec(nw, depth), _layer_spec(w, layer),
                  _layer_spec(qn, layer), _layer_spec(kn, layer)],
        out_specs=[_row_spec(ATTN_WIDTH, tm), _row_spec(KV_WIDTH, tm), _row_spec(KV_WIDTH, tm),
                   _row_spec(ATTN_WIDTH, tm)],
        out_shape=[jax.ShapeDtypeStruct((n, ATTN_WIDTH), act_dtype),
                   jax.ShapeDtypeStruct((n, KV_WIDTH), F32),
                   jax.ShapeDtypeStruct((n, KV_WIDTH), F32),
                   jax.ShapeDtypeStruct((n, ATTN_WIDTH), act_dtype)],
        compiler_params=_params(("parallel",)),
        name="attn_in",
    )(x, nw, w, qn, kn)


def _attn_block_body(is_meta, n_sub, layer, sink_ref, q_ref, kc_ref, kp_ref, vc_ref, vp_ref, km_ref, vm_ref, o_ref):
    j = pl.program_id(1)
    rows = GQA_GROUP * BLOCK
    rowg = lax.broadcasted_iota(jnp.int32, (rows, 1), 0) // BLOCK
    row = lax.broadcasted_iota(jnp.int32, (rows, 2 * BLOCK), 0) & (BLOCK - 1)
    col = lax.broadcasted_iota(jnp.int32, (rows, 2 * BLOCK), 1)
    cur_ok = (col >= BLOCK) & (col - BLOCK <= row)

    def block(r0, kprev, vprev, n_prev):
        mask = cur_ok
        if n_prev:
            mask = mask | ((col < n_prev) & (col > row - (BLOCK - n_prev)))
        for kv in range(N_KV_HEADS):
            sl = slice(kv * HEAD_DIM, (kv + 1) * HEAD_DIM)
            keys = jnp.concatenate([kprev(sl), kc_ref[r0:r0 + BLOCK, sl]], axis=0).astype(BF16)
            vals = jnp.concatenate([vprev(sl), vc_ref[r0:r0 + BLOCK, sl]], axis=0).astype(BF16)
            base = kv * GQA_GROUP * HEAD_DIM
            q4 = jnp.concatenate(
                [q_ref[r0:r0 + BLOCK, base + g * HEAD_DIM:base + (g + 1) * HEAD_DIM] for g in range(GQA_GROUP)],
                axis=0).astype(BF16)
            sink = jnp.zeros((rows, 1), F32)
            for g in range(GQA_GROUP):
                sink = jnp.where(rowg == g, sink_ref[layer, kv * GQA_GROUP + g] * LOG2E, sink)
            s = jnp.where(mask, _dot_nt(q4, keys), NEG)
            m = jnp.maximum(jnp.max(s, axis=-1, keepdims=True), sink)
            p = jnp.exp2(s - m)
            denom = jnp.sum(p, axis=-1, keepdims=True) + jnp.exp2(sink - m)
            o = _dot(p.astype(BF16), vals) / denom
            for g in range(GQA_GROUP):
                o_ref[r0:r0 + BLOCK, base + g * HEAD_DIM:base + (g + 1) * HEAD_DIM] = (
                    o[g * BLOCK:(g + 1) * BLOCK].astype(o_ref.dtype))

    if is_meta:
        block(0, lambda sl: km_ref[:, sl], lambda sl: vm_ref[:, sl], 0)
    else:
        @pl.when(j == 0)
        def _():
            block(0, lambda sl: km_ref[:, sl], lambda sl: vm_ref[:, sl], N_META)

        @pl.when(j > 0)
        def _():
            block(0, lambda sl: kp_ref[:, sl], lambda sl: vp_ref[:, sl], BLOCK)

    for sub in range(1, n_sub):
        p0 = (sub - 1) * BLOCK
        block(sub * BLOCK, lambda sl, p0=p0: kc_ref[p0:p0 + BLOCK, sl], lambda sl, p0=p0: vc_ref[p0:p0 + BLOCK, sl],
              BLOCK)


def _attn_blocks(q, k, v, k_small, v_small, sinks, layer, n_batch, n_blocks, n_sub, first_block, meta_block,
                 is_meta, out_dtype):
    assert n_blocks % n_sub == 0 and first_block % n_sub == 0
    n_steps = n_blocks // n_sub
    rows = n_sub * BLOCK
    cur = lambda b, j: (first_block // n_sub + b * n_steps + j, 0)
    prev = lambda b, j: (jnp.maximum(first_block + (b * n_steps + j) * n_sub - 1, 0), 0)
    meta = lambda b, j: (meta_block, 0)
    out = lambda b, j: (b * n_steps + j, 0)
    return pl.pallas_call(
        functools.partial(_attn_block_body, is_meta, n_sub, layer),
        grid=(n_batch, n_steps),
        in_specs=[pl.BlockSpec(memory_space=pltpu.SMEM),
                  pl.BlockSpec((rows, ATTN_WIDTH), cur),
                  pl.BlockSpec((rows, KV_WIDTH), cur), pl.BlockSpec((BLOCK, KV_WIDTH), prev),
                  pl.BlockSpec((rows, KV_WIDTH), cur), pl.BlockSpec((BLOCK, KV_WIDTH), prev),
                  pl.BlockSpec((BLOCK, KV_WIDTH), meta), pl.BlockSpec((BLOCK, KV_WIDTH), meta)],
        out_specs=pl.BlockSpec((rows, ATTN_WIDTH), out),
        out_shape=jax.ShapeDtypeStruct((n_batch * n_blocks * BLOCK, ATTN_WIDTH), out_dtype),
        compiler_params=_params(("parallel", "parallel")),
        name="attn_meta" if is_meta else "attn_prompt",
    )(sinks, q, k, k, v, v, k_small, v_small)


def _attn_sample_body(fill_layers, layer, sink_ref, q_ref, k_ref, v_ref, ck_ref, cv_ref, *rest):
    o_ref, nk_ref, nv_ref = rest[-3:]
    if fill_layers:
        for other in range(fill_layers):
            if other != layer:
                nk_ref[other] = jnp.zeros(nk_ref.shape[1:], F32)
                nv_ref[other] = jnp.zeros(nv_ref.shape[1:], F32)
        nk_ref, nv_ref = nk_ref.at[layer], nv_ref.at[layer]
    n_seq, rows, _ = q_ref.shape
    n_cache = ck_ref.shape[1]
    n_fresh = k_ref.shape[1]
    wbuf = n_cache // N_KV_HEADS
    per_kv = rows // N_KV_HEADS
    row = lax.broadcasted_iota(jnp.int32, (rows, n_cache + n_fresh), 0)
    col = lax.broadcasted_iota(jnp.int32, (rows, n_cache + n_fresh), 1)
    qpos = (row % per_kv) // GQA_GROUP
    kpos = jnp.where(col < n_cache, col // N_KV_HEADS - wbuf, (col - n_cache) // N_KV_HEADS)
    diff = qpos - kpos
    mask = (col % N_KV_HEADS == row // per_kv) & (diff >= 0) & (diff < WINDOW)
    row1 = lax.broadcasted_iota(jnp.int32, (rows, 1), 0)
    sink = jnp.zeros((rows, 1), F32)
    for hd in range(N_HEADS):
        sink = jnp.where((row1 // per_kv == hd // GQA_GROUP) & (row1 % GQA_GROUP == hd % GQA_GROUP),
                         sink_ref[layer, hd] * LOG2E, sink)
    for s in range(n_seq):
        keys = jnp.concatenate([ck_ref[s], k_ref[s]], axis=0).astype(BF16)
        vals = jnp.concatenate([cv_ref[s], v_ref[s]], axis=0).astype(BF16)
        sc = _dot_nt(q_ref[s].astype(BF16), keys)
        sc = jnp.where(mask, sc, NEG)
        m = jnp.maximum(jnp.max(sc, axis=-1, keepdims=True), sink)
        p = jnp.exp2(sc - m)
        denom = jnp.sum(p, axis=-1, keepdims=True) + jnp.exp2(sink - m)
        o_ref[s] = _dot(p.astype(BF16), vals) / denom
        nk_ref[s, 0:n_cache - n_fresh, :] = ck_ref[s, n_fresh:, :]
        nk_ref[s, n_cache - n_fresh:, :] = k_ref[s]
        nv_ref[s, 0:n_cache - n_fresh, :] = cv_ref[s, n_fresh:, :]
        nv_ref[s, n_cache - n_fresh:, :] = v_ref[s]


def _attn_sample(q, k, v, ck, cv, sinks, layer, nk_prev=None, nv_prev=None):
    sb = SAMPLE_SEQS
    n_seq = q.shape[0]
    n_layers = ck.shape[0]
    blk = lambda a: pl.BlockSpec((sb,) + a.shape[1:], lambda i: (i,) + (0,) * (a.ndim - 1))
    cache_spec = pl.BlockSpec((None, sb) + ck.shape[2:], lambda i: (layer, i, 0, 0))
    in_specs = [pl.BlockSpec(memory_space=pltpu.SMEM), blk(q), blk(k), blk(v), cache_spec, cache_spec]
    args = [sinks, q, k, v, ck, cv]
    if nk_prev is None:
        aliases, fill_layers = {}, n_layers
        new_spec = pl.BlockSpec((n_layers, sb) + ck.shape[2:], lambda i: (0, i, 0, 0))
    else:
        in_specs += [pl.BlockSpec(memory_space=pl.ANY)] * 2
        aliases, fill_layers = {len(args): 1, len(args) + 1: 2}, None
        new_spec = cache_spec
        args += [nk_prev, nv_prev]
    return pl.pallas_call(
        functools.partial(_attn_sample_body, fill_layers, layer),
        grid=(n_seq // sb,),
        in_specs=in_specs,
        out_specs=[blk(q), new_spec, new_spec],
        out_shape=[jax.ShapeDtypeStruct(q.shape, F32),
                   jax.ShapeDtypeStruct(ck.shape, F32),
                   jax.ShapeDtypeStruct(cv.shape, F32)],
        input_output_aliases=aliases,
        compiler_params=_params(("parallel",)),
        name="attn_sample",
    )(*args)


def _attn_out_body(x_ref, o_ref, g_ref, w_ref, y_ref):
    a = o_ref[...].astype(F32) * _silu(g_ref[...].astype(F32))
    y_ref[...] = x_ref[...] + _dot(a.astype(BF16), w_ref[...].astype(BF16))


def _attn_out(x, o, g, w, layer, tm):
    n = x.shape[0]
    return pl.pallas_call(
        _attn_out_body,
        grid=(n // tm,),
        in_specs=[_row_spec(D_MODEL, tm), _row_spec(ATTN_WIDTH, tm), _row_spec(ATTN_WIDTH, tm),
                  _layer_spec(w, layer)],
        out_specs=_row_spec(D_MODEL, tm),
        out_shape=jax.ShapeDtypeStruct(x.shape, F32),
        compiler_params=_params(("parallel",)),
        name="attn_out",
    )(x, o, g, w)


def _ssm_in_body(x_ref, nw_ref, w_ref, z_ref, xbc_ref, dt_ref):
    h = _rmsnorm(x_ref[...], nw_ref[...]).astype(BF16)
    z_ref[...] = _dot(h, w_ref[:, :D_INNER]).astype(z_ref.dtype)
    xbc_ref[...] = _dot(h, w_ref[:, D_INNER:D_INNER + CONV_DIM])
    dt_ref[...] = _dot(h, w_ref[:, D_INNER + CONV_DIM:])


def _ssm_in(x, nw, depth, w, layer, act_dtype, tm):
    n = x.shape[0]
    return pl.pallas_call(
        _ssm_in_body,
        grid=(n // tm,),
        in_specs=[_row_spec(D_MODEL, tm), _layer_spec(nw, depth), _layer_spec(w, layer)],
        out_specs=[_row_spec(D_INNER, tm), _row_spec(CONV_DIM, tm), _row_spec(LANES, tm)],
        out_shape=[jax.ShapeDtypeStruct((n, D_INNER), act_dtype),
                   jax.ShapeDtypeStruct((n, CONV_DIM), F32),
                   jax.ShapeDtypeStruct((n, LANES), F32)],
        compiler_params=_params(("parallel",)),
        name="ssm_in",
    )(x, nw, w)


def _lane_bcast(x, h):
    return jnp.broadcast_to(x[:, h:h + 1], (x.shape[0], LANES))


def _row_bcast(x, h):
    return jnp.broadcast_to(x[h:h + 1, :], (LANES, x.shape[1]))


def _pair_blockdiag(xp):
    lane = lax.broadcasted_iota(jnp.int32, xp.shape, 1)
    xb = xp.astype(BF16)
    zero = jnp.zeros_like(xb)
    return jnp.concatenate([jnp.where(lane < SSM_HEAD_DIM, xb, zero),
                            jnp.where(lane >= SSM_HEAD_DIM, xb, zero)], axis=0)


def _pair_select(a0, a1):
    lane = lax.broadcasted_iota(jnp.int32, a0.shape, 1)
    return jnp.where(lane < SSM_HEAD_DIM, a0, a1)


def _intra_pair(cb, cmask, a2, a2_t, dt_t, pair):
    ms = []
    for hh in range(2):
        h = 2 * pair + hh
        seg = _lane_bcast(a2, h) - _row_bcast(a2_t, h)
        dec = jnp.exp2(jnp.where(cmask, seg, NEG))
        ms.append((cb * dec * _row_bcast(dt_t, h)).astype(BF16))
    return jnp.concatenate(ms, axis=1)


def _pair_out_scale(a2, pair):
    return jnp.exp2(_pair_select(_lane_bcast(a2, 2 * pair), _lane_bcast(a2, 2 * pair + 1)))


def _ssd_chunk_body(is_meta, n_sub, xbc_ref, dt_ref, mtile_ref, init_ref, cw_ref, cbias_ref, dtb_ref, alog_ref,
                    dsk_ref, e_ref, y_ref, st_out_ref, halo_ref, act_ref, st_ref):
    j = pl.program_id(1)
    n_steps = pl.num_programs(1)
    q = BLOCK
    halo = CONV_HALO

    @pl.when(j == 0)
    def _():
        if is_meta:
            halo_ref[...] = jnp.zeros((halo, CONV_DIM), F32)
            st_ref[...] = jnp.zeros(st_ref.shape, F32)
        else:
            halo_ref[...] = mtile_ref[0:halo, :]
            for pair in range(SSM_HEADS // 2):
                ps = slice(pair * LANES, (pair + 1) * LANES)
                st_ref[:, ps] = init_ref[0, ps, :].T

    def chunk(r0):
        n_sh = CONV_W - 1
        r = lax.broadcasted_iota(jnp.int32, (q, n_sh * q), 0)
        c = lax.broadcasted_iota(jnp.int32, (q, n_sh * q), 1)
        sel = (c % q == r - (c // q + 1)).astype(BF16)
        top = 8
        rh = lax.broadcasted_iota(jnp.int32, (top, n_sh * halo), 0)
        ch = lax.broadcasted_iota(jnp.int32, (top, n_sh * halo), 1)
        sel_halo = (ch % halo == halo + rh - (ch // halo + 1)).astype(BF16)

        def taps(x, cs):
            return jnp.concatenate([(cw_ref[CONV_W - 1 - k:CONV_W - k, cs] * x).astype(BF16)
                                    for k in range(1, CONV_W)], axis=0)

        width = 512
        for cc in range(CONV_DIM // width):
            cs = slice(cc * width, (cc + 1) * width)
            xc = xbc_ref[pl.ds(r0, q), cs]
            conv = cbias_ref[:, cs] + cw_ref[CONV_W - 1:CONV_W, cs] * xc + _dot(sel, taps(xc, cs))
            act_ref[:, cs] = _silu(conv)
            conv_top = conv[0:top, :] + _dot(sel_halo, taps(halo_ref[:, cs], cs))
            act_ref[0:top, cs] = _silu(conv_top)
        halo_ref[...] = xbc_ref[pl.ds(r0 + q - halo, halo), :]

        ri = lax.broadcasted_iota(jnp.int32, (q, q), 0)
        ci = lax.broadcasted_iota(jnp.int32, (q, q), 1)
        dt_ok = ci < SSM_HEADS
        if is_meta:
            dt_ok = dt_ok & (ri < N_META)
        dt = jnp.where(dt_ok, _softplus(dt_ref[pl.ds(r0, q), :] + dtb_ref[...]), 0.0)
        a = dt * (-jnp.exp(alog_ref[...]))
        cmask = ri >= ci
        a2 = _exact_left(cmask.astype(BF16), a) * LOG2E
        a2_t = a2.T
        dt_t = dt.T
        w_t = dt_t * jnp.exp2(jnp.broadcast_to(a2_t[:, q - 1:q], (q, q)) - a2_t)
        dec_rows = _exact_right(jnp.exp2(jnp.broadcast_to(a2[q - 1:q, :], (8, LANES))), e_ref[...])[0:1, :]
        rowi = lax.broadcasted_iota(jnp.int32, (q, LANES), 0)

        for g in range(SSM_GROUPS):
            gs = slice(D_INNER + g * D_STATE, D_INNER + (g + 1) * D_STATE)
            bg = act_ref[:, gs]
            cg = act_ref[:, SSM_GROUPS * D_STATE + gs.start:SSM_GROUPS * D_STATE + gs.stop].astype(BF16)
            cb = _dot_nt(cg, bg.astype(BF16))
            bg_t = bg.T
            y_off = _dot(cg, st_ref[:, g * GROUP_WIDTH:(g + 1) * GROUP_WIDTH].astype(BF16))
            for pr in range(PAIRS_PER_GROUP):
                pair = g * PAIRS_PER_GROUP + pr
                ps = slice(pair * LANES, (pair + 1) * LANES)
                xp = act_ref[:, ps]
                if is_meta:
                    xp = jnp.where(rowi < N_META, xp, 0.0)
                xbd = _pair_blockdiag(xp)
                y = _dot(_intra_pair(cb, cmask, a2, a2_t, dt_t, pair), xbd)
                st = st_ref[:, ps]
                y = y + y_off[:, pr * LANES:(pr + 1) * LANES] * _pair_out_scale(a2, pair)
                wn = jnp.concatenate([(bg_t * _row_bcast(w_t, 2 * pair)).astype(BF16),
                                      (bg_t * _row_bcast(w_t, 2 * pair + 1)).astype(BF16)], axis=1)
                st_ref[:, ps] = st * dec_rows[:, ps] + _dot(wn, xbd)
                y_ref[pl.ds(r0, q), ps] = (y + dsk_ref[:, ps] * xp).astype(y_ref.dtype)

    if n_sub == 1:
        chunk(0)
    else:
        def sub_chunk(sub, carry):
            chunk(pl.multiple_of(sub * q, q))
            return carry

        lax.fori_loop(0, n_sub, sub_chunk, 0)

    @pl.when(j == n_steps - 1)
    def _():
        for pair in range(SSM_HEADS // 2):
            ps = slice(pair * LANES, (pair + 1) * LANES)
            st_out_ref[0, ps, :] = st_ref[:, ps].T


def _ssd_chunks(xbc, dt, xbc_small, init, cw, cbias, dtb, alog, dsk, emat, layer, n_batch, n_chunks, n_sub,
                first_block, meta_block, is_meta, out_dtype):
    assert n_chunks % n_sub == 0 and first_block % n_sub == 0
    n_steps = n_chunks // n_sub
    rows = n_sub * BLOCK
    cur = lambda b, j: (first_block // n_sub + b * n_steps + j, 0)
    out = lambda b, j: (b * n_steps + j, 0)
    return pl.pallas_call(
        functools.partial(_ssd_chunk_body, is_meta, n_sub),
        grid=(n_batch, n_steps),
        in_specs=[pl.BlockSpec((rows, CONV_DIM), cur), pl.BlockSpec((rows, LANES), cur),
                  pl.BlockSpec((BLOCK, CONV_DIM), lambda b, j: (meta_block, 0)), _const_spec(init.shape),
                  _layer_spec(cw, layer), _layer_spec(cbias, layer), _layer_spec(dtb, layer),
                  _layer_spec(alog, layer), _layer_spec(dsk, layer), _const_spec(emat.shape)],
        out_specs=[pl.BlockSpec((rows, D_INNER), out),
                   pl.BlockSpec((1, D_INNER, D_STATE), lambda b, j: (b, 0, 0))],
        out_shape=[jax.ShapeDtypeStruct((n_batch * n_chunks * BLOCK, D_INNER), out_dtype),
                   jax.ShapeDtypeStruct((n_batch, D_INNER, D_STATE), F32)],
        scratch_shapes=[pltpu.VMEM((CONV_HALO, CONV_DIM), F32), pltpu.VMEM((BLOCK, CONV_DIM), F32),
                        pltpu.VMEM((D_STATE, D_INNER), F32)],
        compiler_params=_params(("parallel", "arbitrary")),
        name="ssd_meta" if is_meta else "ssd_prompt",
    )(xbc, dt, xbc_small, init, cw, cbias, dtb, alog, dsk, emat)


def _ssd_sample_body(*refs):
    nst_ref = refs[-2]

    last = pl.num_programs(1) - 1

    @pl.when(pl.program_id(1) == last)
    def _():
        _ssd_sample_step(*refs)

    @pl.when(pl.program_id(1) < last)
    def _():
        nst_ref[...] = jnp.zeros(nst_ref.shape, F32)


def _ssd_sample_step(xbc_ref, dt_ref, cs_ref, st_ref, cw_ref, cbias_ref, dtb_ref, alog_ref, dsk_ref, e_ref,
                     *rest):
    y_ref, nst_ref, yt_ref = rest[-3:]
    n_seq = st_ref.shape[0]
    n_new = xbc_ref.shape[0] // n_seq
    n_cs = CONV_W - 1
    qr = n_seq * n_new
    q = LANES
    kpad = 64
    assert qr + n_seq * n_cs <= kpad and n_new >= n_cs

    x_new = xbc_ref[...]
    xc = jnp.concatenate([x_new, cs_ref[...], jnp.zeros((kpad - qr - n_seq * n_cs, CONV_DIM), F32)], axis=0)
    r = lax.broadcasted_iota(jnp.int32, (qr, kpad), 0)
    c = lax.broadcasted_iota(jnp.int32, (qr, kpad), 1)
    s_of_r = r // n_new
    t_of_r = r % n_new
    conv = cbias_ref[...] + cw_ref[CONV_W - 1:CONV_W, :] * x_new
    for k in range(1, CONV_W):
        target = jnp.where(t_of_r >= k, r - k, qr + n_cs * s_of_r + t_of_r + n_cs - k)
        conv = conv + cw_ref[CONV_W - 1 - k:CONV_W - k, :] * _exact_left((c == target).astype(BF16), xc)
    act = jnp.concatenate([_silu(conv), jnp.zeros((q - qr, CONV_DIM), F32)], axis=0)
    xh = act[:, :D_INNER]
    bm = act[:, D_INNER:D_INNER + SSM_GROUPS * D_STATE]
    cm = act[:, D_INNER + SSM_GROUPS * D_STATE:]
    ri = lax.broadcasted_iota(jnp.int32, (q, q), 0)
    ci = lax.broadcasted_iota(jnp.int32, (q, q), 1)
    dt = jnp.concatenate([_softplus(dt_ref[...] + dtb_ref[...]), jnp.zeros((q - qr, LANES), F32)], axis=0)
    dt = jnp.where(ci < SSM_HEADS, dt, 0.0)
    a = dt * (-jnp.exp(alog_ref[...]))
    cmask = (ri >= ci) & (ri // n_new == ci // n_new) & (ri < qr)
    a_cum = _exact_left(cmask.astype(BF16), a)
    a2 = a_cum * LOG2E
    a2_t = a2.T
    dt_t = dt.T
    last = ((ci == (ri // n_new) * n_new + n_new - 1) & (ri < qr)).astype(BF16)
    a2_last = _exact_left(last, a_cum) * LOG2E
    w_exp = _exact_right(dt * jnp.exp2(a2_last - a2), e_ref[...])
    dec_exp = _exact_right(jnp.exp2(a2_last), e_ref[...])
    xw = xh * w_exp

    yt_ref[...] = jnp.zeros(yt_ref.shape, F32)
    colseq = ci // n_new
    rowseq = ri // n_new
    for g in range(SSM_GROUPS):
        gs = slice(g * D_STATE, (g + 1) * D_STATE)
        hs = slice(g * GROUP_WIDTH, (g + 1) * GROUP_WIDTH)
        cg_t = cm[:, gs].T.astype(BF16)
        bg = bm[:, gs].astype(BF16)
        xw_t = jnp.concatenate(
            [xw[:, g * GROUP_WIDTH + i * LANES:g * GROUP_WIDTH + (i + 1) * LANES].T
             for i in range(GROUP_WIDTH // LANES)], axis=0).astype(BF16)
        dec_t = jnp.concatenate(
            [dec_exp[:, g * GROUP_WIDTH + i * LANES:g * GROUP_WIDTH + (i + 1) * LANES].T
             for i in range(GROUP_WIDTH // LANES)], axis=0)
        zero = jnp.zeros((q, q), BF16)
        for s in range(n_seq):
            st0 = st_ref[s, hs, :]
            yt_ref[hs, :] += _dot(st0.astype(BF16), jnp.where(colseq == s, cg_t, zero))
            inc = _dot(xw_t, jnp.where(rowseq == s, bg, zero))
            dcol = jnp.broadcast_to(dec_t[:, s * n_new:s * n_new + 1], (GROUP_WIDTH, q))
            nst_ref[s, hs, :] = st0 * dcol + inc

    for g in range(SSM_GROUPS):
        gs = slice(g * D_STATE, (g + 1) * D_STATE)
        cb = _dot_nt(cm[:, gs].astype(BF16), bm[:, gs].astype(BF16))
        for pr in range(PAIRS_PER_GROUP):
            pair = g * PAIRS_PER_GROUP + pr
            ps = slice(pair * LANES, (pair + 1) * LANES)
            xp = xh[:, ps]
            y = _dot(_intra_pair(cb, cmask, a2, a2_t, dt_t, pair), _pair_blockdiag(xp))
            y = y + yt_ref[ps, :].T * _pair_out_scale(a2, pair) + dsk_ref[:, ps] * xp
            y_ref[:, ps] = y[0:qr, :]


def _ssd_sample(xbc, dt, cs, st, cw, cbias, dtb, alog, dsk, emat, n_new, layer, nst_prev=None):
    n_layers, n_seq = st.shape[0], st.shape[1]
    sb = SAMPLE_SEQS
    n_pass = n_layers if nst_prev is None else 1
    rows = lambda cols, per: pl.BlockSpec((sb * per, cols), lambda i, t: (i, 0))
    st_spec = pl.BlockSpec((None, sb, D_INNER, D_STATE), lambda i, t: (layer, i, 0, 0))
    nst_spec = pl.BlockSpec((None, sb, D_INNER, D_STATE), lambda i, t: ((layer + t + 1) % n_pass, i, 0, 0))
    if nst_prev is not None:
        nst_spec = st_spec
    in_specs = [rows(CONV_DIM, n_new), rows(LANES, n_new), rows(CONV_DIM, CONV_W - 1), st_spec,
                _layer_spec(cw, layer), _layer_spec(cbias, layer), _layer_spec(dtb, layer),
                _layer_spec(alog, layer), _layer_spec(dsk, layer), _const_spec(emat.shape)]
    args = [xbc, dt, cs, st, cw, cbias, dtb, alog, dsk, emat]
    aliases = {}
    if nst_prev is not None:
        in_specs.append(pl.BlockSpec(memory_space=pl.ANY))
        aliases = {len(args): 1}
        args.append(nst_prev)
    return pl.pallas_call(
        _ssd_sample_body,
        grid=(n_seq // sb, n_pass),
        in_specs=in_specs,
        out_specs=[rows(D_INNER, n_new), nst_spec],
        out_shape=[jax.ShapeDtypeStruct((n_seq * n_new, D_INNER), F32),
                   jax.ShapeDtypeStruct(st.shape, F32)],
        input_output_aliases=aliases,
        scratch_shapes=[pltpu.VMEM((D_INNER, LANES), F32)],
        compiler_params=_params(("parallel", "arbitrary")),
        name="ssd_sample",
    )(*args)


def _ssm_out_body(x_ref, y_ref, z_ref, nw_ref, w_ref, o_ref):
    y = y_ref[...].astype(F32) * _silu(z_ref[...].astype(F32))
    parts = []
    for g in range(SSM_GROUPS):
        yg = y[:, g * GROUP_WIDTH:(g + 1) * GROUP_WIDTH]
        parts.append(yg * lax.rsqrt(jnp.mean(yg * yg, axis=-1, keepdims=True) + EPS))
    yn = (jnp.concatenate(parts, axis=1) * nw_ref[...]).astype(BF16)
    o_ref[...] = x_ref[...] + _dot(yn, w_ref[...].astype(BF16))


def _ssm_out(x, y, z, nw, w, layer, tm):
    n = x.shape[0]
    return pl.pallas_call(
        _ssm_out_body,
        grid=(n // tm,),
        in_specs=[_row_spec(D_MODEL, tm), _row_spec(D_INNER, tm), _row_spec(D_INNER, tm),
                  _layer_spec(nw, layer), _layer_spec(w, layer)],
        out_specs=_row_spec(D_MODEL, tm),
        out_shape=jax.ShapeDtypeStruct(x.shape, F32),
        compiler_params=_params(("parallel",)),
        name="ssm_out",
    )(x, y, z, nw, w)


def _rows3(p, width=None):
    p = p.astype(F32)
    if width is not None:
        p = jnp.pad(p, ((0, 0), (0, width - p.shape[1])))
    return p[:, None, :]


def kernel(x_prompt, x_sample, cache_k, cache_v, state_conv, state_ssm, meta_tokens, norm_w,
           w_attn_in, q_norm_w, k_norm_w, attn_sinks, w_attn_out, w_ssm_in, conv_w, conv_b,
           dt_bias, a_log, d_skip, ssm_norm_w, w_ssm_out):
    n_batch, seq, _ = x_prompt.shape
    n_seq, n_new, _ = x_sample.shape
    wbuf = cache_k.shape[2]
    n_blocks = seq // BLOCK
    n_rows = n_seq * n_new
    assert seq % BLOCK == 0 and wbuf == WINDOW and n_rows % BLOCK == 0 and n_new == 4
    meta_block = n_rows // BLOCK
    small_rows = n_rows + BLOCK

    xp = x_prompt.reshape(n_batch * seq, D_MODEL)
    xs = jnp.concatenate([x_sample.reshape(n_rows, D_MODEL), meta_tokens.astype(F32),
                          jnp.zeros((BLOCK - N_META, D_MODEL), F32)], axis=0)

    emat = jnp.pad(jnp.repeat(jnp.eye(SSM_HEADS, dtype=BF16), SSM_HEAD_DIM, axis=1),
                   ((0, LANES - SSM_HEADS), (0, 0)))
    zero_state = jnp.zeros((1, D_INNER, D_STATE), F32)

    w_attn_in_b = w_attn_in.astype(BF16)
    w_ssm_in_b = jnp.pad(w_ssm_in, ((0, 0), (0, 0), (0, LANES - SSM_HEADS))).astype(BF16)
    w_attn_out_b, w_ssm_out_b = w_attn_out, w_ssm_out
    n_attn, n_ssm = cache_k.shape[0], state_ssm.shape[0]
    ck_all = cache_k.reshape(n_attn, n_seq, wbuf * N_KV_HEADS, HEAD_DIM)
    cv_all = cache_v.reshape(n_attn, n_seq, wbuf * N_KV_HEADS, HEAD_DIM)
    st_all = state_ssm.reshape(n_ssm, n_seq, D_INNER, D_STATE)
    new_ck = new_cv = new_st = None

    nw, qn, kn, sinks = _rows3(norm_w), _rows3(q_norm_w), _rows3(k_norm_w), attn_sinks.astype(F32)
    snw = _rows3(ssm_norm_w)
    consts = (conv_w.astype(F32), _rows3(conv_b), _rows3(dt_bias, LANES), _rows3(a_log, LANES),
              _rows3(jnp.repeat(d_skip, SSM_HEAD_DIM, axis=1)), emat)

    kp_l, vp_l, cp_l, sp_l, cs_l = [], [], [], [], []
    for i in range(DEPTH):
        l = i // 2
        if i % 2 == 0:
            qs, ks, vs, gs = _attn_in(xs, nw, i, w_attn_in_b, qn, kn, l, F32, small_rows)
            o_meta = _attn_blocks(qs, ks, vs, ks, vs, sinks, l, 1, 1, 1, meta_block, meta_block, True, F32)
            qq = qs[:n_rows].reshape(n_seq, n_new, N_KV_HEADS, GQA_GROUP * HEAD_DIM).transpose(0, 2, 1, 3)
            o, new_ck, new_cv = _attn_sample(qq.reshape(n_seq, N_HEADS * n_new, HEAD_DIM),
                                             ks[:n_rows].reshape(n_seq, n_new * N_KV_HEADS, HEAD_DIM),
                                             vs[:n_rows].reshape(n_seq, n_new * N_KV_HEADS, HEAD_DIM),
                                             ck_all, cv_all, sinks, l, new_ck, new_cv)
            o = o.reshape(n_seq, N_KV_HEADS, n_new, GQA_GROUP * HEAD_DIM).transpose(0, 2, 1, 3)
            o = jnp.concatenate([o.reshape(n_rows, ATTN_WIDTH), o_meta], axis=0)

            q, k, v, g = _attn_in(xp, nw, i, w_attn_in_b, qn, kn, l, BF16, ROW_TILE)
            op = _attn_blocks(q, k, v, ks, vs, sinks, l, n_batch, n_blocks, ATTN_SUB_BLOCKS, 0, meta_block, False,
                              BF16)
            xp = _attn_out(xp, op, g, w_attn_out_b, l, ROW_TILE)
            xs = _attn_out(xs, o, gs, w_attn_out_b, l, small_rows)
            kp_l.append(k.reshape(n_batch, seq, KV_WIDTH)[:, -WINDOW:].reshape(n_batch, WINDOW, N_KV_HEADS, HEAD_DIM))
            vp_l.append(v.reshape(n_batch, seq, KV_WIDTH)[:, -WINDOW:].reshape(n_batch, WINDOW, N_KV_HEADS, HEAD_DIM))
        else:
            zs, xbcs, dts = _ssm_in(xs, nw, i, w_ssm_in_b, l, F32, small_rows)
            y_meta, st_meta = _ssd_chunks(xbcs, dts, xbcs, zero_state, *consts, l, 1, 1, 1, meta_block, meta_block,
                                          True, F32)
            y, new_st = _ssd_sample(xbcs, dts, state_conv[l].reshape(n_seq * (CONV_W - 1), CONV_DIM), st_all,
                                    *consts, n_new, l, new_st)
            y = jnp.concatenate([y, y_meta], axis=0)

            z, xbc, dt = _ssm_in(xp, nw, i, w_ssm_in_b, l, BF16, ROW_TILE)
            yp, st = _ssd_chunks(xbc, dt, xbcs, st_meta, *consts, l, n_batch, n_blocks, SSD_SUB_CHUNKS, 0,
                                 meta_block, False, BF16)
            xp = _ssm_out(xp, yp, z, snw, w_ssm_out_b, l, ROW_TILE)
            xs = _ssm_out(xs, y, zs, snw, w_ssm_out_b, l, small_rows)
            cp_l.append(xbc.reshape(n_batch, seq, CONV_DIM)[:, -(CONV_W - 1):])
            sp_l.append(st.reshape(n_batch, SSM_HEADS, SSM_HEAD_DIM, D_STATE))
            cs_l.append(xbcs[:n_rows].reshape(n_seq, n_new, CONV_DIM)[:, -(CONV_W - 1):])

    y_prompt = xp.reshape(n_batch, seq, D_MODEL)
    y_sample = xs[:n_rows].reshape(n_seq, n_new, D_MODEL)
    return (y_prompt, y_sample,
            jnp.stack(kp_l), jnp.stack(vp_l), jnp.stack(cp_l), jnp.stack(sp_l),
            new_ck.reshape(cache_k.shape), new_cv.reshape(cache_v.shape), jnp.stack(cs_l),
            new_st.reshape(state_ssm.shape))
```

```python
import functools
import math

import jax
import jax.numpy as jnp
from jax import lax
from jax.experimental import pallas as pl
from jax.experimental.pallas import tpu as pltpu

F32 = jnp.float32
BF16 = jnp.bfloat16

D_MODEL = 1024
DEPTH = 4
N_META = 16
WINDOW = 128
BLOCK = 128
HEAD_DIM = 128
ATTN_WIDTH = 2 * D_MODEL
N_HEADS = ATTN_WIDTH // HEAD_DIM
N_KV_HEADS = 4
GQA_GROUP = N_HEADS // N_KV_HEADS
KV_WIDTH = N_KV_HEADS * HEAD_DIM
D_INNER = 2 * D_MODEL
SSM_HEAD_DIM = 64
SSM_HEADS = D_INNER // SSM_HEAD_DIM
D_STATE = 128
SSM_GROUPS = 4
GROUP_WIDTH = D_INNER // SSM_GROUPS
PAIRS_PER_GROUP = SSM_HEADS // SSM_GROUPS // 2
CONV_W = 4
CONV_DIM = D_INNER + 2 * SSM_GROUPS * D_STATE
EPS = 1e-6
NEG = -1e30
LOG2E = math.log2(math.e)
QK_SCALE_LOG2 = HEAD_DIM ** -0.5 * LOG2E

LANES = 128
ROW_TILE = 512
SAMPLE_SEQS = 8
RING_SLOTS = 3
ATTN_SUB_BLOCKS = 8
SSD_SUB_CHUNKS = 4
CONV_HALO = 16
V7X_VMEM_BYTES = 64 * 1024 * 1024
VMEM_LIMIT = V7X_VMEM_BYTES * 7 // 8


def _dot(a, b):
    return jnp.dot(a, b, preferred_element_type=F32)


def _dot_nt(a, b):
    return lax.dot_general(a, b, (((1,), (1,)), ((), ())), preferred_element_type=F32)


def _rmsnorm(x, w):
    return x * lax.rsqrt(jnp.mean(x * x, axis=-1, keepdims=True) + EPS) * w


def _silu(x):
    return x * jax.nn.sigmoid(x)


def _softplus(x):
    return jnp.maximum(x, 0.0) + jnp.log1p(jnp.exp(-jnp.abs(x)))


def _split3(x):
    hi = x.astype(BF16)
    r = x - hi.astype(F32)
    mid = r.astype(BF16)
    lo = (r - mid.astype(F32)).astype(BF16)
    return hi, mid, lo


def _exact_left(p, x):
    hi, mid, lo = _split3(x)
    return _dot(p, hi) + _dot(p, mid) + _dot(p, lo)


def _exact_right(x, e):
    hi, mid, lo = _split3(x)
    return _dot(hi, e) + _dot(mid, e) + _dot(lo, e)


def _params(semantics):
    return pltpu.CompilerParams(dimension_semantics=semantics, vmem_limit_bytes=VMEM_LIMIT)


def _const_spec(shape):
    nd = len(shape)
    return pl.BlockSpec(shape, lambda *_: (0,) * nd)


def _row_spec(cols, tm):
    return pl.BlockSpec((tm, cols), lambda i: (i, 0))


def _layer_spec(w, layer):
    return pl.BlockSpec((None,) + w.shape[1:], lambda *_: (layer, 0, 0))


def _attn_in_body(x_ref, nw_ref, w_ref, qn_ref, kn_ref, q_ref, k_ref, v_ref, g_ref):
    h = _rmsnorm(x_ref[...], nw_ref[...]).astype(BF16)
    q = _dot(h, w_ref[:, :ATTN_WIDTH])
    for hd in range(N_HEADS):
        sl = slice(hd * HEAD_DIM, (hd + 1) * HEAD_DIM)
        q_ref[:, sl] = (_rmsnorm(q[:, sl], qn_ref[...]) * QK_SCALE_LOG2).astype(q_ref.dtype)
    k = _dot(h, w_ref[:, ATTN_WIDTH:ATTN_WIDTH + KV_WIDTH])
    for hd in range(N_KV_HEADS):
        sl = slice(hd * HEAD_DIM, (hd + 1) * HEAD_DIM)
        k_ref[:, sl] = _rmsnorm(k[:, sl], kn_ref[...])
    v_ref[...] = _dot(h, w_ref[:, ATTN_WIDTH + KV_WIDTH:ATTN_WIDTH + 2 * KV_WIDTH])
    g_ref[...] = _dot(h, w_ref[:, ATTN_WIDTH + 2 * KV_WIDTH:]).astype(g_ref.dtype)


def _attn_in(x, nw, depth, w, qn, kn, layer, act_dtype, tm):
    n = x.shape[0]
    return pl.pallas_call(
        _attn_in_body,
        grid=(n // tm,),
        in_specs=[_row_spec(D_MODEL, tm), _layer_spec(nw, depth), _layer_spec(w, layer),
                  _layer_spec(qn, layer), _layer_spec(kn, layer)],
        out_specs=[_row_spec(ATTN_WIDTH, tm), _row_spec(KV_WIDTH, tm), _row_spec(KV_WIDTH, tm),
                   _row_spec(ATTN_WIDTH, tm)],
        out_shape=[jax.ShapeDtypeStruct((n, ATTN_WIDTH), act_dtype),
                   jax.ShapeDtypeStruct((n, KV_WIDTH), F32),
                   jax.ShapeDtypeStruct((n, KV_WIDTH), F32),
                   jax.ShapeDtypeStruct((n, ATTN_WIDTH), act_dtype)],
        compiler_params=_params(("parallel",)),
        name="attn_in",
    )(x, nw, w, qn, kn)


def _attn_block_body(is_meta, n_sub, layer, sink_ref, q_ref, kc_ref, kp_ref, vc_ref, vp_ref, km_ref, vm_ref, o_ref):
    j = pl.program_id(1)
    rows = GQA_GROUP * BLOCK
    rowg = lax.broadcasted_iota(jnp.int32, (rows, 1), 0) // BLOCK
    row = lax.broadcasted_iota(jnp.int32, (rows, 2 * BLOCK), 0) & (BLOCK - 1)
    col = lax.broadcasted_iota(jnp.int32, (rows, 2 * BLOCK), 1)
    cur_ok = (col >= BLOCK) & (col - BLOCK <= row)

    def block(r0, kprev, vprev, n_prev):
        mask = cur_ok
        if n_prev:
            mask = mask | ((col < n_prev) & (col > row - (BLOCK - n_prev)))
        for kv in range(N_KV_HEADS):
            sl = slice(kv * HEAD_DIM, (kv + 1) * HEAD_DIM)
            keys = jnp.concatenate([kprev(sl), kc_ref[r0:r0 + BLOCK, sl]], axis=0).astype(BF16)
            vals = jnp.concatenate([vprev(sl), vc_ref[r0:r0 + BLOCK, sl]], axis=0).astype(BF16)
            base = kv * GQA_GROUP * HEAD_DIM
            q4 = jnp.concatenate(
                [q_ref[r0:r0 + BLOCK, base + g * HEAD_DIM:base + (g + 1) * HEAD_DIM] for g in range(GQA_GROUP)],
                axis=0).astype(BF16)
            sink = jnp.zeros((rows, 1), F32)
            for g in range(GQA_GROUP):
                sink = jnp.where(rowg == g, sink_ref[layer, kv * GQA_GROUP + g] * LOG2E, sink)
            s = jnp.where(mask, _dot_nt(q4, keys), NEG)
            m = jnp.maximum(jnp.max(s, axis=-1, keepdims=True), sink)
            p = jnp.exp2(s - m)
            denom = jnp.sum(p, axis=-1, keepdims=True) + jnp.exp2(sink - m)
            o = _dot(p.astype(BF16), vals) / denom
            for g in range(GQA_GROUP):
                o_ref[r0:r0 + BLOCK, base + g * HEAD_DIM:base + (g + 1) * HEAD_DIM] = (
                    o[g * BLOCK:(g + 1) * BLOCK].astype(o_ref.dtype))

    if is_meta:
        block(0, lambda sl: km_ref[:, sl], lambda sl: vm_ref[:, sl], 0)
    else:
        @pl.when(j == 0)
        def _():
            block(0, lambda sl: km_ref[:, sl], lambda sl: vm_ref[:, sl], N_META)

        @pl.when(j > 0)
        def _():
            block(0, lambda sl: kp_ref[:, sl], lambda sl: vp_ref[:, sl], BLOCK)

    for sub in range(1, n_sub):
        p0 = (sub - 1) * BLOCK
        block(sub * BLOCK, lambda sl, p0=p0: kc_ref[p0:p0 + BLOCK, sl], lambda sl, p0=p0: vc_ref[p0:p0 + BLOCK, sl],
              BLOCK)


def _attn_blocks(q, k, v, k_small, v_small, sinks, layer, n_batch, n_blocks, n_sub, first_block, meta_block,
                 is_meta, out_dtype):
    assert n_blocks % n_sub == 0 and first_block % n_sub == 0
    n_steps = n_blocks // n_sub
    rows = n_sub * BLOCK
    cur = lambda b, j: (first_block // n_sub + b * n_steps + j, 0)
    prev = lambda b, j: (jnp.maximum(first_block + (b * n_steps + j) * n_sub - 1, 0), 0)
    meta = lambda b, j: (meta_block, 0)
    out = lambda b, j: (b * n_steps + j, 0)
    return pl.pallas_call(
        functools.partial(_attn_block_body, is_meta, n_sub, layer),
        grid=(n_batch, n_steps),
        in_specs=[pl.BlockSpec(memory_space=pltpu.SMEM),
                  pl.BlockSpec((rows, ATTN_WIDTH), cur),
                  pl.BlockSpec((rows, KV_WIDTH), cur), pl.BlockSpec((BLOCK, KV_WIDTH), prev),
                  pl.BlockSpec((rows, KV_WIDTH), cur), pl.BlockSpec((BLOCK, KV_WIDTH), prev),
                  pl.BlockSpec((BLOCK, KV_WIDTH), meta), pl.BlockSpec((BLOCK, KV_WIDTH), meta)],
        out_specs=pl.BlockSpec((rows, ATTN_WIDTH), out),
        out_shape=jax.ShapeDtypeStruct((n_batch * n_blocks * BLOCK, ATTN_WIDTH), out_dtype),
        compiler_params=_params(("parallel", "parallel")),
        name="attn_meta" if is_meta else "attn_prompt",
    )(sinks, q, k, k, v, v, k_small, v_small)


def _attn_sample_body(fill_layers, layer, sink_ref, q_ref, k_ref, v_ref, ck_ref, cv_ref, *rest):
    o_ref, nk_ref, nv_ref = rest[-3:]
    if fill_layers:
        for other in range(fill_layers):
            if other != layer:
                nk_ref[other] = jnp.zeros(nk_ref.shape[1:], F32)
                nv_ref[other] = jnp.zeros(nv_ref.shape[1:], F32)
        nk_ref, nv_ref = nk_ref.at[layer], nv_ref.at[layer]
    n_seq, rows, _ = q_ref.shape
    n_cache = ck_ref.shape[1]
    n_fresh = k_ref.shape[1]
    wbuf = n_cache // N_KV_HEADS
    per_kv = rows // N_KV_HEADS
    row = lax.broadcasted_iota(jnp.int32, (rows, n_cache + n_fresh), 0)
    col = lax.broadcasted_iota(jnp.int32, (rows, n_cache + n_fresh), 1)
    qpos = (row % per_kv) // GQA_GROUP
    kpos = jnp.where(col < n_cache, col // N_KV_HEADS - wbuf, (col - n_cache) // N_KV_HEADS)
    diff = qpos - kpos
    mask = (col % N_KV_HEADS == row // per_kv) & (diff >= 0) & (diff < WINDOW)
    row1 = lax.broadcasted_iota(jnp.int32, (rows, 1), 0)
    sink = jnp.zeros((rows, 1), F32)
    for hd in range(N_HEADS):
        sink = jnp.where((row1 // per_kv == hd // GQA_GROUP) & (row1 % GQA_GROUP == hd % GQA_GROUP),
                         sink_ref[layer, hd] * LOG2E, sink)
    for s in range(n_seq):
        keys = jnp.concatenate([ck_ref[s], k_ref[s]], axis=0).astype(BF16)
        vals = jnp.concatenate([cv_ref[s], v_ref[s]], axis=0).astype(BF16)
        sc = _dot_nt(q_ref[s].astype(BF16), keys)
        sc = jnp.where(mask, sc, NEG)
        m = jnp.maximum(jnp.max(sc, axis=-1, keepdims=True), sink)
        p = jnp.exp2(sc - m)
        denom = jnp.sum(p, axis=-1, keepdims=True) + jnp.exp2(sink - m)
        o_ref[s] = _dot(p.astype(BF16), vals) / denom
        nk_ref[s, 0:n_cache - n_fresh, :] = ck_ref[s, n_fresh:, :]
        nk_ref[s, n_cache - n_fresh:, :] = k_ref[s]
        nv_ref[s, 0:n_cache - n_fresh, :] = cv_ref[s, n_fresh:, :]
        nv_ref[s, n_cache - n_fresh:, :] = v_ref[s]


def _attn_sample(q, k, v, ck, cv, sinks, layer, nk_prev=None, nv_prev=None):
    sb = SAMPLE_SEQS
    n_seq = q.shape[0]
    n_layers = ck.shape[0]
    blk = lambda a: pl.BlockSpec((sb,) + a.shape[1:], lambda i: (i,) + (0,) * (a.ndim - 1))
    cache_spec = pl.BlockSpec((None, sb) + ck.shape[2:], lambda i: (layer, i, 0, 0))
    in_specs = [pl.BlockSpec(memory_space=pltpu.SMEM), blk(q), blk(k), blk(v), cache_spec, cache_spec]
    args = [sinks, q, k, v, ck, cv]
    if nk_prev is None:
        aliases, fill_layers = {}, n_layers
        new_spec = pl.BlockSpec((n_layers, sb) + ck.shape[2:], lambda i: (0, i, 0, 0))
    else:
        in_specs += [pl.BlockSpec(memory_space=pl.ANY)] * 2
        aliases, fill_layers = {len(args): 1, len(args) + 1: 2}, None
        new_spec = cache_spec
        args += [nk_prev, nv_prev]
    return pl.pallas_call(
        functools.partial(_attn_sample_body, fill_layers, layer),
        grid=(n_seq // sb,),
        in_specs=in_specs,
        out_specs=[blk(q), new_spec, new_spec],
        out_shape=[jax.ShapeDtypeStruct(q.shape, F32),
                   jax.ShapeDtypeStruct(ck.shape, F32),
                   jax.ShapeDtypeStruct(cv.shape, F32)],
        input_output_aliases=aliases,
        compiler_params=_params(("parallel",)),
        name="attn_sample",
    )(*args)


def _attn_out_body(x_ref, o_ref, g_ref, w_ref, y_ref):
    a = o_ref[...].astype(F32) * _silu(g_ref[...].astype(F32))
    y_ref[...] = x_ref[...] + _dot(a.astype(BF16), w_ref[...].astype(BF16))


def _ring_body(inner, hbm0, hbm1, hbm2, *rest):
    *other, buf0, buf1, buf2, sem = rest
    hbms, bufs = (hbm0, hbm1, hbm2), (buf0, buf1, buf2)
    tm = buf0.shape[1]
    step, n_steps = pl.program_id(0), pl.num_programs(0)

    def copies(t):
        slot = t % RING_SLOTS
        rows = pl.ds(pl.multiple_of(t * tm, tm), tm)
        return [pltpu.make_async_copy(h.at[rows, :], b.at[slot], sem.at[i, slot])
                for i, (h, b) in enumerate(zip(hbms, bufs))]

    @pl.when(step == 0)
    def _():
        for t in range(RING_SLOTS - 1):
            for c in copies(t):
                c.start()

    @pl.when(step + RING_SLOTS - 1 < n_steps)
    def _():
        for c in copies(step + RING_SLOTS - 1):
            c.start()

    for c in copies(step):
        c.wait()
    slot = step % RING_SLOTS
    inner(buf0.at[slot], buf1.at[slot], buf2.at[slot], *other)


def _ring_call(inner, rows3, others, other_specs, tm, name):
    n = rows3[0].shape[0]
    assert n // tm >= RING_SLOTS
    return pl.pallas_call(
        functools.partial(_ring_body, inner),
        grid=(n // tm,),
        in_specs=[pl.BlockSpec(memory_space=pl.ANY)] * 3 + other_specs,
        out_specs=_row_spec(D_MODEL, tm),
        out_shape=jax.ShapeDtypeStruct((n, D_MODEL), F32),
        scratch_shapes=[pltpu.VMEM((RING_SLOTS, tm, a.shape[1]), a.dtype) for a in rows3]
        + [pltpu.SemaphoreType.DMA((3, RING_SLOTS))],
        compiler_params=_params(("arbitrary",)),
        name=name,
    )(*rows3, *others)


def _attn_out(x, o, g, w, layer, tm):
    n = x.shape[0]
    if n // tm >= RING_SLOTS:
        return _ring_call(_attn_out_body, (x, o, g), (w,), [_layer_spec(w, layer)], tm, "attn_out")
    return pl.pallas_call(
        _attn_out_body,
        grid=(n // tm,),
        in_specs=[_row_spec(D_MODEL, tm), _row_spec(ATTN_WIDTH, tm), _row_spec(ATTN_WIDTH, tm),
                  _layer_spec(w, layer)],
        out_specs=_row_spec(D_MODEL, tm),
        out_shape=jax.ShapeDtypeStruct(x.shape, F32),
        compiler_params=_params(("parallel",)),
        name="attn_out",
    )(x, o, g, w)


def _ssm_in_body(x_ref, nw_ref, w_ref, z_ref, xbc_ref, dt_ref):
    h = _rmsnorm(x_ref[...], nw_ref[...]).astype(BF16)
    z_ref[...] = _dot(h, w_ref[:, :D_INNER]).astype(z_ref.dtype)
    xbc_ref[...] = _dot(h, w_ref[:, D_INNER:D_INNER + CONV_DIM])
    dt_ref[:, :SSM_HEADS] = _dot(h, w_ref[:, D_INNER + CONV_DIM:])
    dt_ref[:, SSM_HEADS:] = jnp.zeros((dt_ref.shape[0], LANES - SSM_HEADS), F32)


def _ssm_in(x, nw, depth, w, layer, act_dtype, tm):
    n = x.shape[0]
    return pl.pallas_call(
        _ssm_in_body,
        grid=(n // tm,),
        in_specs=[_row_spec(D_MODEL, tm), _layer_spec(nw, depth), _layer_spec(w, layer)],
        out_specs=[_row_spec(D_INNER, tm), _row_spec(CONV_DIM, tm), _row_spec(LANES, tm)],
        out_shape=[jax.ShapeDtypeStruct((n, D_INNER), act_dtype),
                   jax.ShapeDtypeStruct((n, CONV_DIM), F32),
                   jax.ShapeDtypeStruct((n, LANES), F32)],
        compiler_params=_params(("parallel",)),
        name="ssm_in",
    )(x, nw, w)


def _lane_bcast(x, h):
    return jnp.broadcast_to(x[:, h:h + 1], (x.shape[0], LANES))


def _row_bcast(x, h):
    return jnp.broadcast_to(x[h:h + 1, :], (LANES, x.shape[1]))


def _pair_blockdiag(xp):
    lane = lax.broadcasted_iota(jnp.int32, xp.shape, 1)
    xb = xp.astype(BF16)
    zero = jnp.zeros_like(xb)
    return jnp.concatenate([jnp.where(lane < SSM_HEAD_DIM, xb, zero),
                            jnp.where(lane >= SSM_HEAD_DIM, xb, zero)], axis=0)


def _pair_select(a0, a1):
    lane = lax.broadcasted_iota(jnp.int32, a0.shape, 1)
    return jnp.where(lane < SSM_HEAD_DIM, a0, a1)


def _intra_pair(cb, cmask, a2, a2_t, dt_t, pair):
    ms = []
    for hh in range(2):
        h = 2 * pair + hh
        seg = _lane_bcast(a2, h) - _row_bcast(a2_t, h)
        dec = jnp.exp2(jnp.where(cmask, seg, NEG))
        ms.append((cb * dec * _row_bcast(dt_t, h)).astype(BF16))
    return jnp.concatenate(ms, axis=1)


def _pair_out_scale(a2, pair):
    return jnp.exp2(_pair_select(_lane_bcast(a2, 2 * pair), _lane_bcast(a2, 2 * pair + 1)))


def _ssd_chunk_body(is_meta, n_sub, xbc_ref, dt_ref, mtile_ref, init_ref, cw_ref, cbias_ref, dtb_ref, alog_ref,
                    dsk_ref, e_ref, y_ref, st_out_ref, halo_ref, act_ref, st_ref):
    j = pl.program_id(1)
    n_steps = pl.num_programs(1)
    q = BLOCK
    halo = CONV_HALO

    @pl.when(j == 0)
    def _():
        if is_meta:
            halo_ref[...] = jnp.zeros((halo, CONV_DIM), F32)
            st_ref[...] = jnp.zeros(st_ref.shape, F32)
        else:
            halo_ref[...] = mtile_ref[0:halo, :]
            for pair in range(SSM_HEADS // 2):
                ps = slice(pair * LANES, (pair + 1) * LANES)
                st_ref[:, ps] = init_ref[0, ps, :].T

    def chunk(r0):
        n_sh = CONV_W - 1
        r = lax.broadcasted_iota(jnp.int32, (q, n_sh * q), 0)
        c = lax.broadcasted_iota(jnp.int32, (q, n_sh * q), 1)
        sel = (c % q == r - (c // q + 1)).astype(BF16)
        top = 8
        rh = lax.broadcasted_iota(jnp.int32, (top, n_sh * halo), 0)
        ch = lax.broadcasted_iota(jnp.int32, (top, n_sh * halo), 1)
        sel_halo = (ch % halo == halo + rh - (ch // halo + 1)).astype(BF16)

        def taps(x, cs):
            return jnp.concatenate([(cw_ref[CONV_W - 1 - k:CONV_W - k, cs] * x).astype(BF16)
                                    for k in range(1, CONV_W)], axis=0)

        width = 512
        for cc in range(CONV_DIM // width):
            cs = slice(cc * width, (cc + 1) * width)
            xc = xbc_ref[pl.ds(r0, q), cs]
            conv = cbias_ref[:, cs] + cw_ref[CONV_W - 1:CONV_W, cs] * xc + _dot(sel, taps(xc, cs))
            act_ref[:, cs] = _silu(conv)
            conv_top = conv[0:top, :] + _dot(sel_halo, taps(halo_ref[:, cs], cs))
            act_ref[0:top, cs] = _silu(conv_top)
        halo_ref[...] = xbc_ref[pl.ds(r0 + q - halo, halo), :]

        ri = lax.broadcasted_iota(jnp.int32, (q, q), 0)
        ci = lax.broadcasted_iota(jnp.int32, (q, q), 1)
        dt_ok = ci < SSM_HEADS
        if is_meta:
            dt_ok = dt_ok & (ri < N_META)
        dt = jnp.where(dt_ok, _softplus(dt_ref[pl.ds(r0, q), :] + dtb_ref[...]), 0.0)
        a = dt * (-jnp.exp(alog_ref[...]))
        cmask = ri >= ci
        a2 = _exact_left(cmask.astype(BF16), a) * LOG2E
        a2_t = a2.T
        dt_t = dt.T
        w_t = dt_t * jnp.exp2(jnp.broadcast_to(a2_t[:, q - 1:q], (q, q)) - a2_t)
        dec_rows = _exact_right(jnp.exp2(jnp.broadcast_to(a2[q - 1:q, :], (8, LANES))), e_ref[...])[0:1, :]
        rowi = lax.broadcasted_iota(jnp.int32, (q, LANES), 0)

        for g in range(SSM_GROUPS):
            gs = slice(D_INNER + g * D_STATE, D_INNER + (g + 1) * D_STATE)
            bg = act_ref[:, gs]
            cg = act_ref[:, SSM_GROUPS * D_STATE + gs.start:SSM_GROUPS * D_STATE + gs.stop].astype(BF16)
            cb = _dot_nt(cg, bg.astype(BF16))
            bg_t = bg.T
            y_off = _dot(cg, st_ref[:, g * GROUP_WIDTH:(g + 1) * GROUP_WIDTH].astype(BF16))
            for pr in range(PAIRS_PER_GROUP):
                pair = g * PAIRS_PER_GROUP + pr
                ps = slice(pair * LANES, (pair + 1) * LANES)
                xp = act_ref[:, ps]
                if is_meta:
                    xp = jnp.where(rowi < N_META, xp, 0.0)
                xbd = _pair_blockdiag(xp)
                y = _dot(_intra_pair(cb, cmask, a2, a2_t, dt_t, pair), xbd)
                st = st_ref[:, ps]
                y = y + y_off[:, pr * LANES:(pr + 1) * LANES] * _pair_out_scale(a2, pair)
                wn = jnp.concatenate([(bg_t * _row_bcast(w_t, 2 * pair)).astype(BF16),
                                      (bg_t * _row_bcast(w_t, 2 * pair + 1)).astype(BF16)], axis=1)
                st_ref[:, ps] = st * dec_rows[:, ps] + _dot(wn, xbd)
                y_ref[pl.ds(r0, q), ps] = (y + dsk_ref[:, ps] * xp).astype(y_ref.dtype)

    if n_sub == 1:
        chunk(0)
    else:
        def sub_chunk(sub, carry):
            chunk(pl.multiple_of(sub * q, q))
            return carry

        lax.fori_loop(0, n_sub, sub_chunk, 0)

    @pl.when(j == n_steps - 1)
    def _():
        for pair in range(SSM_HEADS // 2):
            ps = slice(pair * LANES, (pair + 1) * LANES)
            st_out_ref[0, ps, :] = st_ref[:, ps].T


def _ssd_chunks(xbc, dt, xbc_small, init, cw, cbias, dtb, alog, dsk, emat, layer, n_batch, n_chunks, n_sub,
                first_block, meta_block, is_meta, out_dtype):
    assert n_chunks % n_sub == 0 and first_block % n_sub == 0
    n_steps = n_chunks // n_sub
    rows = n_sub * BLOCK
    cur = lambda b, j: (first_block // n_sub + b * n_steps + j, 0)
    out = lambda b, j: (b * n_steps + j, 0)
    return pl.pallas_call(
        functools.partial(_ssd_chunk_body, is_meta, n_sub),
        grid=(n_batch, n_steps),
        in_specs=[pl.BlockSpec((rows, CONV_DIM), cur), pl.BlockSpec((rows, LANES), cur),
                  pl.BlockSpec((BLOCK, CONV_DIM), lambda b, j: (meta_block, 0)), _const_spec(init.shape),
                  _layer_spec(cw, layer), _layer_spec(cbias, layer), _layer_spec(dtb, layer),
                  _layer_spec(alog, layer), _layer_spec(dsk, layer), _const_spec(emat.shape)],
        out_specs=[pl.BlockSpec((rows, D_INNER), out),
                   pl.BlockSpec((1, D_INNER, D_STATE), lambda b, j: (b, 0, 0))],
        out_shape=[jax.ShapeDtypeStruct((n_batch * n_chunks * BLOCK, D_INNER), out_dtype),
                   jax.ShapeDtypeStruct((n_batch, D_INNER, D_STATE), F32)],
        scratch_shapes=[pltpu.VMEM((CONV_HALO, CONV_DIM), F32), pltpu.VMEM((BLOCK, CONV_DIM), F32),
                        pltpu.VMEM((D_STATE, D_INNER), F32)],
        compiler_params=_params(("parallel", "arbitrary")),
        name="ssd_meta" if is_meta else "ssd_prompt",
    )(xbc, dt, xbc_small, init, cw, cbias, dtb, alog, dsk, emat)


def _ssd_sample_body(*refs):
    nst_ref = refs[-2]

    last = pl.num_programs(1) - 1

    @pl.when(pl.program_id(1) == last)
    def _():
        _ssd_sample_step(*refs)

    @pl.when(pl.program_id(1) < last)
    def _():
        nst_ref[...] = jnp.zeros(nst_ref.shape, F32)


def _ssd_sample_step(xbc_ref, dt_ref, cs_ref, st_ref, cw_ref, cbias_ref, dtb_ref, alog_ref, dsk_ref, e_ref,
                     *rest):
    y_ref, nst_ref, yt_ref = rest[-3:]
    n_seq = st_ref.shape[0]
    n_new = xbc_ref.shape[0] // n_seq
    n_cs = CONV_W - 1
    qr = n_seq * n_new
    q = LANES
    kpad = 64
    assert qr + n_seq * n_cs <= kpad and n_new >= n_cs

    x_new = xbc_ref[...]
    xc = jnp.concatenate([x_new, cs_ref[...], jnp.zeros((kpad - qr - n_seq * n_cs, CONV_DIM), F32)], axis=0)
    r = lax.broadcasted_iota(jnp.int32, (qr, kpad), 0)
    c = lax.broadcasted_iota(jnp.int32, (qr, kpad), 1)
    s_of_r = r // n_new
    t_of_r = r % n_new
    conv = cbias_ref[...] + cw_ref[CONV_W - 1:CONV_W, :] * x_new
    for k in range(1, CONV_W):
        target = jnp.where(t_of_r >= k, r - k, qr + n_cs * s_of_r + t_of_r + n_cs - k)
        conv = conv + cw_ref[CONV_W - 1 - k:CONV_W - k, :] * _exact_left((c == target).astype(BF16), xc)
    act = jnp.concatenate([_silu(conv), jnp.zeros((q - qr, CONV_DIM), F32)], axis=0)
    xh = act[:, :D_INNER]
    bm = act[:, D_INNER:D_INNER + SSM_GROUPS * D_STATE]
    cm = act[:, D_INNER + SSM_GROUPS * D_STATE:]
    ri = lax.broadcasted_iota(jnp.int32, (q, q), 0)
    ci = lax.broadcasted_iota(jnp.int32, (q, q), 1)
    dt = jnp.concatenate([_softplus(dt_ref[...] + dtb_ref[...]), jnp.zeros((q - qr, LANES), F32)], axis=0)
    dt = jnp.where(ci < SSM_HEADS, dt, 0.0)
    a = dt * (-jnp.exp(alog_ref[...]))
    cmask = (ri >= ci) & (ri // n_new == ci // n_new) & (ri < qr)
    a_cum = _exact_left(cmask.astype(BF16), a)
    a2 = a_cum * LOG2E
    a2_t = a2.T
    dt_t = dt.T
    last = ((ci == (ri // n_new) * n_new + n_new - 1) & (ri < qr)).astype(BF16)
    a2_last = _exact_left(last, a_cum) * LOG2E
    w_exp = _exact_right(dt * jnp.exp2(a2_last - a2), e_ref[...])
    dec_exp = _exact_right(jnp.exp2(a2_last), e_ref[...])
    xw = xh * w_exp

    yt_ref[...] = jnp.zeros(yt_ref.shape, F32)
    colseq = ci // n_new
    rowseq = ri // n_new
    for g in range(SSM_GROUPS):
        gs = slice(g * D_STATE, (g + 1) * D_STATE)
        hs = slice(g * GROUP_WIDTH, (g + 1) * GROUP_WIDTH)
        cg_t = cm[:, gs].T.astype(BF16)
        bg = bm[:, gs].astype(BF16)
        xw_t = jnp.concatenate(
            [xw[:, g * GROUP_WIDTH + i * LANES:g * GROUP_WIDTH + (i + 1) * LANES].T
             for i in range(GROUP_WIDTH // LANES)], axis=0).astype(BF16)
        dec_t = jnp.concatenate(
            [dec_exp[:, g * GROUP_WIDTH + i * LANES:g * GROUP_WIDTH + (i + 1) * LANES].T
             for i in range(GROUP_WIDTH // LANES)], axis=0)
        zero = jnp.zeros((q, q), BF16)
        for s in range(n_seq):
            st0 = st_ref[s, hs, :]
            yt_ref[hs, :] += _dot(st0.astype(BF16), jnp.where(colseq == s, cg_t, zero))
            inc = _dot(xw_t, jnp.where(rowseq == s, bg, zero))
            dcol = jnp.broadcast_to(dec_t[:, s * n_new:s * n_new + 1], (GROUP_WIDTH, q))
            nst_ref[s, hs, :] = st0 * dcol + inc

    for g in range(SSM_GROUPS):
        gs = slice(g * D_STATE, (g + 1) * D_STATE)
        cb = _dot_nt(cm[:, gs].astype(BF16), bm[:, gs].astype(BF16))
        for pr in range(PAIRS_PER_GROUP):
            pair = g * PAIRS_PER_GROUP + pr
            ps = slice(pair * LANES, (pair + 1) * LANES)
            xp = xh[:, ps]
            y = _dot(_intra_pair(cb, cmask, a2, a2_t, dt_t, pair), _pair_blockdiag(xp))
            y = y + yt_ref[ps, :].T * _pair_out_scale(a2, pair) + dsk_ref[:, ps] * xp
            y_ref[:, ps] = y[0:qr, :]


def _ssd_sample(xbc, dt, cs, st, cw, cbias, dtb, alog, dsk, emat, n_new, layer, nst_prev=None):
    n_layers, n_seq = st.shape[0], st.shape[1]
    sb = SAMPLE_SEQS
    n_pass = n_layers if nst_prev is None else 1
    rows = lambda cols, per: pl.BlockSpec((sb * per, cols), lambda i, t: (i, 0))
    st_spec = pl.BlockSpec((None, sb, D_INNER, D_STATE), lambda i, t: (layer, i, 0, 0))
    nst_spec = pl.BlockSpec((None, sb, D_INNER, D_STATE), lambda i, t: ((layer + t + 1) % n_pass, i, 0, 0))
    if nst_prev is not None:
        nst_spec = st_spec
    in_specs = [rows(CONV_DIM, n_new), rows(LANES, n_new), rows(CONV_DIM, CONV_W - 1), st_spec,
                _layer_spec(cw, layer), _layer_spec(cbias, layer), _layer_spec(dtb, layer),
                _layer_spec(alog, layer), _layer_spec(dsk, layer), _const_spec(emat.shape)]
    args = [xbc, dt, cs, st, cw, cbias, dtb, alog, dsk, emat]
    aliases = {}
    if nst_prev is not None:
        in_specs.append(pl.BlockSpec(memory_space=pl.ANY))
        aliases = {len(args): 1}
        args.append(nst_prev)
    return pl.pallas_call(
        _ssd_sample_body,
        grid=(n_seq // sb, n_pass),
        in_specs=in_specs,
        out_specs=[rows(D_INNER, n_new), nst_spec],
        out_shape=[jax.ShapeDtypeStruct((n_seq * n_new, D_INNER), F32),
                   jax.ShapeDtypeStruct(st.shape, F32)],
        input_output_aliases=aliases,
        scratch_shapes=[pltpu.VMEM((D_INNER, LANES), F32)],
        compiler_params=_params(("parallel", "arbitrary")),
        name="ssd_sample",
    )(*args)


def _ssm_out_body(x_ref, y_ref, z_ref, nw_ref, w_ref, o_ref):
    y = y_ref[...].astype(F32) * _silu(z_ref[...].astype(F32))
    parts = []
    for g in range(SSM_GROUPS):
        yg = y[:, g * GROUP_WIDTH:(g + 1) * GROUP_WIDTH]
        parts.append(yg * lax.rsqrt(jnp.mean(yg * yg, axis=-1, keepdims=True) + EPS))
    yn = (jnp.concatenate(parts, axis=1) * nw_ref[...]).astype(BF16)
    o_ref[...] = x_ref[...] + _dot(yn, w_ref[...].astype(BF16))


def _ssm_out(x, y, z, nw, w, layer, tm):
    n = x.shape[0]
    if n // tm >= RING_SLOTS:
        return _ring_call(_ssm_out_body, (x, y, z), (nw, w), [_layer_spec(nw, layer), _layer_spec(w, layer)], tm,
                          "ssm_out")
    return pl.pallas_call(
        _ssm_out_body,
        grid=(n // tm,),
        in_specs=[_row_spec(D_MODEL, tm), _row_spec(D_INNER, tm), _row_spec(D_INNER, tm),
                  _layer_spec(nw, layer), _layer_spec(w, layer)],
        out_specs=_row_spec(D_MODEL, tm),
        out_shape=jax.ShapeDtypeStruct(x.shape, F32),
        compiler_params=_params(("parallel",)),
        name="ssm_out",
    )(x, y, z, nw, w)


def _rows3(p, width=None):
    p = p.astype(F32)
    if width is not None:
        p = jnp.pad(p, ((0, 0), (0, width - p.shape[1])))
    return p[:, None, :]


def kernel(x_prompt, x_sample, cache_k, cache_v, state_conv, state_ssm, meta_tokens, norm_w,
           w_attn_in, q_norm_w, k_norm_w, attn_sinks, w_attn_out, w_ssm_in, conv_w, conv_b,
           dt_bias, a_log, d_skip, ssm_norm_w, w_ssm_out):
    n_batch, seq, _ = x_prompt.shape
    n_seq, n_new, _ = x_sample.shape
    wbuf = cache_k.shape[2]
    n_blocks = seq // BLOCK
    n_rows = n_seq * n_new
    assert seq % BLOCK == 0 and wbuf == WINDOW and n_rows % BLOCK == 0 and n_new == 4
    meta_block = n_rows // BLOCK
    small_rows = n_rows + BLOCK

    xp = x_prompt.reshape(n_batch * seq, D_MODEL)
    xs = jnp.concatenate([x_sample.reshape(n_rows, D_MODEL), meta_tokens.astype(F32),
                          jnp.zeros((BLOCK - N_META, D_MODEL), F32)], axis=0)

    emat = jnp.pad(jnp.repeat(jnp.eye(SSM_HEADS, dtype=BF16), SSM_HEAD_DIM, axis=1),
                   ((0, LANES - SSM_HEADS), (0, 0)))
    zero_state = jnp.zeros((1, D_INNER, D_STATE), F32)

    w_attn_in_b, w_ssm_in_b = w_attn_in.astype(BF16), w_ssm_in.astype(BF16)
    w_attn_out_b, w_ssm_out_b = w_attn_out, w_ssm_out
    n_attn, n_ssm = cache_k.shape[0], state_ssm.shape[0]
    ck_all = cache_k.reshape(n_attn, n_seq, wbuf * N_KV_HEADS, HEAD_DIM)
    cv_all = cache_v.reshape(n_attn, n_seq, wbuf * N_KV_HEADS, HEAD_DIM)
    st_all = state_ssm.reshape(n_ssm, n_seq, D_INNER, D_STATE)
    new_ck = new_cv = new_st = None

    nw, qn, kn, sinks = _rows3(norm_w), _rows3(q_norm_w), _rows3(k_norm_w), attn_sinks.astype(F32)
    snw = _rows3(ssm_norm_w)
    consts = (conv_w.astype(F32), _rows3(conv_b), _rows3(dt_bias, LANES), _rows3(a_log, LANES),
              _rows3(jnp.repeat(d_skip, SSM_HEAD_DIM, axis=1)), emat)

    kp_l, vp_l, cp_l, sp_l, cs_l = [], [], [], [], []
    for i in range(DEPTH):
        l = i // 2
        if i % 2 == 0:
            qs, ks, vs, gs = _attn_in(xs, nw, i, w_attn_in_b, qn, kn, l, F32, small_rows)
            o_meta = _attn_blocks(qs, ks, vs, ks, vs, sinks, l, 1, 1, 1, meta_block, meta_block, True, F32)
            qq = qs[:n_rows].reshape(n_seq, n_new, N_KV_HEADS, GQA_GROUP * HEAD_DIM).transpose(0, 2, 1, 3)
            o, new_ck, new_cv = _attn_sample(qq.reshape(n_seq, N_HEADS * n_new, HEAD_DIM),
                                             ks[:n_rows].reshape(n_seq, n_new * N_KV_HEADS, HEAD_DIM),
                                             vs[:n_rows].reshape(n_seq, n_new * N_KV_HEADS, HEAD_DIM),
                                             ck_all, cv_all, sinks, l, new_ck, new_cv)
            o = o.reshape(n_seq, N_KV_HEADS, n_new, GQA_GROUP * HEAD_DIM).transpose(0, 2, 1, 3)
            o = jnp.concatenate([o.reshape(n_rows, ATTN_WIDTH), o_meta], axis=0)

            q, k, v, g = _attn_in(xp, nw, i, w_attn_in_b, qn, kn, l, BF16, ROW_TILE)
            op = _attn_blocks(q, k, v, ks, vs, sinks, l, n_batch, n_blocks, ATTN_SUB_BLOCKS, 0, meta_block, False,
                              BF16)
            xp = _attn_out(xp, op, g, w_attn_out_b, l, ROW_TILE)
            xs = _attn_out(xs, o, gs, w_attn_out_b, l, small_rows)
            kp_l.append(k.reshape(n_batch, seq, KV_WIDTH)[:, -WINDOW:].reshape(n_batch, WINDOW, N_KV_HEADS, HEAD_DIM))
            vp_l.append(v.reshape(n_batch, seq, KV_WIDTH)[:, -WINDOW:].reshape(n_batch, WINDOW, N_KV_HEADS, HEAD_DIM))
        else:
            zs, xbcs, dts = _ssm_in(xs, nw, i, w_ssm_in_b, l, F32, small_rows)
            y_meta, st_meta = _ssd_chunks(xbcs, dts, xbcs, zero_state, *consts, l, 1, 1, 1, meta_block, meta_block,
                                          True, F32)
            y, new_st = _ssd_sample(xbcs, dts, state_conv[l].reshape(n_seq * (CONV_W - 1), CONV_DIM), st_all,
                                    *consts, n_new, l, new_st)
            y = jnp.concatenate([y, y_meta], axis=0)

            z, xbc, dt = _ssm_in(xp, nw, i, w_ssm_in_b, l, BF16, ROW_TILE)
            yp, st = _ssd_chunks(xbc, dt, xbcs, st_meta, *consts, l, n_batch, n_blocks, SSD_SUB_CHUNKS, 0,
                                 meta_block, False, BF16)
            xp = _ssm_out(xp, yp, z, snw, w_ssm_out_b, l, ROW_TILE)
            xs = _ssm_out(xs, y, zs, snw, w_ssm_out_b, l, small_rows)
            cp_l.append(xbc.reshape(n_batch, seq, CONV_DIM)[:, -(CONV_W - 1):])
            sp_l.append(st.reshape(n_batch, SSM_HEADS, SSM_HEAD_DIM, D_STATE))
            cs_l.append(xbcs[:n_rows].reshape(n_seq, n_new, CONV_DIM)[:, -(CONV_W - 1):])

    y_prompt = xp.reshape(n_batch, seq, D_MODEL)
    y_sample = xs[:n_rows].reshape(n_seq, n_new, D_MODEL)
    return (y_prompt, y_sample,
            jnp.stack(kp_l), jnp.stack(vp_l), jnp.stack(cp_l), jnp.stack(sp_l),
            new_ck.reshape(cache_k.shape), new_cv.reshape(cache_v.shape), jnp.stack(cs_l),
            new_st.reshape(state_ssm.shape))
```

```python
import functools
import math

import jax
import jax.numpy as jnp
from jax import lax
from jax.experimental import pallas as pl
from jax.experimental.pallas import tpu as pltpu

F32 = jnp.float32
BF16 = jnp.bfloat16

D_MODEL = 1024
DEPTH = 4
N_META = 16
WINDOW = 128
BLOCK = 128
HEAD_DIM = 128
ATTN_WIDTH = 2 * D_MODEL
N_HEADS = ATTN_WIDTH // HEAD_DIM
N_KV_HEADS = 4
GQA_GROUP = N_HEADS // N_KV_HEADS
KV_WIDTH = N_KV_HEADS * HEAD_DIM
D_INNER = 2 * D_MODEL
SSM_HEAD_DIM = 64
SSM_HEADS = D_INNER // SSM_HEAD_DIM
D_STATE = 128
SSM_GROUPS = 4
GROUP_WIDTH = D_INNER // SSM_GROUPS
PAIRS_PER_GROUP = SSM_HEADS // SSM_GROUPS // 2
CONV_W = 4
CONV_DIM = D_INNER + 2 * SSM_GROUPS * D_STATE
EPS = 1e-6
NEG = -1e30
LOG2E = math.log2(math.e)
QK_SCALE_LOG2 = HEAD_DIM ** -0.5 * LOG2E

LANES = 128
ROW_TILE = 512
SAMPLE_SEQS = 8
RING_SLOTS = 3
ATTN_SUB_BLOCKS = 8
SSD_SUB_CHUNKS = 4
CONV_HALO = 16
V7X_VMEM_BYTES = 64 * 1024 * 1024
VMEM_LIMIT = V7X_VMEM_BYTES * 7 // 8


def _dot(a, b):
    return jnp.dot(a, b, preferred_element_type=F32)


def _dot_nt(a, b):
    return lax.dot_general(a, b, (((1,), (1,)), ((), ())), preferred_element_type=F32)


def _rmsnorm(x, w):
    return x * lax.rsqrt(jnp.mean(x * x, axis=-1, keepdims=True) + EPS) * w


def _silu(x):
    return x * jax.nn.sigmoid(x)


def _softplus(x):
    return jnp.maximum(x, 0.0) + jnp.log1p(jnp.exp(-jnp.abs(x)))


def _split3(x):
    hi = x.astype(BF16)
    r = x - hi.astype(F32)
    mid = r.astype(BF16)
    lo = (r - mid.astype(F32)).astype(BF16)
    return hi, mid, lo


def _exact_left(p, x):
    hi, mid, lo = _split3(x)
    return _dot(p, hi) + _dot(p, mid) + _dot(p, lo)


def _exact_right(x, e):
    hi, mid, lo = _split3(x)
    return _dot(hi, e) + _dot(mid, e) + _dot(lo, e)


def _params(semantics):
    return pltpu.CompilerParams(dimension_semantics=semantics, vmem_limit_bytes=VMEM_LIMIT)


def _const_spec(shape):
    nd = len(shape)
    return pl.BlockSpec(shape, lambda *_: (0,) * nd)


def _row_spec(cols, tm):
    return pl.BlockSpec((tm, cols), lambda i: (i, 0))


def _layer_spec(w, layer):
    return pl.BlockSpec((None,) + w.shape[1:], lambda *_: (layer, 0, 0))


def _attn_in_body(x_ref, nw_ref, w_ref, qn_ref, kn_ref, q_ref, k_ref, v_ref, g_ref):
    h = _rmsnorm(x_ref[...], nw_ref[...]).astype(BF16)
    q = _dot(h, w_ref[:, :ATTN_WIDTH])
    for hd in range(N_HEADS):
        sl = slice(hd * HEAD_DIM, (hd + 1) * HEAD_DIM)
        q_ref[:, sl] = (_rmsnorm(q[:, sl], qn_ref[...]) * QK_SCALE_LOG2).astype(q_ref.dtype)
    k = _dot(h, w_ref[:, ATTN_WIDTH:ATTN_WIDTH + KV_WIDTH])
    for hd in range(N_KV_HEADS):
        sl = slice(hd * HEAD_DIM, (hd + 1) * HEAD_DIM)
        k_ref[:, sl] = _rmsnorm(k[:, sl], kn_ref[...])
    v_ref[...] = _dot(h, w_ref[:, ATTN_WIDTH + KV_WIDTH:ATTN_WIDTH + 2 * KV_WIDTH])
    g_ref[...] = _dot(h, w_ref[:, ATTN_WIDTH + 2 * KV_WIDTH:]).astype(g_ref.dtype)


def _attn_in(x, nw, depth, w, qn, kn, layer, act_dtype, tm):
    n = x.shape[0]
    return pl.pallas_call(
        _attn_in_body,
        grid=(n // tm,),
        in_specs=[_row_spec(D_MODEL, tm), _layer_spec(nw, depth), _layer_spec(w, layer),
                  _layer_spec(qn, layer), _layer_spec(kn, layer)],
        out_specs=[_row_spec(ATTN_WIDTH, tm), _row_spec(KV_WIDTH, tm), _row_spec(KV_WIDTH, tm),
                   _row_spec(ATTN_WIDTH, tm)],
        out_shape=[jax.ShapeDtypeStruct((n, ATTN_WIDTH), act_dtype),
                   jax.ShapeDtypeStruct((n, KV_WIDTH), F32),
                   jax.ShapeDtypeStruct((n, KV_WIDTH), F32),
                   jax.ShapeDtypeStruct((n, ATTN_WIDTH), act_dtype)],
        compiler_params=_params(("parallel",)),
        name="attn_in",
    )(x, nw, w, qn, kn)


def _attn_block_body(is_meta, n_sub, layer, sink_ref, q_ref, kc_ref, kp_ref, vc_ref, vp_ref, km_ref, vm_ref, o_ref):
    j = pl.program_id(1)
    rows = GQA_GROUP * BLOCK
    rowg = lax.broadcasted_iota(jnp.int32, (rows, 1), 0) // BLOCK
    row = lax.broadcasted_iota(jnp.int32, (rows, 2 * BLOCK), 0) & (BLOCK - 1)
    col = lax.broadcasted_iota(jnp.int32, (rows, 2 * BLOCK), 1)
    cur_ok = (col >= BLOCK) & (col - BLOCK <= row)

    def block(r0, kprev, vprev, n_prev):
        mask = cur_ok
        if n_prev:
            mask = mask | ((col < n_prev) & (col > row - (BLOCK - n_prev)))
        for kv in range(N_KV_HEADS):
            sl = slice(kv * HEAD_DIM, (kv + 1) * HEAD_DIM)
            keys = jnp.concatenate([kprev(sl), kc_ref[r0:r0 + BLOCK, sl]], axis=0).astype(BF16)
            vals = jnp.concatenate([vprev(sl), vc_ref[r0:r0 + BLOCK, sl]], axis=0).astype(BF16)
            base = kv * GQA_GROUP * HEAD_DIM
            q4 = jnp.concatenate(
                [q_ref[r0:r0 + BLOCK, base + g * HEAD_DIM:base + (g + 1) * HEAD_DIM] for g in range(GQA_GROUP)],
                axis=0).astype(BF16)
            sink = jnp.zeros((rows, 1), F32)
            for g in range(GQA_GROUP):
                sink = jnp.where(rowg == g, sink_ref[layer, kv * GQA_GROUP + g] * LOG2E, sink)
            s = jnp.where(mask, _dot_nt(q4, keys), NEG)
            m = jnp.maximum(jnp.max(s, axis=-1, keepdims=True), sink)
            p = jnp.exp2(s - m)
            denom = jnp.sum(p, axis=-1, keepdims=True) + jnp.exp2(sink - m)
            o = _dot(p.astype(BF16), vals) / denom
            for g in range(GQA_GROUP):
                o_ref[r0:r0 + BLOCK, base + g * HEAD_DIM:base + (g + 1) * HEAD_DIM] = (
                    o[g * BLOCK:(g + 1) * BLOCK].astype(o_ref.dtype))

    if is_meta:
        block(0, lambda sl: km_ref[:, sl], lambda sl: vm_ref[:, sl], 0)
    else:
        @pl.when(j == 0)
        def _():
            block(0, lambda sl: km_ref[:, sl], lambda sl: vm_ref[:, sl], N_META)

        @pl.when(j > 0)
        def _():
            block(0, lambda sl: kp_ref[:, sl], lambda sl: vp_ref[:, sl], BLOCK)

    for sub in range(1, n_sub):
        p0 = (sub - 1) * BLOCK
        block(sub * BLOCK, lambda sl, p0=p0: kc_ref[p0:p0 + BLOCK, sl], lambda sl, p0=p0: vc_ref[p0:p0 + BLOCK, sl],
              BLOCK)


def _attn_blocks(q, k, v, k_small, v_small, sinks, layer, n_batch, n_blocks, n_sub, first_block, meta_block,
                 is_meta, out_dtype):
    assert n_blocks % n_sub == 0 and first_block % n_sub == 0
    n_steps = n_blocks // n_sub
    rows = n_sub * BLOCK
    cur = lambda b, j: (first_block // n_sub + b * n_steps + j, 0)
    prev = lambda b, j: (jnp.maximum(first_block + (b * n_steps + j) * n_sub - 1, 0), 0)
    meta = lambda b, j: (meta_block, 0)
    out = lambda b, j: (b * n_steps + j, 0)
    return pl.pallas_call(
        functools.partial(_attn_block_body, is_meta, n_sub, layer),
        grid=(n_batch, n_steps),
        in_specs=[pl.BlockSpec(memory_space=pltpu.SMEM),
                  pl.BlockSpec((rows, ATTN_WIDTH), cur),
                  pl.BlockSpec((rows, KV_WIDTH), cur), pl.BlockSpec((BLOCK, KV_WIDTH), prev),
                  pl.BlockSpec((rows, KV_WIDTH), cur), pl.BlockSpec((BLOCK, KV_WIDTH), prev),
                  pl.BlockSpec((BLOCK, KV_WIDTH), meta), pl.BlockSpec((BLOCK, KV_WIDTH), meta)],
        out_specs=pl.BlockSpec((rows, ATTN_WIDTH), out),
        out_shape=jax.ShapeDtypeStruct((n_batch * n_blocks * BLOCK, ATTN_WIDTH), out_dtype),
        compiler_params=_params(("parallel", "parallel")),
        name="attn_meta" if is_meta else "attn_prompt",
    )(sinks, q, k, k, v, v, k_small, v_small)


def _attn_sample_body(fill_layers, layer, sink_ref, q_ref, k_ref, v_ref, ck_ref, cv_ref, *rest):
    o_ref, nk_ref, nv_ref = rest[-3:]
    if fill_layers:
        for other in range(fill_layers):
            if other != layer:
                nk_ref[other] = jnp.zeros(nk_ref.shape[1:], F32)
                nv_ref[other] = jnp.zeros(nv_ref.shape[1:], F32)
        nk_ref, nv_ref = nk_ref.at[layer], nv_ref.at[layer]
    n_seq, rows, _ = q_ref.shape
    n_cache = ck_ref.shape[1]
    n_fresh = k_ref.shape[1]
    wbuf = n_cache // N_KV_HEADS
    per_kv = rows // N_KV_HEADS
    row = lax.broadcasted_iota(jnp.int32, (rows, n_cache + n_fresh), 0)
    col = lax.broadcasted_iota(jnp.int32, (rows, n_cache + n_fresh), 1)
    qpos = (row % per_kv) // GQA_GROUP
    kpos = jnp.where(col < n_cache, col // N_KV_HEADS - wbuf, (col - n_cache) // N_KV_HEADS)
    diff = qpos - kpos
    mask = (col % N_KV_HEADS == row // per_kv) & (diff >= 0) & (diff < WINDOW)
    row1 = lax.broadcasted_iota(jnp.int32, (rows, 1), 0)
    sink = jnp.zeros((rows, 1), F32)
    for hd in range(N_HEADS):
        sink = jnp.where((row1 // per_kv == hd // GQA_GROUP) & (row1 % GQA_GROUP == hd % GQA_GROUP),
                         sink_ref[layer, hd] * LOG2E, sink)
    for s in range(n_seq):
        keys = jnp.concatenate([ck_ref[s], k_ref[s]], axis=0).astype(BF16)
        vals = jnp.concatenate([cv_ref[s], v_ref[s]], axis=0).astype(BF16)
        sc = _dot_nt(q_ref[s].astype(BF16), keys)
        sc = jnp.where(mask, sc, NEG)
        m = jnp.maximum(jnp.max(sc, axis=-1, keepdims=True), sink)
        p = jnp.exp2(sc - m)
        denom = jnp.sum(p, axis=-1, keepdims=True) + jnp.exp2(sink - m)
        o_ref[s] = _dot(p.astype(BF16), vals) / denom
        nk_ref[s, 0:n_cache - n_fresh, :] = ck_ref[s, n_fresh:, :]
        nk_ref[s, n_cache - n_fresh:, :] = k_ref[s]
        nv_ref[s, 0:n_cache - n_fresh, :] = cv_ref[s, n_fresh:, :]
        nv_ref[s, n_cache - n_fresh:, :] = v_ref[s]


def _attn_sample(q, k, v, ck, cv, sinks, layer, nk_prev=None, nv_prev=None):
    sb = SAMPLE_SEQS
    n_seq = q.shape[0]
    n_layers = ck.shape[0]
    blk = lambda a: pl.BlockSpec((sb,) + a.shape[1:], lambda i: (i,) + (0,) * (a.ndim - 1))
    cache_spec = pl.BlockSpec((None, sb) + ck.shape[2:], lambda i: (layer, i, 0, 0))
    in_specs = [pl.BlockSpec(memory_space=pltpu.SMEM), blk(q), blk(k), blk(v), cache_spec, cache_spec]
    args = [sinks, q, k, v, ck, cv]
    if nk_prev is None:
        aliases, fill_layers = {}, n_layers
        new_spec = pl.BlockSpec((n_layers, sb) + ck.shape[2:], lambda i: (0, i, 0, 0))
    else:
        in_specs += [pl.BlockSpec(memory_space=pl.ANY)] * 2
        aliases, fill_layers = {len(args): 1, len(args) + 1: 2}, None
        new_spec = cache_spec
        args += [nk_prev, nv_prev]
    return pl.pallas_call(
        functools.partial(_attn_sample_body, fill_layers, layer),
        grid=(n_seq // sb,),
        in_specs=in_specs,
        out_specs=[blk(q), new_spec, new_spec],
        out_shape=[jax.ShapeDtypeStruct(q.shape, F32),
                   jax.ShapeDtypeStruct(ck.shape, F32),
                   jax.ShapeDtypeStruct(cv.shape, F32)],
        input_output_aliases=aliases,
        compiler_params=_params(("parallel",)),
        name="attn_sample",
    )(*args)


def _gated(v_ref, gate_ref):
    if gate_ref.dtype == BF16:
        gate = gate_ref[...]
        return v_ref[...] * (gate * jax.nn.sigmoid(gate))
    return v_ref[...].astype(F32) * _silu(gate_ref[...].astype(F32))


def _attn_out_body(x_ref, o_ref, g_ref, w_ref, y_ref):
    y_ref[...] = x_ref[...] + _dot(_gated(o_ref, g_ref).astype(BF16), w_ref[...].astype(BF16))


def _ring_body(inner, hbm0, hbm1, hbm2, *rest):
    *other, buf0, buf1, buf2, sem = rest
    hbms, bufs = (hbm0, hbm1, hbm2), (buf0, buf1, buf2)
    tm = buf0.shape[1]
    step, n_steps = pl.program_id(0), pl.num_programs(0)

    def copies(t):
        slot = t % RING_SLOTS
        rows = pl.ds(pl.multiple_of(t * tm, tm), tm)
        return [pltpu.make_async_copy(h.at[rows, :], b.at[slot], sem.at[i, slot])
                for i, (h, b) in enumerate(zip(hbms, bufs))]

    @pl.when(step == 0)
    def _():
        for t in range(RING_SLOTS - 1):
            for c in copies(t):
                c.start()

    @pl.when(step + RING_SLOTS - 1 < n_steps)
    def _():
        for c in copies(step + RING_SLOTS - 1):
            c.start()

    for c in copies(step):
        c.wait()
    slot = step % RING_SLOTS
    inner(buf0.at[slot], buf1.at[slot], buf2.at[slot], *other)


def _ring_call(inner, rows3, others, other_specs, tm, name):
    n = rows3[0].shape[0]
    assert n // tm >= RING_SLOTS
    return pl.pallas_call(
        functools.partial(_ring_body, inner),
        grid=(n // tm,),
        in_specs=[pl.BlockSpec(memory_space=pl.ANY)] * 3 + other_specs,
        out_specs=_row_spec(D_MODEL, tm),
        out_shape=jax.ShapeDtypeStruct((n, D_MODEL), F32),
        scratch_shapes=[pltpu.VMEM((RING_SLOTS, tm, a.shape[1]), a.dtype) for a in rows3]
        + [pltpu.SemaphoreType.DMA((3, RING_SLOTS))],
        compiler_params=_params(("arbitrary",)),
        name=name,
    )(*rows3, *others)


def _attn_out(x, o, g, w, layer, tm):
    n = x.shape[0]
    if n // tm >= RING_SLOTS:
        return _ring_call(_attn_out_body, (x, o, g), (w,), [_layer_spec(w, layer)], tm, "attn_out")
    return pl.pallas_call(
        _attn_out_body,
        grid=(n // tm,),
        in_specs=[_row_spec(D_MODEL, tm), _row_spec(ATTN_WIDTH, tm), _row_spec(ATTN_WIDTH, tm),
                  _layer_spec(w, layer)],
        out_specs=_row_spec(D_MODEL, tm),
        out_shape=jax.ShapeDtypeStruct(x.shape, F32),
        compiler_params=_params(("parallel",)),
        name="attn_out",
    )(x, o, g, w)


def _ssm_in_body(x_ref, nw_ref, w_ref, z_ref, xbc_ref, dt_ref):
    h = _rmsnorm(x_ref[...], nw_ref[...]).astype(BF16)
    z_ref[...] = _dot(h, w_ref[:, :D_INNER]).astype(z_ref.dtype)
    xbc_ref[...] = _dot(h, w_ref[:, D_INNER:D_INNER + CONV_DIM])
    dt_ref[:, :SSM_HEADS] = _dot(h, w_ref[:, D_INNER + CONV_DIM:])
    dt_ref[:, SSM_HEADS:] = jnp.zeros((dt_ref.shape[0], LANES - SSM_HEADS), F32)


def _ssm_in(x, nw, depth, w, layer, act_dtype, tm):
    n = x.shape[0]
    return pl.pallas_call(
        _ssm_in_body,
        grid=(n // tm,),
        in_specs=[_row_spec(D_MODEL, tm), _layer_spec(nw, depth), _layer_spec(w, layer)],
        out_specs=[_row_spec(D_INNER, tm), _row_spec(CONV_DIM, tm), _row_spec(LANES, tm)],
        out_shape=[jax.ShapeDtypeStruct((n, D_INNER), act_dtype),
                   jax.ShapeDtypeStruct((n, CONV_DIM), F32),
                   jax.ShapeDtypeStruct((n, LANES), F32)],
        compiler_params=_params(("parallel",)),
        name="ssm_in",
    )(x, nw, w)


def _lane_bcast(x, h):
    return jnp.broadcast_to(x[:, h:h + 1], (x.shape[0], LANES))


def _row_bcast(x, h):
    return jnp.broadcast_to(x[h:h + 1, :], (LANES, x.shape[1]))


def _pair_blockdiag(xp):
    lane = lax.broadcasted_iota(jnp.int32, xp.shape, 1)
    xb = xp.astype(BF16)
    zero = jnp.zeros_like(xb)
    return jnp.concatenate([jnp.where(lane < SSM_HEAD_DIM, xb, zero),
                            jnp.where(lane >= SSM_HEAD_DIM, xb, zero)], axis=0)


def _pair_select(a0, a1):
    lane = lax.broadcasted_iota(jnp.int32, a0.shape, 1)
    return jnp.where(lane < SSM_HEAD_DIM, a0, a1)


def _intra_pair(cb, cmask, a2, a2_t, dt_t, pair):
    ms = []
    for hh in range(2):
        h = 2 * pair + hh
        seg = _lane_bcast(a2, h) - _row_bcast(a2_t, h)
        dec = jnp.exp2(jnp.where(cmask, seg, NEG))
        ms.append((cb * dec * _row_bcast(dt_t, h)).astype(BF16))
    return jnp.concatenate(ms, axis=1)


def _pair_out_scale(a2, pair):
    return jnp.exp2(_pair_select(_lane_bcast(a2, 2 * pair), _lane_bcast(a2, 2 * pair + 1)))


def _ssd_chunk_body(is_meta, n_sub, xbc_ref, dt_ref, mtile_ref, init_ref, cw_ref, cbias_ref, dtb_ref, alog_ref,
                    dsk_ref, e_ref, y_ref, st_out_ref, halo_ref, act_ref, st_ref):
    j = pl.program_id(1)
    n_steps = pl.num_programs(1)
    q = BLOCK
    halo = CONV_HALO

    @pl.when(j == 0)
    def _():
        if is_meta:
            halo_ref[...] = jnp.zeros((halo, CONV_DIM), F32)
            st_ref[...] = jnp.zeros(st_ref.shape, F32)
        else:
            halo_ref[...] = mtile_ref[0:halo, :]
            for pair in range(SSM_HEADS // 2):
                ps = slice(pair * LANES, (pair + 1) * LANES)
                st_ref[:, ps] = init_ref[0, ps, :].T

    def chunk(r0):
        n_sh = CONV_W - 1
        r = lax.broadcasted_iota(jnp.int32, (q, n_sh * q), 0)
        c = lax.broadcasted_iota(jnp.int32, (q, n_sh * q), 1)
        sel = (c % q == r - (c // q + 1)).astype(BF16)
        top = 8
        rh = lax.broadcasted_iota(jnp.int32, (top, n_sh * halo), 0)
        ch = lax.broadcasted_iota(jnp.int32, (top, n_sh * halo), 1)
        sel_halo = (ch % halo == halo + rh - (ch // halo + 1)).astype(BF16)

        def taps(x, cs):
            return jnp.concatenate([(cw_ref[CONV_W - 1 - k:CONV_W - k, cs] * x).astype(BF16)
                                    for k in range(1, CONV_W)], axis=0)

        width = 512
        for cc in range(CONV_DIM // width):
            cs = slice(cc * width, (cc + 1) * width)
            xc = xbc_ref[pl.ds(r0, q), cs]
            conv = cbias_ref[:, cs] + cw_ref[CONV_W - 1:CONV_W, cs] * xc + _dot(sel, taps(xc, cs))
            act_ref[:, cs] = _silu(conv)
            conv_top = conv[0:top, :] + _dot(sel_halo, taps(halo_ref[:, cs], cs))
            act_ref[0:top, cs] = _silu(conv_top)
        halo_ref[...] = xbc_ref[pl.ds(r0 + q - halo, halo), :]

        ri = lax.broadcasted_iota(jnp.int32, (q, q), 0)
        ci = lax.broadcasted_iota(jnp.int32, (q, q), 1)
        dt_ok = ci < SSM_HEADS
        if is_meta:
            dt_ok = dt_ok & (ri < N_META)
        dt = jnp.where(dt_ok, _softplus(dt_ref[pl.ds(r0, q), :] + dtb_ref[...]), 0.0)
        a = dt * (-jnp.exp(alog_ref[...]))
        cmask = ri >= ci
        a2 = _exact_left(cmask.astype(BF16), a) * LOG2E
        a2_t = a2.T
        dt_t = dt.T
        w_t = dt_t * jnp.exp2(jnp.broadcast_to(a2_t[:, q - 1:q], (q, q)) - a2_t)
        dec_rows = _exact_right(jnp.exp2(jnp.broadcast_to(a2[q - 1:q, :], (8, LANES))), e_ref[...])[0:1, :]
        rowi = lax.broadcasted_iota(jnp.int32, (q, LANES), 0)

        for g in range(SSM_GROUPS):
            gs = slice(D_INNER + g * D_STATE, D_INNER + (g + 1) * D_STATE)
            bg = act_ref[:, gs]
            cg = act_ref[:, SSM_GROUPS * D_STATE + gs.start:SSM_GROUPS * D_STATE + gs.stop].astype(BF16)
            cb = _dot_nt(cg, bg.astype(BF16))
            bg_t = bg.T
            y_off = _dot(cg, st_ref[:, g * GROUP_WIDTH:(g + 1) * GROUP_WIDTH].astype(BF16))
            for pr in range(PAIRS_PER_GROUP):
                pair = g * PAIRS_PER_GROUP + pr
                ps = slice(pair * LANES, (pair + 1) * LANES)
                xp = act_ref[:, ps]
                if is_meta:
                    xp = jnp.where(rowi < N_META, xp, 0.0)
                xbd = _pair_blockdiag(xp)
                y = _dot(_intra_pair(cb, cmask, a2, a2_t, dt_t, pair), xbd)
                st = st_ref[:, ps]
                y = y + y_off[:, pr * LANES:(pr + 1) * LANES] * _pair_out_scale(a2, pair)
                wn = jnp.concatenate([(bg_t * _row_bcast(w_t, 2 * pair)).astype(BF16),
                                      (bg_t * _row_bcast(w_t, 2 * pair + 1)).astype(BF16)], axis=1)
                st_ref[:, ps] = st * dec_rows[:, ps] + _dot(wn, xbd)
                y_ref[pl.ds(r0, q), ps] = (y + dsk_ref[:, ps] * xp).astype(y_ref.dtype)

    if n_sub == 1:
        chunk(0)
    else:
        def sub_chunk(sub, carry):
            chunk(pl.multiple_of(sub * q, q))
            return carry

        lax.fori_loop(0, n_sub, sub_chunk, 0)

    @pl.when(j == n_steps - 1)
    def _():
        for pair in range(SSM_HEADS // 2):
            ps = slice(pair * LANES, (pair + 1) * LANES)
            st_out_ref[0, ps, :] = st_ref[:, ps].T


def _ssd_chunks(xbc, dt, xbc_small, init, cw, cbias, dtb, alog, dsk, emat, layer, n_batch, n_chunks, n_sub,
                first_block, meta_block, is_meta, out_dtype):
    assert n_chunks % n_sub == 0 and first_block % n_sub == 0
    n_steps = n_chunks // n_sub
    rows = n_sub * BLOCK
    cur = lambda b, j: (first_block // n_sub + b * n_steps + j, 0)
    out = lambda b, j: (b * n_steps + j, 0)
    return pl.pallas_call(
        functools.partial(_ssd_chunk_body, is_meta, n_sub),
        grid=(n_batch, n_steps),
        in_specs=[pl.BlockSpec((rows, CONV_DIM), cur), pl.BlockSpec((rows, LANES), cur),
                  pl.BlockSpec((BLOCK, CONV_DIM), lambda b, j: (meta_block, 0)), _const_spec(init.shape),
                  _layer_spec(cw, layer), _layer_spec(cbias, layer), _layer_spec(dtb, layer),
                  _layer_spec(alog, layer), _layer_spec(dsk, layer), _const_spec(emat.shape)],
        out_specs=[pl.BlockSpec((rows, D_INNER), out),
                   pl.BlockSpec((1, D_INNER, D_STATE), lambda b, j: (b, 0, 0))],
        out_shape=[jax.ShapeDtypeStruct((n_batch * n_chunks * BLOCK, D_INNER), out_dtype),
                   jax.ShapeDtypeStruct((n_batch, D_INNER, D_STATE), F32)],
        scratch_shapes=[pltpu.VMEM((CONV_HALO, CONV_DIM), F32), pltpu.VMEM((BLOCK, CONV_DIM), F32),
                        pltpu.VMEM((D_STATE, D_INNER), F32)],
        compiler_params=_params(("parallel", "arbitrary")),
        name="ssd_meta" if is_meta else "ssd_prompt",
    )(xbc, dt, xbc_small, init, cw, cbias, dtb, alog, dsk, emat)


def _ssd_sample_body(*refs):
    nst_ref = refs[-2]

    last = pl.num_programs(1) - 1

    @pl.when(pl.program_id(1) == last)
    def _():
        _ssd_sample_step(*refs)

    @pl.when(pl.program_id(1) < last)
    def _():
        nst_ref[...] = jnp.zeros(nst_ref.shape, F32)


def _ssd_sample_step(xbc_ref, dt_ref, cs_ref, st_ref, cw_ref, cbias_ref, dtb_ref, alog_ref, dsk_ref, e_ref,
                     *rest):
    y_ref, nst_ref, yt_ref = rest[-3:]
    n_seq = st_ref.shape[0]
    n_new = xbc_ref.shape[0] // n_seq
    n_cs = CONV_W - 1
    qr = n_seq * n_new
    q = LANES
    kpad = 64
    assert qr + n_seq * n_cs <= kpad and n_new >= n_cs

    x_new = xbc_ref[...]
    xc = jnp.concatenate([x_new, cs_ref[...], jnp.zeros((kpad - qr - n_seq * n_cs, CONV_DIM), F32)], axis=0)
    r = lax.broadcasted_iota(jnp.int32, (qr, kpad), 0)
    c = lax.broadcasted_iota(jnp.int32, (qr, kpad), 1)
    s_of_r = r // n_new
    t_of_r = r % n_new
    conv = cbias_ref[...] + cw_ref[CONV_W - 1:CONV_W, :] * x_new
    for k in range(1, CONV_W):
        target = jnp.where(t_of_r >= k, r - k, qr + n_cs * s_of_r + t_of_r + n_cs - k)
        conv = conv + cw_ref[CONV_W - 1 - k:CONV_W - k, :] * _exact_left((c == target).astype(BF16), xc)
    act = jnp.concatenate([_silu(conv), jnp.zeros((q - qr, CONV_DIM), F32)], axis=0)
    xh = act[:, :D_INNER]
    bm = act[:, D_INNER:D_INNER + SSM_GROUPS * D_STATE]
    cm = act[:, D_INNER + SSM_GROUPS * D_STATE:]
    ri = lax.broadcasted_iota(jnp.int32, (q, q), 0)
    ci = lax.broadcasted_iota(jnp.int32, (q, q), 1)
    dt = jnp.concatenate([_softplus(dt_ref[...] + dtb_ref[...]), jnp.zeros((q - qr, LANES), F32)], axis=0)
    dt = jnp.where(ci < SSM_HEADS, dt, 0.0)
    a = dt * (-jnp.exp(alog_ref[...]))
    cmask = (ri >= ci) & (ri // n_new == ci // n_new) & (ri < qr)
    a_cum = _exact_left(cmask.astype(BF16), a)
    a2 = a_cum * LOG2E
    a2_t = a2.T
    dt_t = dt.T
    last = ((ci == (ri // n_new) * n_new + n_new - 1) & (ri < qr)).astype(BF16)
    a2_last = _exact_left(last, a_cum) * LOG2E
    w_exp = _exact_right(dt * jnp.exp2(a2_last - a2), e_ref[...])
    dec_exp = _exact_right(jnp.exp2(a2_last), e_ref[...])
    xw = xh * w_exp

    yt_ref[...] = jnp.zeros(yt_ref.shape, F32)
    colseq = ci // n_new
    rowseq = ri // n_new
    for g in range(SSM_GROUPS):
        gs = slice(g * D_STATE, (g + 1) * D_STATE)
        hs = slice(g * GROUP_WIDTH, (g + 1) * GROUP_WIDTH)
        cg_t = cm[:, gs].T.astype(BF16)
        bg = bm[:, gs].astype(BF16)
        xw_t = jnp.concatenate(
            [xw[:, g * GROUP_WIDTH + i * LANES:g * GROUP_WIDTH + (i + 1) * LANES].T
             for i in range(GROUP_WIDTH // LANES)], axis=0).astype(BF16)
        dec_t = jnp.concatenate(
            [dec_exp[:, g * GROUP_WIDTH + i * LANES:g * GROUP_WIDTH + (i + 1) * LANES].T
             for i in range(GROUP_WIDTH // LANES)], axis=0)
        zero = jnp.zeros((q, q), BF16)
        for s in range(n_seq):
            st0 = st_ref[s, hs, :]
            yt_ref[hs, :] += _dot(st0.astype(BF16), jnp.where(colseq == s, cg_t, zero))
            inc = _dot(xw_t, jnp.where(rowseq == s, bg, zero))
            dcol = jnp.broadcast_to(dec_t[:, s * n_new:s * n_new + 1], (GROUP_WIDTH, q))
            nst_ref[s, hs, :] = st0 * dcol + inc

    for g in range(SSM_GROUPS):
        gs = slice(g * D_STATE, (g + 1) * D_STATE)
        cb = _dot_nt(cm[:, gs].astype(BF16), bm[:, gs].astype(BF16))
        for pr in range(PAIRS_PER_GROUP):
            pair = g * PAIRS_PER_GROUP + pr
            ps = slice(pair * LANES, (pair + 1) * LANES)
            xp = xh[:, ps]
            y = _dot(_intra_pair(cb, cmask, a2, a2_t, dt_t, pair), _pair_blockdiag(xp))
            y = y + yt_ref[ps, :].T * _pair_out_scale(a2, pair) + dsk_ref[:, ps] * xp
            y_ref[:, ps] = y[0:qr, :]


def _ssd_sample(xbc, dt, cs, st, cw, cbias, dtb, alog, dsk, emat, n_new, layer, nst_prev=None):
    n_layers, n_seq = st.shape[0], st.shape[1]
    sb = SAMPLE_SEQS
    n_pass = n_layers if nst_prev is None else 1
    rows = lambda cols, per: pl.BlockSpec((sb * per, cols), lambda i, t: (i, 0))
    st_spec = pl.BlockSpec((None, sb, D_INNER, D_STATE), lambda i, t: (layer, i, 0, 0))
    nst_spec = pl.BlockSpec((None, sb, D_INNER, D_STATE), lambda i, t: ((layer + t + 1) % n_pass, i, 0, 0))
    if nst_prev is not None:
        nst_spec = st_spec
    in_specs = [rows(CONV_DIM, n_new), rows(LANES, n_new), rows(CONV_DIM, CONV_W - 1), st_spec,
                _layer_spec(cw, layer), _layer_spec(cbias, layer), _layer_spec(dtb, layer),
                _layer_spec(alog, layer), _layer_spec(dsk, layer), _const_spec(emat.shape)]
    args = [xbc, dt, cs, st, cw, cbias, dtb, alog, dsk, emat]
    aliases = {}
    if nst_prev is not None:
        in_specs.append(pl.BlockSpec(memory_space=pl.ANY))
        aliases = {len(args): 1}
        args.append(nst_prev)
    return pl.pallas_call(
        _ssd_sample_body,
        grid=(n_seq // sb, n_pass),
        in_specs=in_specs,
        out_specs=[rows(D_INNER, n_new), nst_spec],
        out_shape=[jax.ShapeDtypeStruct((n_seq * n_new, D_INNER), F32),
                   jax.ShapeDtypeStruct(st.shape, F32)],
        input_output_aliases=aliases,
        scratch_shapes=[pltpu.VMEM((D_INNER, LANES), F32)],
        compiler_params=_params(("parallel", "arbitrary")),
        name="ssd_sample",
    )(*args)


def _ssm_out_body(x_ref, y_ref, z_ref, nw_ref, w_ref, o_ref):
    y = _gated(y_ref, z_ref).astype(F32)
    parts = []
    for g in range(SSM_GROUPS):
        yg = y[:, g * GROUP_WIDTH:(g + 1) * GROUP_WIDTH]
        parts.append(yg * lax.rsqrt(jnp.mean(yg * yg, axis=-1, keepdims=True) + EPS))
    yn = (jnp.concatenate(parts, axis=1) * nw_ref[...]).astype(BF16)
    o_ref[...] = x_ref[...] + _dot(yn, w_ref[...].astype(BF16))


def _ssm_out(x, y, z, nw, w, layer, tm):
    n = x.shape[0]
    if n // tm >= RING_SLOTS:
        return _ring_call(_ssm_out_body, (x, y, z), (nw, w), [_layer_spec(nw, layer), _layer_spec(w, layer)], tm,
                          "ssm_out")
    return pl.pallas_call(
        _ssm_out_body,
        grid=(n // tm,),
        in_specs=[_row_spec(D_MODEL, tm), _row_spec(D_INNER, tm), _row_spec(D_INNER, tm),
                  _layer_spec(nw, layer), _layer_spec(w, layer)],
        out_specs=_row_spec(D_MODEL, tm),
        out_shape=jax.ShapeDtypeStruct(x.shape, F32),
        compiler_params=_params(("parallel",)),
        name="ssm_out",
    )(x, y, z, nw, w)


def _rows3(p, width=None):
    p = p.astype(F32)
    if width is not None:
        p = jnp.pad(p, ((0, 0), (0, width - p.shape[1])))
    return p[:, None, :]


def kernel(x_prompt, x_sample, cache_k, cache_v, state_conv, state_ssm, meta_tokens, norm_w,
           w_attn_in, q_norm_w, k_norm_w, attn_sinks, w_attn_out, w_ssm_in, conv_w, conv_b,
           dt_bias, a_log, d_skip, ssm_norm_w, w_ssm_out):
    n_batch, seq, _ = x_prompt.shape
    n_seq, n_new, _ = x_sample.shape
    wbuf = cache_k.shape[2]
    n_blocks = seq // BLOCK
    n_rows = n_seq * n_new
    assert seq % BLOCK == 0 and wbuf == WINDOW and n_rows % BLOCK == 0 and n_new == 4
    meta_block = n_rows // BLOCK
    small_rows = n_rows + BLOCK

    xp = x_prompt.reshape(n_batch * seq, D_MODEL)
    xs = jnp.concatenate([x_sample.reshape(n_rows, D_MODEL), meta_tokens.astype(F32),
                          jnp.zeros((BLOCK - N_META, D_MODEL), F32)], axis=0)

    emat = jnp.pad(jnp.repeat(jnp.eye(SSM_HEADS, dtype=BF16), SSM_HEAD_DIM, axis=1),
                   ((0, LANES - SSM_HEADS), (0, 0)))
    zero_state = jnp.zeros((1, D_INNER, D_STATE), F32)

    w_attn_in_b, w_ssm_in_b = w_attn_in.astype(BF16), w_ssm_in.astype(BF16)
    w_attn_out_b, w_ssm_out_b = w_attn_out, w_ssm_out
    n_attn, n_ssm = cache_k.shape[0], state_ssm.shape[0]
    ck_all = cache_k.reshape(n_attn, n_seq, wbuf * N_KV_HEADS, HEAD_DIM)
    cv_all = cache_v.reshape(n_attn, n_seq, wbuf * N_KV_HEADS, HEAD_DIM)
    st_all = state_ssm.reshape(n_ssm, n_seq, D_INNER, D_STATE)
    new_ck = new_cv = new_st = None

    nw, qn, kn, sinks = _rows3(norm_w), _rows3(q_norm_w), _rows3(k_norm_w), attn_sinks.astype(F32)
    snw = _rows3(ssm_norm_w)
    consts = (conv_w.astype(F32), _rows3(conv_b), _rows3(dt_bias, LANES), _rows3(a_log, LANES),
              _rows3(jnp.repeat(d_skip, SSM_HEAD_DIM, axis=1)), emat)

    kp_l, vp_l, cp_l, sp_l, cs_l = [], [], [], [], []
    for i in range(DEPTH):
        l = i // 2
        if i % 2 == 0:
            qs, ks, vs, gs = _attn_in(xs, nw, i, w_attn_in_b, qn, kn, l, F32, small_rows)
            o_meta = _attn_blocks(qs, ks, vs, ks, vs, sinks, l, 1, 1, 1, meta_block, meta_block, True, F32)
            qq = qs[:n_rows].reshape(n_seq, n_new, N_KV_HEADS, GQA_GROUP * HEAD_DIM).transpose(0, 2, 1, 3)
            o, new_ck, new_cv = _attn_sample(qq.reshape(n_seq, N_HEADS * n_new, HEAD_DIM),
                                             ks[:n_rows].reshape(n_seq, n_new * N_KV_HEADS, HEAD_DIM),
                                             vs[:n_rows].reshape(n_seq, n_new * N_KV_HEADS, HEAD_DIM),
                                             ck_all, cv_all, sinks, l, new_ck, new_cv)
            o = o.reshape(n_seq, N_KV_HEADS, n_new, GQA_GROUP * HEAD_DIM).transpose(0, 2, 1, 3)
            o = jnp.concatenate([o.reshape(n_rows, ATTN_WIDTH), o_meta], axis=0)

            q, k, v, g = _attn_in(xp, nw, i, w_attn_in_b, qn, kn, l, BF16, ROW_TILE)
            op = _attn_blocks(q, k, v, ks, vs, sinks, l, n_batch, n_blocks, ATTN_SUB_BLOCKS, 0, meta_block, False,
                              BF16)
            xp = _attn_out(xp, op, g, w_attn_out_b, l, ROW_TILE)
            xs = _attn_out(xs, o, gs, w_attn_out_b, l, small_rows)
            kp_l.append(k.reshape(n_batch, seq, KV_WIDTH)[:, -WINDOW:].reshape(n_batch, WINDOW, N_KV_HEADS, HEAD_DIM))
            vp_l.append(v.reshape(n_batch, seq, KV_WIDTH)[:, -WINDOW:].reshape(n_batch, WINDOW, N_KV_HEADS, HEAD_DIM))
        else:
            zs, xbcs, dts = _ssm_in(xs, nw, i, w_ssm_in_b, l, F32, small_rows)
            y_meta, st_meta = _ssd_chunks(xbcs, dts, xbcs, zero_state, *consts, l, 1, 1, 1, meta_block, meta_block,
                                          True, F32)
            y, new_st = _ssd_sample(xbcs, dts, state_conv[l].reshape(n_seq * (CONV_W - 1), CONV_DIM), st_all,
                                    *consts, n_new, l, new_st)
            y = jnp.concatenate([y, y_meta], axis=0)

            z, xbc, dt = _ssm_in(xp, nw, i, w_ssm_in_b, l, BF16, ROW_TILE)
            yp, st = _ssd_chunks(xbc, dt, xbcs, st_meta, *consts, l, n_batch, n_blocks, SSD_SUB_CHUNKS, 0,
                                 meta_block, False, BF16)
            xp = _ssm_out(xp, yp, z, snw, w_ssm_out_b, l, ROW_TILE)
            xs = _ssm_out(xs, y, zs, snw, w_ssm_out_b, l, small_rows)
            cp_l.append(xbc.reshape(n_batch, seq, CONV_DIM)[:, -(CONV_W - 1):])
            sp_l.append(st.reshape(n_batch, SSM_HEADS, SSM_HEAD_DIM, D_STATE))
            cs_l.append(xbcs[:n_rows].reshape(n_seq, n_new, CONV_DIM)[:, -(CONV_W - 1):])

    y_prompt = xp.reshape(n_batch, seq, D_MODEL)
    y_sample = xs[:n_rows].reshape(n_seq, n_new, D_MODEL)
    return (y_prompt, y_sample,
            jnp.stack(kp_l), jnp.stack(vp_l), jnp.stack(cp_l), jnp.stack(sp_l),
            new_ck.reshape(cache_k.shape), new_cv.reshape(cache_v.shape), jnp.stack(cs_l),
            new_st.reshape(state_ssm.shape))
```
